```python
import jax, jax.numpy as jnp
from jax import lax
import numpy as np

D_MODEL = 1024
BATCH = 8
SEQ = 4096
DEPTH = 1

MEM_LEN = 256
MLA_HEADS = 8
MLA_NOPE = 64
MLA_ROPE = 32
MLA_V = 64
MLA_Q_RANK = 256
MLA_KV_RANK = 128
FOX_HEADS = 8
FOX_HEAD_DIM = 64
MIX_WIDTH = MLA_HEADS * MLA_V + FOX_HEADS * FOX_HEAD_DIM
IN_COLS = MLA_Q_RANK + MLA_KV_RANK + MLA_ROPE + 3 * FOX_HEADS * FOX_HEAD_DIM + FOX_HEADS
X_HEADS = 4
X_HEAD_DIM = D_MODEL // X_HEADS
N_EXPERTS = 32
TOP_K = 4
D_EXPERT = D_MODEL
SWIGLU_LIMIT = 7.0
SWIGLU_ALPHA = 1.702
ROPE_THETA = 10000.0
Q_BLOCK = 128
EPS = 1e-6

kernel_name = "hymba_mla_fox_memxattn_moe"


def rmsnorm(x, g):
    xf = x.astype(jnp.float32)
    y = xf * lax.rsqrt(jnp.mean(xf * xf, axis=-1, keepdims=True) + EPS)
    return (y * g.astype(jnp.float32)).astype(x.dtype)


def rope(x, pos):
    half = x.shape[-1] // 2
    inv = ROPE_THETA ** (-jnp.arange(half, dtype=jnp.float32) / half)
    ang = pos.astype(jnp.float32)[:, None] * inv[None, :]
    cos = jnp.cos(ang).astype(x.dtype)
    sin = jnp.sin(ang).astype(x.dtype)
    x1, x2 = x[..., :half], x[..., half:]
    return jnp.concatenate([x1 * cos - x2 * sin, x2 * cos + x1 * sin], axis=-1)


def causal_block_attention(q, k, v, scale, log_decay=None):
    B, H, S, Dk = q.shape
    nb = S // Q_BLOCK
    q_blocks = q.reshape(B, H, nb, Q_BLOCK, Dk).transpose(2, 0, 1, 3, 4)
    key_pos = jnp.arange(S)
    if log_decay is None:
        xs = (jnp.arange(nb), q_blocks)
    else:
        c_blocks = log_decay.reshape(B, H, nb, Q_BLOCK).transpose(2, 0, 1, 3)
        xs = (jnp.arange(nb), q_blocks, c_blocks)

    def one_block(args):
        i, qi = args[0], args[1]
        s = jnp.einsum('bhqd,bhkd->bhqk', qi, k, preferred_element_type=jnp.float32) * scale
        if log_decay is not None:
            s = s + args[2][..., :, None] - log_decay.astype(jnp.float32)[:, :, None, :]
        q_pos = i * Q_BLOCK + jnp.arange(Q_BLOCK)
        mask = key_pos[None, :] <= q_pos[:, None]
        s = jnp.where(mask, s, jnp.finfo(jnp.float32).min)
        p = jax.nn.softmax(s, axis=-1)
        return jnp.einsum('bhqk,bhkd->bhqd', p.astype(v.dtype), v)

    out = lax.map(one_block, xs)
    return out.transpose(1, 2, 0, 3, 4).reshape(B, H, S, v.shape[-1])


def hybrid_mixer(hn, w_in, q_norm, w_q_up, kv_norm, w_kv_up, b_forget, g_out_mla, g_out_fox, w_o):
    B, S, _ = hn.shape
    proj = hn @ w_in
    fw = FOX_HEADS * FOX_HEAD_DIM
    cuts = list(np.cumsum([MLA_Q_RANK, MLA_KV_RANK, MLA_ROPE, fw, fw, fw]))
    q_lat, kv_lat, k_pe, fq, fk, fv, f_logit = jnp.split(proj, cuts, axis=-1)
    pos = jnp.arange(S)

    q = (rmsnorm(q_lat, q_norm) @ w_q_up).reshape(B, S, MLA_HEADS, MLA_NOPE + MLA_ROPE).transpose(0, 2, 1, 3)
    q_mla = jnp.concatenate([q[..., :MLA_NOPE], rope(q[..., MLA_NOPE:], pos)], axis=-1)
    kv = (rmsnorm(kv_lat, kv_norm) @ w_kv_up).reshape(B, S, MLA_HEADS, MLA_NOPE + MLA_V).transpose(0, 2, 1, 3)
    k_rope = rope(k_pe[:, None, :, :], pos)
    k_mla = jnp.concatenate([kv[..., :MLA_NOPE], jnp.broadcast_to(k_rope, (B, MLA_HEADS, S, MLA_ROPE))], axis=-1)
    v_mla = kv[..., MLA_NOPE:]
    o_mla = causal_block_attention(q_mla, k_mla, v_mla, (MLA_NOPE + MLA_ROPE) ** -0.5)

    heads = lambda t: t.reshape(B, S, FOX_HEADS, FOX_HEAD_DIM).transpose(0, 2, 1, 3)
    log_f = jax.nn.log_sigmoid(f_logit.astype(jnp.float32) + b_forget.astype(jnp.float32))
    c = jnp.cumsum(log_f, axis=1).transpose(0, 2, 1)
    o_fox = causal_block_attention(heads(fq), heads(fk), heads(fv), FOX_HEAD_DIM ** -0.5, c)

    o_mla = o_mla.transpose(0, 2, 1, 3).reshape(B, S, MLA_HEADS * MLA_V)
    o_fox = o_fox.transpose(0, 2, 1, 3).reshape(B, S, fw)
    o = jnp.concatenate([rmsnorm(o_mla, g_out_mla), rmsnorm(o_fox, g_out_fox)], axis=-1)
    return o @ w_o


def memory_cross_attention(hn, mem, mem_norm, w_xq, w_mem_kv, w_xo):
    B, S, _ = hn.shape
    M = mem.shape[1]
    mn = rmsnorm(mem, mem_norm)
    q = (hn @ w_xq).reshape(B, S, X_HEADS, X_HEAD_DIM)
    kv = (mn @ w_mem_kv).reshape(B, M, 2, X_HEADS, X_HEAD_DIM)
    k, v = kv[:, :, 0], kv[:, :, 1]
    s = jnp.einsum('bshd,bmhd->bhsm', q, k, preferred_element_type=jnp.float32) * X_HEAD_DIM ** -0.5
    p = jax.nn.softmax(s, axis=-1)
    o = jnp.einsum('bhsm,bmhd->bshd', p.astype(v.dtype), v).reshape(B, S, D_MODEL)
    return o @ w_xo


def moe_ffn(hn, w_router, b_router, w_gate_up, b_gate_up, w_down, b_down):
    B, S, D = hn.shape
    T = B * S
    xt = hn.reshape(T, D)
    logits = (xt @ w_router).astype(jnp.float32) + b_router.astype(jnp.float32)
    top_vals, top_idx = lax.top_k(logits, TOP_K)
    gates = jax.nn.softmax(top_vals, axis=-1)
    flat_e = top_idx.reshape(-1)
    order = jnp.argsort(flat_e)
    e_sorted = flat_e[order]
    tok = order // TOP_K
    group_sizes = jnp.bincount(flat_e, length=N_EXPERTS).astype(jnp.int32)
    xs = xt[tok]
    h = lax.ragged_dot(xs, w_gate_up, group_sizes) + b_gate_up[e_sorted]
    glu = jnp.minimum(h[:, ::2], SWIGLU_LIMIT)
    lin = jnp.clip(h[:, 1::2], -SWIGLU_LIMIT, SWIGLU_LIMIT)
    act = glu * jax.nn.sigmoid(SWIGLU_ALPHA * glu) * (lin + 1.0)
    y = lax.ragged_dot(act, w_down, group_sizes) + b_down[e_sorted]
    y = y * gates.reshape(-1)[order].astype(y.dtype)[:, None]
    out = jax.ops.segment_sum(y, tok, num_segments=T)
    return out.reshape(B, S, D)


def setup_inputs(seed: int = 0) -> dict:
    key = jax.random.key(seed)
    ks = jax.random.split(key, 32)
    f32 = jnp.float32
    L = DEPTH
    nrm = lambda k, shape, fan_in: jax.random.normal(k, shape, f32) * (fan_in ** -0.5)
    gain = lambda k, shape: 1.0 + 0.02 * jax.random.normal(k, shape, f32)
    small = lambda k, shape: 0.01 * jax.random.normal(k, shape, f32)
    return {
        "x": jax.random.normal(ks[0], (BATCH, SEQ, D_MODEL), f32),
        "mem": jax.random.normal(ks[1], (BATCH, MEM_LEN, D_MODEL), f32),
        "ln_mix": gain(ks[2], (L, D_MODEL)),
        "w_in": nrm(ks[3], (L, D_MODEL, IN_COLS), D_MODEL),
        "q_norm": gain(ks[4], (L, MLA_Q_RANK)),
        "w_q_up": nrm(ks[5], (L, MLA_Q_RANK, MLA_HEADS * (MLA_NOPE + MLA_ROPE)), MLA_Q_RANK),
        "kv_norm": gain(ks[6], (L, MLA_KV_RANK)),
        "w_kv_up": nrm(ks[7], (L, MLA_KV_RANK, MLA_HEADS * (MLA_NOPE + MLA_V)), MLA_KV_RANK),
        "b_forget": jax.random.uniform(ks[8], (L, FOX_HEADS), f32, minval=1.0, maxval=4.0),
        "g_out_mla": gain(ks[9], (L, MLA_HEADS * MLA_V)),
        "g_out_fox": gain(ks[10], (L, FOX_HEADS * FOX_HEAD_DIM)),
        "w_o": nrm(ks[11], (L, MIX_WIDTH, D_MODEL), MIX_WIDTH),
        "ln_cross": gain(ks[12], (L, D_MODEL)),
        "mem_norm": gain(ks[13], (L, D_MODEL)),
        "w_xq": nrm(ks[14], (L, D_MODEL, D_MODEL), D_MODEL),
        "w_mem_kv": nrm(ks[15], (L, D_MODEL, 2 * D_MODEL), D_MODEL),
        "w_xo": nrm(ks[16], (L, D_MODEL, D_MODEL), D_MODEL),
        "ln_ffn": gain(ks[17], (L, D_MODEL)),
        "w_router": nrm(ks[18], (L, D_MODEL, N_EXPERTS), D_MODEL),
        "b_router": small(ks[19], (L, N_EXPERTS)),
        "w_gate_up": nrm(ks[20], (L, N_EXPERTS, D_MODEL, 2 * D_EXPERT), D_MODEL),
        "b_gate_up": small(ks[21], (L, N_EXPERTS, 2 * D_EXPERT)),
        "w_down": nrm(ks[22], (L, N_EXPERTS, D_EXPERT, D_MODEL), D_EXPERT),
        "b_down": small(ks[23], (L, N_EXPERTS, D_MODEL)),
        "ln_final": gain(ks[24], (D_MODEL,)),
    }


def reference(x, mem, ln_mix, w_in, q_norm, w_q_up, kv_norm, w_kv_up, b_forget, g_out_mla, g_out_fox, w_o,
              ln_cross, mem_norm, w_xq, w_mem_kv, w_xo, ln_ffn, w_router, b_router, w_gate_up, b_gate_up,
              w_down, b_down, ln_final):
    h = x
    for l in range(DEPTH):
        h = h + hybrid_mixer(rmsnorm(h, ln_mix[l]), w_in[l], q_norm[l], w_q_up[l], kv_norm[l], w_kv_up[l],
                             b_forget[l], g_out_mla[l], g_out_fox[l], w_o[l])
        h = h + memory_cross_attention(rmsnorm(h, ln_cross[l]), mem, mem_norm[l], w_xq[l], w_mem_kv[l], w_xo[l])
        h = h + moe_ffn(rmsnorm(h, ln_ffn[l]), w_router[l], b_router[l], w_gate_up[l], b_gate_up[l],
                        w_down[l], b_down[l])
    return rmsnorm(h, ln_final)
```

```python
import functools

import jax
import jax.numpy as jnp
import numpy as np
from jax import lax
from jax.experimental import pallas as pl
from jax.experimental.pallas import tpu as pltpu

F32 = jnp.float32
BF16 = jnp.bfloat16

D_MODEL = 1024
MEM_LEN = 256
MLA_HEADS = 8
MLA_NOPE = 64
MLA_ROPE = 32
MLA_V = 64
MLA_Q_RANK = 256
MLA_KV_RANK = 128
FOX_HEADS = 8
FOX_HEAD_DIM = 64
X_HEADS = 4
X_HEAD_DIM = D_MODEL // X_HEADS
N_EXPERTS = 32
TOP_K = 4
D_EXPERT = D_MODEL
SWIGLU_LIMIT = 7.0
SWIGLU_ALPHA = 1.702
ROPE_THETA = 10000.0
EPS = 1e-6

N_HEADS = MLA_HEADS + FOX_HEADS
HEAD_LANES = 128
LANES = 128
NEG_BIG = -1e30

_C_QLAT = 0
_C_KVLAT = _C_QLAT + MLA_Q_RANK
_C_KPE = _C_KVLAT + MLA_KV_RANK
_C_KPE_SWAP = _C_KPE + LANES
_C_FQ = _C_KPE_SWAP + LANES
_C_FK = _C_FQ + FOX_HEADS * FOX_HEAD_DIM
_C_FV = _C_FK + FOX_HEADS * FOX_HEAD_DIM
_C_FLOGIT = _C_FV + FOX_HEADS * FOX_HEAD_DIM
IN_COLS_PACKED = _C_FLOGIT + LANES

VMEM_LIMIT = 56 * 1024 * 1024

TM_PRE = 256
TQ = 512
TM_POST = 256
TS = 512
TM_EXP = 512
TC = 256


def _rms(x, eps=EPS):
    return x * lax.rsqrt(jnp.mean(x * x, axis=-1, keepdims=True) + eps)


def _dot(a, b):
    return jnp.dot(a, b, preferred_element_type=F32)


def _dot_nt(a, b):
    return lax.dot_general(a, b, (((1,), (1,)), ((), ())), preferred_element_type=F32)


def _split3(x):
    hi = x.astype(BF16)
    r = x - hi.astype(F32)
    mid = r.astype(BF16)
    lo = (r - mid.astype(F32)).astype(BF16)
    return hi, mid, lo


def _premix_kernel(x_ref, g_ref, win_ref, qn_ref, wqm_ref, wqs_ref, kvn_ref, wkk_ref, wkv_ref,
                   cq_ref, sq_ref, ck_ref, sk_ref, bf_ref, ltri_ref, eq_ref, ek_ref, oq_ref, ok_ref,
                   q_out, k_out, v_out, carry_ref):
    i = pl.program_id(1)

    @pl.when(i == 0)
    def _():
        carry_ref[...] = jnp.zeros_like(carry_ref)

    x = x_ref[0]
    hn = (_rms(x) * g_ref[...]).astype(BF16)
    proj = _dot(hn, win_ref[...])

    qn = (_rms(proj[:, _C_QLAT:_C_QLAT + MLA_Q_RANK]) * qn_ref[...]).astype(BF16)
    qm = _dot(qn, wqm_ref[...])
    qs = _dot(qn, wqs_ref[...])
    cq = cq_ref[...]
    sq = sq_ref[...]
    for h in range(MLA_HEADS):
        sl = slice(h * HEAD_LANES, (h + 1) * HEAD_LANES)
        q_out[0, h] = (qm[:, sl] * cq + qs[:, sl] * sq).astype(BF16)

    kvn = (_rms(proj[:, _C_KVLAT:_C_KVLAT + MLA_KV_RANK]) * kvn_ref[...]).astype(BF16)
    kk = _dot(kvn, wkk_ref[...])
    vv = _dot(kvn, wkv_ref[...])
    kr = (proj[:, _C_KPE:_C_KPE + LANES] * ck_ref[...]
          + proj[:, _C_KPE_SWAP:_C_KPE_SWAP + LANES] * sk_ref[...])
    for h in range(MLA_HEADS):
        sl = slice(h * HEAD_LANES, (h + 1) * HEAD_LANES)
        k_out[0, h] = (kk[:, sl] + kr).astype(BF16)
        v_out[0, h] = vv[:, sl].astype(BF16)

    z = proj[:, _C_FLOGIT:_C_FLOGIT + LANES] + bf_ref[...]
    logf = jnp.minimum(z, 0.0) - jnp.log(1.0 + jnp.exp(-jnp.abs(z)))
    hi, mid, lo = _split3(logf)
    ltri = ltri_ref[...]
    cs = _dot(ltri, hi) + _dot(ltri, mid) + _dot(ltri, lo)
    c = cs + carry_ref[0:1, :]
    carry_ref[...] = jnp.broadcast_to(c[TM_PRE - 1:TM_PRE, :], carry_ref.shape)
    chi, cmid, clo = _split3(c)
    cparts = jnp.concatenate([chi, cmid, clo], axis=1)
    aug_q = _dot(cparts, eq_ref[...]) + oq_ref[...]
    aug_k = _dot(cparts, ek_ref[...]) + ok_ref[...]
    lane = lax.broadcasted_iota(jnp.int32, (TM_PRE, LANES), 1)
    low = lane < FOX_HEAD_DIM
    scale = FOX_HEAD_DIM ** -0.5
    for p in range(FOX_HEADS // 2):
        fq = proj[:, _C_FQ + p * LANES:_C_FQ + (p + 1) * LANES] * scale
        fk = proj[:, _C_FK + p * LANES:_C_FK + (p + 1) * LANES]
        fv = proj[:, _C_FV + p * LANES:_C_FV + (p + 1) * LANES]
        for par in range(2):
            h = 2 * p + par
            keep = low if par == 0 else jnp.logical_not(low)
            sl = slice(h * HEAD_LANES, (h + 1) * HEAD_LANES)
            q_out[0, MLA_HEADS + h] = (jnp.where(keep, fq, 0.0) + aug_q[:, sl]).astype(BF16)
            k_out[0, MLA_HEADS + h] = (jnp.where(keep, fk, 0.0) + aug_k[:, sl]).astype(BF16)
            v_out[0, MLA_HEADS + h] = jnp.where(keep, fv, 0.0).astype(BF16)


def _premix(x, g, win, qn, wqm, wqs, kvn, wkk, wkv, cq, sq, ck, sk, bf, ltri, eq, ek, oq, ok):
    B, S, _ = x.shape
    tm = TM_PRE
    const = lambda shape: pl.BlockSpec(shape, lambda b, i: (0,) * len(shape))
    rows = lambda w: pl.BlockSpec((tm, w), lambda b, i: (i, 0))
    head_out = pl.BlockSpec((1, N_HEADS, tm, HEAD_LANES), lambda b, i: (b, 0, i, 0))
    out_sds = jax.ShapeDtypeStruct((B, N_HEADS, S, HEAD_LANES), BF16)
    return pl.pallas_call(
        _premix_kernel,
        grid=(B, S // tm),
        in_specs=[
            pl.BlockSpec((1, tm, D_MODEL), lambda b, i: (b, i, 0)),
            const(g.shape), const(win.shape), const(qn.shape), const(wqm.shape), const(wqs.shape),
            const(kvn.shape), const(wkk.shape), const(wkv.shape),
            rows(LANES), rows(LANES), rows(LANES), rows(LANES),
            const(bf.shape), const(ltri.shape), const(eq.shape), const(ek.shape),
            const(oq.shape), const(ok.shape),
        ],
        out_specs=[head_out, head_out, head_out],
        out_shape=[out_sds, out_sds, out_sds],
        scratch_shapes=[pltpu.VMEM((8, LANES), F32)],
        compiler_params=pltpu.CompilerParams(
            dimension_semantics=("arbitrary", "arbitrary"), vmem_limit_bytes=VMEM_LIMIT),
        name="premix",
    )(x, g, win, qn, wqm, wqs, kvn, wkk, wkv, cq, sq, ck, sk, bf, ltri, eq, ek, oq, ok)


def _attn_kernel(q_ref, k_ref, v_ref, o_ref):
    i = pl.program_id(2)
    row = lax.broadcasted_iota(jnp.int32, (TQ, TQ), 0)
    col = lax.broadcasted_iota(jnp.int32, (TQ, TQ), 1)
    causal = col <= row
    out = jnp.zeros((TQ, HEAD_LANES), F32)
    for hh in range(2):
        q = q_ref[0, hh]

        def step(j, carry, masked):
            m, l, acc = carry
            start = pl.multiple_of(j * TQ, TQ)
            k = k_ref[0, hh, pl.ds(start, TQ), :]
            v = v_ref[0, hh, pl.ds(start, TQ), :]
            s = _dot_nt(q, k)
            if masked:
                s = jnp.where(causal, s, NEG_BIG)
            m_new = jnp.maximum(m, jnp.max(s, axis=-1, keepdims=True))
            p = jnp.exp(s - m_new)
            alpha = jnp.exp(m - m_new)
            l = alpha * l + jnp.sum(p, axis=-1, keepdims=True)
            acc = alpha * acc + _dot(p.astype(BF16), v)
            return m_new, l, acc

        init = (jnp.full((TQ, 1), NEG_BIG, F32), jnp.zeros((TQ, 1), F32),
                jnp.zeros((TQ, HEAD_LANES), F32))
        carry = lax.fori_loop(0, i, lambda j, c: step(j, c, False), init)
        m, l, acc = step(i, carry, True)
        out = out + acc / l
    o_ref[0] = out.astype(o_ref.dtype)


def _attention(q, k, v):
    B, H, S, _ = q.shape
    return pl.pallas_call(
        _attn_kernel,
        grid=(B, H // 2, S // TQ),
        in_specs=[
            pl.BlockSpec((1, 2, TQ, HEAD_LANES), lambda b, p, i: (b, p, i, 0)),
            pl.BlockSpec((1, 2, S, HEAD_LANES), lambda b, p, i: (b, p, 0, 0)),
            pl.BlockSpec((1, 2, S, HEAD_LANES), lambda b, p, i: (b, p, 0, 0)),
        ],
        out_specs=pl.BlockSpec((1, TQ, HEAD_LANES), lambda b, p, i: (b, i, p)),
        out_shape=jax.ShapeDtypeStruct((B, S, (H // 2) * HEAD_LANES), BF16),
        compiler_params=pltpu.CompilerParams(
            dimension_semantics=("arbitrary", "arbitrary", "arbitrary"), vmem_limit_bytes=VMEM_LIMIT),
        name="attn",
    )(q, k, v)


def _memkv_kernel(mem_ref, g_ref, w_ref, o_ref):
    mn = (_rms(mem_ref[0]) * g_ref[...]).astype(BF16)
    o_ref[0] = _dot(mn, w_ref[...]).astype(o_ref.dtype)


def _memkv(mem, g, w):
    B, M, _ = mem.shape
    return pl.pallas_call(
        _memkv_kernel,
        grid=(B,),
        in_specs=[
            pl.BlockSpec((1, M, D_MODEL), lambda b: (b, 0, 0)),
            pl.BlockSpec(g.shape, lambda b: (0, 0)),
            pl.BlockSpec(w.shape, lambda b: (0, 0)),
        ],
        out_specs=pl.BlockSpec((1, M, 2 * D_MODEL), lambda b: (b, 0, 0)),
        out_shape=jax.ShapeDtypeStruct((B, M, 2 * D_MODEL), BF16),
        compiler_params=pltpu.CompilerParams(
            dimension_semantics=("arbitrary",), vmem_limit_bytes=VMEM_LIMIT),
        name="memkv",
    )(mem, g, w)


def _postmix_kernel(o_ref, x_ref, gmla_ref, gfox_ref, wo_ref, lnx_ref, wxq_ref, mkv_ref, wxo_ref,
                    lnf_ref, wr_hi_ref, wr_lo_ref, br_ref, u_ref,
                    h2_out, hn_out, idx_out, rank_out, gate_out, cnt_out, cnt_ref):
    first = jnp.logical_and(pl.program_id(0) == 0, pl.program_id(1) == 0)

    @pl.when(first)
    def _():
        cnt_ref[...] = jnp.zeros_like(cnt_ref)

    tm = TM_POST
    half = D_MODEL // 2
    o = o_ref[0].astype(F32)
    on = jnp.concatenate([_rms(o[:, :half]) * gmla_ref[...], _rms(o[:, half:]) * gfox_ref[...]], axis=1)
    h1 = x_ref[0] + _dot(on.astype(BF16), wo_ref[...])

    hn2 = (_rms(h1) * lnx_ref[...]).astype(BF16)
    qx = _dot(hn2, wxq_ref[...]).astype(BF16)
    heads = []
    for h in range(X_HEADS):
        sl = slice(h * X_HEAD_DIM, (h + 1) * X_HEAD_DIM)
        kh = mkv_ref[0, :, sl]
        vh = mkv_ref[0, :, D_MODEL + h * X_HEAD_DIM:D_MODEL + (h + 1) * X_HEAD_DIM]
        s = _dot_nt(qx[:, sl], kh) * (X_HEAD_DIM ** -0.5)
        p = jnp.exp(s - jnp.max(s, axis=-1, keepdims=True))
        p = p / jnp.sum(p, axis=-1, keepdims=True)
        heads.append(_dot(p.astype(BF16), vh))
    ox = jnp.concatenate(heads, axis=1).astype(BF16)
    h2 = h1 + _dot(ox, wxo_ref[...])
    h2_out[0] = h2

    hn3 = _rms(h2) * lnf_ref[...]
    hn_out[0] = hn3
    a_hi = hn3.astype(BF16)
    a_lo = (hn3 - a_hi.astype(F32)).astype(BF16)
    logits = (_dot_nt(wr_hi_ref[...], a_hi) + _dot_nt(wr_hi_ref[...], a_lo)
              + _dot_nt(wr_lo_ref[...], a_hi)) + br_ref[:, 0:1]

    eid = lax.broadcasted_iota(jnp.int32, (N_EXPERTS, tm), 0).astype(F32)
    vals = logits
    top_v, top_i, hots = [], [], []
    for _ in range(TOP_K):
        mx = jnp.max(vals, axis=0, keepdims=True)
        sel = jnp.min(jnp.where(vals == mx, eid, float(N_EXPERTS)), axis=0, keepdims=True)
        hot = eid == sel
        vals = jnp.where(hot, -jnp.inf, vals)
        top_v.append(mx)
        top_i.append(sel.astype(jnp.int32))
        hots.append(hot)
    ex = [jnp.exp(v - top_v[0]) for v in top_v]
    den = ex[0] + ex[1] + ex[2] + ex[3]
    gate_out[...] = jnp.concatenate([e / den for e in ex], axis=0)
    idx_out[...] = jnp.concatenate(top_i, axis=0)

    hot_all = jnp.where(hots[0] | hots[1] | hots[2] | hots[3], 1.0, 0.0)
    before = _dot(hot_all.astype(BF16), u_ref[...]) + cnt_ref[:, 0:1]
    ranks = [jnp.sum(jnp.where(hot, before, 0.0), axis=0, keepdims=True) for hot in hots]
    rank_out[...] = jnp.concatenate(ranks, axis=0).astype(jnp.int32)
    cnt_new = cnt_ref[...] + jnp.sum(hot_all, axis=1, keepdims=True)
    cnt_ref[...] = cnt_new
    cnt_out[...] = cnt_new


def _postmix(o, x, gmla, gfox, wo, lnx, wxq, mkv, wxo, lnf, wr_hi, wr_lo, br, u):
    B, S, _ = x.shape
    tm = TM_POST
    nt = S // tm
    T = B * S
    const = lambda a: pl.BlockSpec(a.shape, lambda b, i: (0,) * a.ndim)
    tok = pl.BlockSpec((1, tm, D_MODEL), lambda b, i: (b, i, 0))
    route = pl.BlockSpec((TOP_K, tm), lambda b, i: (0, b * nt + i))
    return pl.pallas_call(
        _postmix_kernel,
        grid=(B, nt),
        in_specs=[
            tok, tok, const(gmla), const(gfox), const(wo), const(lnx), const(wxq),
            pl.BlockSpec((1, MEM_LEN, 2 * D_MODEL), lambda b, i: (b, 0, 0)),
            const(wxo), const(lnf), const(wr_hi), const(wr_lo), const(br), const(u),
        ],
        out_specs=[tok, tok, route, route, route,
                   pl.BlockSpec((N_EXPERTS, LANES), lambda b, i: (0, 0))],
        out_shape=[
            jax.ShapeDtypeStruct((B, S, D_MODEL), F32),
            jax.ShapeDtypeStruct((B, S, D_MODEL), F32),
            jax.ShapeDtypeStruct((TOP_K, T), jnp.int32),
            jax.ShapeDtypeStruct((TOP_K, T), jnp.int32),
            jax.ShapeDtypeStruct((TOP_K, T), F32),
            jax.ShapeDtypeStruct((N_EXPERTS, LANES), F32),
        ],
        scratch_shapes=[pltpu.VMEM((N_EXPERTS, LANES), F32)],
        compiler_params=pltpu.CompilerParams(
            dimension_semantics=("arbitrary", "arbitrary"), vmem_limit_bytes=VMEM_LIMIT),
        name="postmix",
    )(o, x, gmla, gfox, wo, lnx, wxq, mkv, wxo, lnf, wr_hi, wr_lo, br, u)


def _scatter_kernel(hn_ref, dest_hbm, xs_hbm, dest_smem, idx_sem, row_sem):
    i = pl.program_id(0)
    n_idx = TS * TOP_K
    idx_cp = pltpu.make_async_copy(dest_hbm.at[pl.ds(pl.multiple_of(i * n_idx, n_idx), n_idx)],
                                   dest_smem, idx_sem)
    idx_cp.start()
    idx_cp.wait()

    def issue(t, _):
        for k in range(TOP_K):
            d = dest_smem[t * TOP_K + k]
            pltpu.make_async_copy(hn_ref.at[pl.ds(t, 1), :], xs_hbm.at[pl.ds(d, 1), :], row_sem).start()
        return 0

    lax.fori_loop(0, TS, issue, 0)
    for _ in range(TOP_K):
        pltpu.make_async_copy(hn_ref, xs_hbm.at[pl.ds(0, TS), :], row_sem).wait()


def _scatter(hn, dest_flat):
    T = hn.shape[0]
    return pl.pallas_call(
        _scatter_kernel,
        grid=(T // TS,),
        in_specs=[
            pl.BlockSpec((TS, D_MODEL), lambda i: (i, 0)),
            pl.BlockSpec(memory_space=pl.ANY),
        ],
        out_specs=pl.BlockSpec(memory_space=pl.ANY),
        out_shape=jax.ShapeDtypeStruct((T * TOP_K, D_MODEL), F32),
        scratch_shapes=[pltpu.SMEM((TS * TOP_K,), jnp.int32), pltpu.SemaphoreType.DMA,
                        pltpu.SemaphoreType.DMA],
        compiler_params=pltpu.CompilerParams(
            dimension_semantics=("arbitrary",), vmem_limit_bytes=VMEM_LIMIT),
        name="scatter",
    )(hn, dest_flat)


def _expert_kernel(tile_ref, exp_ref, lo_ref, hi_ref, first_ref,
                   xs_ref, wg_ref, wl_ref, bg_ref, bl_ref, wd_ref, bd_ref, y_ref):
    w = pl.program_id(0)
    lo = lo_ref[w]
    hi = hi_ref[w]

    @pl.when(hi > lo)
    def _():
        x = xs_ref[...].astype(BF16)
        g = _dot(x, wg_ref[0]) + bg_ref[0]
        l = _dot(x, wl_ref[0]) + bl_ref[0]
        glu = jnp.minimum(g, SWIGLU_LIMIT)
        lin = jnp.clip(l, -SWIGLU_LIMIT, SWIGLU_LIMIT)
        act = glu * (1.0 / (1.0 + jnp.exp(-SWIGLU_ALPHA * glu))) * (lin + 1.0)
        y = _dot(act.astype(BF16), wd_ref[0]) + bd_ref[0]
        r = tile_ref[w] * TM_EXP + lax.broadcasted_iota(jnp.int32, (TM_EXP, 1), 0)
        mine = jnp.logical_and(r >= lo, r < hi)
        is_first = first_ref[w] == 1

        @pl.when(is_first)
        def _():
            y_ref[...] = jnp.where(mine, y, 0.0)

        @pl.when(jnp.logical_not(is_first))
        def _():
            y_ref[...] = jnp.where(mine, y, y_ref[...])


def _experts(meta, xs, wg, wl, bg, bl, wd, bd):
    R = xs.shape[0]
    n_items = meta[0].shape[0]
    by_tile = lambda w, tile, exp, lo, hi, first: (tile[w], 0)
    by_exp = lambda w, tile, exp, lo, hi, first: (exp[w], 0, 0)
    grid_spec = pltpu.PrefetchScalarGridSpec(
        num_scalar_prefetch=5,
        grid=(n_items,),
        in_specs=[
            pl.BlockSpec((TM_EXP, D_MODEL), by_tile),
            pl.BlockSpec((1, D_MODEL, D_EXPERT), by_exp),
            pl.BlockSpec((1, D_MODEL, D_EXPERT), by_exp),
            pl.BlockSpec((1, 1, D_EXPERT), by_exp),
            pl.BlockSpec((1, 1, D_EXPERT), by_exp),
            pl.BlockSpec((1, D_EXPERT, D_MODEL), by_exp),
            pl.BlockSpec((1, 1, D_MODEL), by_exp),
        ],
        out_specs=pl.BlockSpec((TM_EXP, D_MODEL), by_tile),
    )
    return pl.pallas_call(
        _expert_kernel,
        grid_spec=grid_spec,
        out_shape=jax.ShapeDtypeStruct((R, D_MODEL), F32),
        compiler_params=pltpu.CompilerParams(
            dimension_semantics=("arbitrary",), vmem_limit_bytes=VMEM_LIMIT),
        name="experts",
    )(*meta, xs, wg, wl, bg, bl, wd, bd)


def _combine_kernel(h2_ref, gate_ref, lnf_ref, dest_hbm, y_hbm, out_ref, dest_smem, buf, idx_sem, row_sem):
    i = pl.program_id(0)
    n_idx = TC * TOP_K
    idx_cp = pltpu.make_async_copy(dest_hbm.at[pl.ds(pl.multiple_of(i * n_idx, n_idx), n_idx)],
                                   dest_smem, idx_sem)
    idx_cp.start()
    idx_cp.wait()

    def issue(t, _):
        for k in range(TOP_K):
            d = dest_smem[t * TOP_K + k]
            pltpu.make_async_copy(y_hbm.at[pl.ds(d, 1), :], buf.at[k, pl.ds(t, 1), :], row_sem).start()
        return 0

    lax.fori_loop(0, TC, issue, 0)
    for k in range(TOP_K):
        pltpu.make_async_copy(y_hbm.at[pl.ds(0, TC), :], buf.at[k], row_sem).wait()

    h = h2_ref[...]
    gates = gate_ref[...]
    for k in range(TOP_K):
        h = h + buf[k] * gates[:, k:k + 1]
    out_ref[...] = _rms(h) * lnf_ref[...]


def _combine(h2, gates_t, lnf, dest_flat, y):
    T = h2.shape[0]
    return pl.pallas_call(
        _combine_kernel,
        grid=(T // TC,),
        in_specs=[
            pl.BlockSpec((TC, D_MODEL), lambda i: (i, 0)),
            pl.BlockSpec((TC, TOP_K), lambda i: (i, 0)),
            pl.BlockSpec(lnf.shape, lambda i: (0, 0)),
            pl.BlockSpec(memory_space=pl.ANY),
            pl.BlockSpec(memory_space=pl.ANY),
        ],
        out_specs=pl.BlockSpec((TC, D_MODEL), lambda i: (i, 0)),
        out_shape=jax.ShapeDtypeStruct((T, D_MODEL), F32),
        scratch_shapes=[pltpu.SMEM((TC * TOP_K,), jnp.int32), pltpu.VMEM((TOP_K, TC, D_MODEL), F32),
                        pltpu.SemaphoreType.DMA, pltpu.SemaphoreType.DMA],
        compiler_params=pltpu.CompilerParams(
            dimension_semantics=("arbitrary",), vmem_limit_bytes=VMEM_LIMIT),
        name="combine",
    )(h2, gates_t, lnf, dest_flat, y)


def _rope_tables(S):
    half = MLA_ROPE // 2
    inv = ROPE_THETA ** (-jnp.arange(half, dtype=F32) / half)
    ang = jnp.arange(S, dtype=F32)[:, None] * inv[None, :]
    cos, sin = jnp.cos(ang), jnp.sin(ang)
    z = lambda n: jnp.zeros((S, n), F32)
    pad = HEAD_LANES - MLA_NOPE - MLA_ROPE
    c_tab = jnp.concatenate([jnp.ones((S, MLA_NOPE), F32), cos, cos, z(pad)], axis=1)
    s_tab = jnp.concatenate([z(MLA_NOPE), sin, sin, z(pad)], axis=1)
    return c_tab, s_tab


def _pad_cols(w, left, width=HEAD_LANES):
    return jnp.pad(w, ((0, 0), (left, width - left - w.shape[1])))


def _pack_mixer_weights(w_in, w_q_up, w_kv_up):
    half = MLA_ROPE // 2
    cuts = np.cumsum([MLA_Q_RANK, MLA_KV_RANK, MLA_ROPE, 512, 512, 512])
    w_ql, w_kvl, w_kpe, w_fq, w_fk, w_fv, w_fl = jnp.split(w_in, cuts, axis=1)
    kpe_swap = jnp.concatenate([-w_kpe[:, half:], w_kpe[:, :half]], axis=1)
    win = jnp.concatenate([
        w_ql, w_kvl, _pad_cols(w_kpe, MLA_NOPE), _pad_cols(kpe_swap, MLA_NOPE),
        w_fq, w_fk, w_fv, _pad_cols(w_fl, 0)], axis=1).astype(BF16)

    dq = MLA_NOPE + MLA_ROPE
    wq = w_q_up.reshape(MLA_Q_RANK, MLA_HEADS, dq)
    zq = lambda n: jnp.zeros((MLA_Q_RANK, MLA_HEADS, n), F32)
    wqm = jnp.concatenate([wq, zq(HEAD_LANES - dq)], axis=2)
    wqs = jnp.concatenate([zq(MLA_NOPE), -wq[:, :, MLA_NOPE + half:], wq[:, :, MLA_NOPE:MLA_NOPE + half],
                           zq(HEAD_LANES - dq)], axis=2)
    wkv = w_kv_up.reshape(MLA_KV_RANK, MLA_HEADS, MLA_NOPE + MLA_V)
    zk = lambda n: jnp.zeros((MLA_KV_RANK, MLA_HEADS, n), F32)
    wkk = jnp.concatenate([wkv[:, :, :MLA_NOPE], zk(HEAD_LANES - MLA_NOPE)], axis=2)
    v_even = jnp.concatenate([wkv[:, :, MLA_NOPE:], zk(HEAD_LANES - MLA_V)], axis=2)
    v_odd = jnp.concatenate([zk(HEAD_LANES - MLA_V), wkv[:, :, MLA_NOPE:]], axis=2)
    odd = (jnp.arange(MLA_HEADS) % 2 == 1)[None, :, None]
    wkvv = jnp.where(odd, v_odd, v_even)
    flat = lambda w: w.reshape(w.shape[0], MLA_HEADS * HEAD_LANES).astype(BF16)
    return win, flat(wqm), flat(wqs), flat(wkk), flat(wkvv)


def _fox_placement():
    eq = np.zeros((3 * LANES, FOX_HEADS * HEAD_LANES), np.float32)
    ek = np.zeros((3 * LANES, FOX_HEADS * HEAD_LANES), np.float32)
    oq = np.zeros((1, FOX_HEADS * HEAD_LANES), np.float32)
    ok = np.zeros((1, FOX_HEADS * HEAD_LANES), np.float32)
    for h in range(FOX_HEADS):
        base = h * HEAD_LANES + (FOX_HEAD_DIM if h % 2 == 0 else 0)
        for part in range(3):
            eq[part * LANES + h, base + part] = 1.0
            ok[0, base + part] = 1.0
            oq[0, base + 3 + part] = 1.0
            ek[part * LANES + h, base + 3 + part] = -1.0
    return jnp.asarray(eq, BF16), jnp.asarray(ek, BF16), jnp.asarray(oq), jnp.asarray(ok)


def _work_items(counts, n_rows):
    n_tiles = n_rows // TM_EXP
    n_items = n_tiles + N_EXPERTS - 1
    offs = jnp.concatenate([jnp.zeros((1,), jnp.int32), jnp.cumsum(counts)]).astype(jnp.int32)
    start, end = offs[:-1], offs[1:]
    first_tile = start // TM_EXP
    last_tile = jnp.maximum(end - 1, 0) // TM_EXP
    per_exp = jnp.where(counts > 0, last_tile - first_tile + 1, 0)
    item_end = jnp.cumsum(per_exp)
    item_start = item_end - per_exp
    total = item_end[-1]
    w = jnp.arange(n_items, dtype=jnp.int32)
    e = jnp.minimum(jnp.searchsorted(item_end, w, side="right"), N_EXPERTS - 1).astype(jnp.int32)
    active = w < total
    last_e = jnp.max(jnp.where(counts > 0, jnp.arange(N_EXPERTS, dtype=jnp.int32), 0))
    e = jnp.where(active, e, last_e)
    tile = jnp.where(active, first_tile[e] + (w - item_start[e]), n_tiles - 1).astype(jnp.int32)
    lo = jnp.maximum(start[e], tile * TM_EXP)
    hi = jnp.minimum(end[e], (tile + 1) * TM_EXP)
    lo = jnp.where(active, lo, 0).astype(jnp.int32)
    hi = jnp.where(active, hi, 0).astype(jnp.int32)
    prev_tile = jnp.concatenate([jnp.full((1,), -1, jnp.int32), tile[:-1]])
    first = (tile != prev_tile).astype(jnp.int32)
    return (tile, e, lo, hi, first), offs


def kernel(x, mem, ln_mix, w_in, q_norm, w_q_up, kv_norm, w_kv_up, b_forget, g_out_mla, g_out_fox, w_o,
           ln_cross, mem_norm, w_xq, w_mem_kv, w_xo, ln_ffn, w_router, b_router, w_gate_up, b_gate_up,
           w_down, b_down, ln_final):
    B, S, _ = x.shape
    T = B * S
    row = lambda v: v.reshape(1, -1).astype(F32)

    win, wqm, wqs, wkk, wkvv = _pack_mixer_weights(w_in[0], w_q_up[0], w_kv_up[0])
    c_tab, s_tab = _rope_tables(S)
    q_scale = (MLA_NOPE + MLA_ROPE) ** -0.5
    eq, ek, oq, ok = _fox_placement()
    ltri = jnp.asarray(np.tril(np.ones((TM_PRE, TM_PRE), np.float32)), BF16)
    bf = _pad_cols(row(b_forget[0]), 0)
    q, k, v = _premix(x, row(ln_mix[0]), win, row(q_norm[0]), wqm, wqs, row(kv_norm[0]), wkk, wkvv,
                      c_tab * q_scale, s_tab * q_scale, c_tab, s_tab, bf, ltri, eq, ek, oq, ok)
    o = _attention(q, k, v)

    mkv = _memkv(mem, row(mem_norm[0]), w_mem_kv[0].astype(BF16))
    wr = w_router[0].T
    wr_hi = wr.astype(BF16)
    wr_lo = (wr - wr_hi.astype(F32)).astype(BF16)
    br = jnp.broadcast_to(b_router[0].astype(F32)[:, None], (N_EXPERTS, LANES))
    u = jnp.asarray(np.triu(np.ones((TM_POST, TM_POST), np.float32), 1), BF16)
    h2, hn3, idx, rank, gates, cnt = _postmix(
        o, x, row(g_out_mla[0]), row(g_out_fox[0]), w_o[0].astype(BF16), row(ln_cross[0]),
        w_xq[0].astype(BF16), mkv, w_xo[0].astype(BF16), row(ln_ffn[0]), wr_hi, wr_lo, br, u)

    counts = cnt[:, 0].astype(jnp.int32)
    meta, offs = _work_items(counts, T * TOP_K)
    dest = offs[idx] + rank
    dest_flat = dest.T.reshape(-1)

    xs = _scatter(hn3.reshape(T, D_MODEL), dest_flat)
    wgu = w_gate_up[0].reshape(N_EXPERTS, D_MODEL, D_EXPERT, 2)
    bgu = b_gate_up[0].reshape(N_EXPERTS, 1, D_EXPERT, 2)
    y = _experts(meta, xs, wgu[..., 0].astype(BF16), wgu[..., 1].astype(BF16),
                 bgu[..., 0].astype(F32), bgu[..., 1].astype(F32),
                 w_down[0].astype(BF16), b_down[0].reshape(N_EXPERTS, 1, D_MODEL).astype(F32))

    out = _combine(h2.reshape(T, D_MODEL), gates.T, row(ln_final), dest_flat, y)
    return out.reshape(B, S, D_MODEL)
```

```python
import functools

import jax
import jax.numpy as jnp
import numpy as np
from jax import lax
from jax.experimental import pallas as pl
from jax.experimental.pallas import tpu as pltpu

F32 = jnp.float32
BF16 = jnp.bfloat16

D_MODEL = 1024
MEM_LEN = 256
MLA_HEADS = 8
MLA_NOPE = 64
MLA_ROPE = 32
MLA_V = 64
MLA_Q_RANK = 256
MLA_KV_RANK = 128
FOX_HEADS = 8
FOX_HEAD_DIM = 64
X_HEADS = 4
X_HEAD_DIM = D_MODEL // X_HEADS
N_EXPERTS = 32
TOP_K = 4
D_EXPERT = D_MODEL
SWIGLU_LIMIT = 7.0
SWIGLU_ALPHA = 1.702
ROPE_THETA = 10000.0
EPS = 1e-6

N_HEADS = MLA_HEADS + FOX_HEADS
HEAD_LANES = 128
LANES = 128
NEG_BIG = -1e30
LOG2E = 1.4426950408889634
V_ONES_EVEN = 64
V_ONES_ODD = 0

_C_QLAT = 0
_C_KVLAT = _C_QLAT + MLA_Q_RANK
_C_KPE = _C_KVLAT + MLA_KV_RANK
_C_KPE_SWAP = _C_KPE + LANES
_C_FQ = _C_KPE_SWAP + LANES
_C_FK = _C_FQ + FOX_HEADS * FOX_HEAD_DIM
_C_FV = _C_FK + FOX_HEADS * FOX_HEAD_DIM
_C_FLOGIT = _C_FV + FOX_HEADS * FOX_HEAD_DIM
IN_COLS_PACKED = _C_FLOGIT + LANES

VMEM_LIMIT = 56 * 1024 * 1024

TM_PRE = 256
TQ = 1024
TK = 1024
TD = 512
TM_POST = 256
TS = 512
TM_EXP = 512
TC = 256


def _rms(x, eps=EPS):
    return x * lax.rsqrt(jnp.mean(x * x, axis=-1, keepdims=True) + eps)


def _dot(a, b):
    return jnp.dot(a, b, preferred_element_type=F32)


def _dot_nt(a, b):
    return lax.dot_general(a, b, (((1,), (1,)), ((), ())), preferred_element_type=F32)


def _split3(x):
    hi = x.astype(BF16)
    r = x - hi.astype(F32)
    mid = r.astype(BF16)
    lo = (r - mid.astype(F32)).astype(BF16)
    return hi, mid, lo


def _premix_kernel(x_ref, g_ref, win_ref, qn_ref, wqm_ref, wqs_ref, kvn_ref, wkk_ref, wkv_ref,
                   cq_ref, sq_ref, ck_ref, sk_ref, bf_ref, ltri_ref, eq_ref, ek_ref, oq_ref, ok_ref,
                   q_out, k_out, v_out, carry_ref):
    i = pl.program_id(1)

    @pl.when(i == 0)
    def _():
        carry_ref[...] = jnp.zeros_like(carry_ref)

    x = x_ref[0]
    hn = (_rms(x) * g_ref[...]).astype(BF16)
    proj = _dot(hn, win_ref[...])

    qn = (_rms(proj[:, _C_QLAT:_C_QLAT + MLA_Q_RANK]) * qn_ref[...]).astype(BF16)
    qm = _dot(qn, wqm_ref[...])
    qs = _dot(qn, wqs_ref[...])
    cq = cq_ref[...]
    sq = sq_ref[...]
    for h in range(MLA_HEADS):
        sl = slice(h * HEAD_LANES, (h + 1) * HEAD_LANES)
        q_out[0, h] = (qm[:, sl] * cq + qs[:, sl] * sq).astype(BF16)

    kvn = (_rms(proj[:, _C_KVLAT:_C_KVLAT + MLA_KV_RANK]) * kvn_ref[...]).astype(BF16)
    kk = _dot(kvn, wkk_ref[...])
    vv = _dot(kvn, wkv_ref[...])
    kr = (proj[:, _C_KPE:_C_KPE + LANES] * ck_ref[...]
          + proj[:, _C_KPE_SWAP:_C_KPE_SWAP + LANES] * sk_ref[...])
    lane = lax.broadcasted_iota(jnp.int32, (TM_PRE, LANES), 1)
    low = lane < FOX_HEAD_DIM
    v_ones = (jnp.where(lane == V_ONES_EVEN, 1.0, 0.0), jnp.where(lane == V_ONES_ODD, 1.0, 0.0))
    for h in range(MLA_HEADS):
        sl = slice(h * HEAD_LANES, (h + 1) * HEAD_LANES)
        k_out[0, h] = (kk[:, sl] + kr).astype(BF16)
        v_out[0, h] = (vv[:, sl] + v_ones[h % 2]).astype(BF16)

    z = proj[:, _C_FLOGIT:_C_FLOGIT + LANES] + bf_ref[...]
    logf = jnp.minimum(z, 0.0) - jnp.log(1.0 + jnp.exp(-jnp.abs(z)))
    hi, mid, lo = _split3(logf)
    ltri = ltri_ref[...]
    cs = _dot(ltri, hi) + _dot(ltri, mid) + _dot(ltri, lo)
    c = cs + carry_ref[0:1, :]
    carry_ref[...] = jnp.broadcast_to(c[TM_PRE - 1:TM_PRE, :], carry_ref.shape)
    chi, cmid, clo = _split3(c * LOG2E)
    cparts = jnp.concatenate([chi, cmid, clo], axis=1)
    aug_q = _dot(cparts, eq_ref[...]) + oq_ref[...]
    aug_k = _dot(cparts, ek_ref[...]) + ok_ref[...]
    scale = FOX_HEAD_DIM ** -0.5 * LOG2E
    for p in range(FOX_HEADS // 2):
        fq = proj[:, _C_FQ + p * LANES:_C_FQ + (p + 1) * LANES] * scale
        fk = proj[:, _C_FK + p * LANES:_C_FK + (p + 1) * LANES]
        fv = proj[:, _C_FV + p * LANES:_C_FV + (p + 1) * LANES]
        for par in range(2):
            h = 2 * p + par
            keep = low if par == 0 else jnp.logical_not(low)
            sl = slice(h * HEAD_LANES, (h + 1) * HEAD_LANES)
            q_out[0, MLA_HEADS + h] = (jnp.where(keep, fq, 0.0) + aug_q[:, sl]).astype(BF16)
            k_out[0, MLA_HEADS + h] = (jnp.where(keep, fk, 0.0) + aug_k[:, sl]).astype(BF16)
            v_out[0, MLA_HEADS + h] = jnp.where(keep, fv, v_ones[par]).astype(BF16)


def _premix(x, g, win, qn, wqm, wqs, kvn, wkk, wkv, cq, sq, ck, sk, bf, ltri, eq, ek, oq, ok):
    B, S, _ = x.shape
    tm = TM_PRE
    const = lambda shape: pl.BlockSpec(shape, lambda b, i: (0,) * len(shape))
    rows = lambda w: pl.BlockSpec((tm, w), lambda b, i: (i, 0))
    head_out = pl.BlockSpec((1, N_HEADS, tm, HEAD_LANES), lambda b, i: (b, 0, i, 0))
    out_sds = jax.ShapeDtypeStruct((B, N_HEADS, S, HEAD_LANES), BF16)
    return pl.pallas_call(
        _premix_kernel,
        grid=(B, S // tm),
        in_specs=[
            pl.BlockSpec((1, tm, D_MODEL), lambda b, i: (b, i, 0)),
            const(g.shape), const(win.shape), const(qn.shape), const(wqm.shape), const(wqs.shape),
            const(kvn.shape), const(wkk.shape), const(wkv.shape),
            rows(LANES), rows(LANES), rows(LANES), rows(LANES),
            const(bf.shape), const(ltri.shape), const(eq.shape), const(ek.shape),
            const(oq.shape), const(ok.shape),
        ],
        out_specs=[head_out, head_out, head_out],
        out_shape=[out_sds, out_sds, out_sds],
        scratch_shapes=[pltpu.VMEM((8, LANES), F32)],
        compiler_params=pltpu.CompilerParams(
            dimension_semantics=("arbitrary", "arbitrary"), vmem_limit_bytes=VMEM_LIMIT),
        name="premix",
    )(x, g, win, qn, wqm, wqs, kvn, wkk, wkv, cq, sq, ck, sk, bf, ltri, eq, ek, oq, ok)


def _attn_kernel(q_ref, k_ref, v_ref, o_ref, m_sc, acc_sc):
    i = pl.program_id(2)
    r_loc = lax.broadcasted_iota(jnp.int32, (TD, TD), 0)
    c_loc = lax.broadcasted_iota(jnp.int32, (TD, TD), 1)
    m_sc[...] = jnp.full(m_sc.shape, NEG_BIG, F32)
    acc_sc[...] = jnp.zeros(acc_sc.shape, F32)

    def block(hh, r0, nr, key_start, nk, mask):
        rows = slice(r0, r0 + nr)
        k = k_ref[0, hh, pl.ds(key_start, nk), :]
        v = v_ref[0, hh, pl.ds(key_start, nk), :]
        s = _dot_nt(q_ref[0, hh, rows, :], k)
        if mask is not None:
            s = jnp.where(mask, s, NEG_BIG)
        m_old = m_sc[hh, rows, :]
        m_new = jnp.maximum(m_old, jnp.max(s, axis=-1, keepdims=True))
        p = jnp.exp2(s - jnp.concatenate([m_new] * (nk // LANES), axis=1))
        alpha = jnp.exp2(m_old - m_new)
        acc_sc[hh, rows, :] = alpha * acc_sc[hh, rows, :] + _dot(p.astype(BF16), v)
        m_sc[hh, rows, :] = m_new

    def full_tile(j, carry):
        start = pl.multiple_of(j * TK, TK)
        for hh in range(2):
            block(hh, 0, TQ, start, TK, None)
        return carry

    lax.fori_loop(0, i * (TQ // TK), full_tile, 0)

    for kt in range(TQ // TD):
        start = pl.multiple_of(i * TQ + kt * TD, TD)
        for hh in range(2):
            for qt in range(kt, TQ // TD):
                block(hh, qt * TD, TD, start, TD, (c_loc <= r_loc) if qt == kt else None)

    lane = lax.broadcasted_iota(jnp.int32, (TQ, HEAD_LANES), 1)
    a0 = acc_sc[0]
    a1 = acc_sc[1]
    out = jnp.where(lane < V_ONES_EVEN, a0 / a0[:, V_ONES_EVEN:V_ONES_EVEN + 1],
                    a1 / a1[:, V_ONES_ODD:V_ONES_ODD + 1])
    o_ref[0] = out.astype(o_ref.dtype)


def _attention(q, k, v):
    B, H, S, _ = q.shape
    return pl.pallas_call(
        _attn_kernel,
        grid=(B, H // 2, S // TQ),
        in_specs=[
            pl.BlockSpec((1, 2, TQ, HEAD_LANES), lambda b, p, i: (b, p, i, 0)),
            pl.BlockSpec((1, 2, S, HEAD_LANES), lambda b, p, i: (b, p, 0, 0)),
            pl.BlockSpec((1, 2, S, HEAD_LANES), lambda b, p, i: (b, p, 0, 0)),
        ],
        out_specs=pl.BlockSpec((1, TQ, HEAD_LANES), lambda b, p, i: (b, i, p)),
        out_shape=jax.ShapeDtypeStruct((B, S, (H // 2) * HEAD_LANES), BF16),
        scratch_shapes=[pltpu.VMEM((2, TQ, LANES), F32), pltpu.VMEM((2, TQ, HEAD_LANES), F32)],
        compiler_params=pltpu.CompilerParams(
            dimension_semantics=("arbitrary", "arbitrary", "arbitrary"), vmem_limit_bytes=VMEM_LIMIT),
        name="attn",
    )(q, k, v)


def _memkv_kernel(mem_ref, g_ref, w_ref, o_ref):
    mn = (_rms(mem_ref[0]) * g_ref[...]).astype(BF16)
    o_ref[0] = _dot(mn, w_ref[...]).astype(o_ref.dtype)


def _memkv(mem, g, w):
    B, M, _ = mem.shape
    return pl.pallas_call(
        _memkv_kernel,
        grid=(B,),
        in_specs=[
            pl.BlockSpec((1, M, D_MODEL), lambda b: (b, 0, 0)),
            pl.BlockSpec(g.shape, lambda b: (0, 0)),
            pl.BlockSpec(w.shape, lambda b: (0, 0)),
        ],
        out_specs=pl.BlockSpec((1, M, 2 * D_MODEL), lambda b: (b, 0, 0)),
        out_shape=jax.ShapeDtypeStruct((B, M, 2 * D_MODEL), BF16),
        compiler_params=pltpu.CompilerParams(
            dimension_semantics=("arbitrary",), vmem_limit_bytes=VMEM_LIMIT),
        name="memkv",
    )(mem, g, w)


def _postmix_kernel(o_ref, x_ref, gmla_ref, gfox_ref, wo_ref, lnx_ref, wxq_ref, mkv_ref, wxo_ref,
                    lnf_ref, wr_hi_ref, wr_lo_ref, br_ref, u_ref,
                    h2_out, hn_out, idx_out, rank_out, gate_out, cnt_out, cnt_ref):
    first = jnp.logical_and(pl.program_id(0) == 0, pl.program_id(1) == 0)

    @pl.when(first)
    def _():
        cnt_ref[...] = jnp.zeros_like(cnt_ref)

    tm = TM_POST
    half = D_MODEL // 2
    o = o_ref[0].astype(F32)
    on = jnp.concatenate([_rms(o[:, :half]) * gmla_ref[...], _rms(o[:, half:]) * gfox_ref[...]], axis=1)
    h1 = x_ref[0] + _dot(on.astype(BF16), wo_ref[...])

    hn2 = (_rms(h1) * lnx_ref[...]).astype(BF16)
    qx = _dot(hn2, wxq_ref[...]).astype(BF16)
    heads = []
    for h in range(X_HEADS):
        sl = slice(h * X_HEAD_DIM, (h + 1) * X_HEAD_DIM)
        kh = mkv_ref[0, :, sl]
        vh = mkv_ref[0, :, D_MODEL + h * X_HEAD_DIM:D_MODEL + (h + 1) * X_HEAD_DIM]
        s = _dot_nt(qx[:, sl], kh) * (X_HEAD_DIM ** -0.5)
        p = jnp.exp(s - jnp.max(s, axis=-1, keepdims=True))
        p = p / jnp.sum(p, axis=-1, keepdims=True)
        heads.append(_dot(p.astype(BF16), vh))
    ox = jnp.concatenate(heads, axis=1).astype(BF16)
    h2 = h1 + _dot(ox, wxo_ref[...])
    h2_out[0] = h2

    hn3 = _rms(h2) * lnf_ref[...]
    hn_out[0] = hn3
    a_hi = hn3.astype(BF16)
    a_lo = (hn3 - a_hi.astype(F32)).astype(BF16)
    logits = (_dot_nt(wr_hi_ref[...], a_hi) + _dot_nt(wr_hi_ref[...], a_lo)
              + _dot_nt(wr_lo_ref[...], a_hi)) + br_ref[:, 0:1]

    eid = lax.broadcasted_iota(jnp.int32, (N_EXPERTS, tm), 0).astype(F32)
    vals = logits
    top_v, top_i, hots = [], [], []
    for _ in range(TOP_K):
        mx = jnp.max(vals, axis=0, keepdims=True)
        sel = jnp.min(jnp.where(vals == mx, eid, float(N_EXPERTS)), axis=0, keepdims=True)
        hot = eid == sel
        vals = jnp.where(hot, -jnp.inf, vals)
        top_v.append(mx)
        top_i.append(sel.astype(jnp.int32))
        hots.append(hot)
    ex = [jnp.exp(v - top_v[0]) for v in top_v]
    den = ex[0] + ex[1] + ex[2] + ex[3]
    gate_out[...] = jnp.concatenate([e / den for e in ex], axis=0)
    idx_out[...] = jnp.concatenate(top_i, axis=0)

    hot_all = jnp.where(hots[0] | hots[1] | hots[2] | hots[3], 1.0, 0.0)
    before = _dot(hot_all.astype(BF16), u_ref[...]) + cnt_ref[:, 0:1]
    ranks = [jnp.sum(jnp.where(hot, before, 0.0), axis=0, keepdims=True) for hot in hots]
    rank_out[...] = jnp.concatenate(ranks, axis=0).astype(jnp.int32)
    cnt_new = cnt_ref[...] + jnp.sum(hot_all, axis=1, keepdims=True)
    cnt_ref[...] = cnt_new
    cnt_out[...] = cnt_new


def _postmix(o, x, gmla, gfox, wo, lnx, wxq, mkv, wxo, lnf, wr_hi, wr_lo, br, u):
    B, S, _ = x.shape
    tm = TM_POST
    nt = S // tm
    T = B * S
    const = lambda a: pl.BlockSpec(a.shape, lambda b, i: (0,) * a.ndim)
    tok = pl.BlockSpec((1, tm, D_MODEL), lambda b, i: (b, i, 0))
    route = pl.BlockSpec((TOP_K, tm), lambda b, i: (0, b * nt + i))
    return pl.pallas_call(
        _postmix_kernel,
        grid=(B, nt),
        in_specs=[
            tok, tok, const(gmla), const(gfox), const(wo), const(lnx), const(wxq),
            pl.BlockSpec((1, MEM_LEN, 2 * D_MODEL), lambda b, i: (b, 0, 0)),
            const(wxo), const(lnf), const(wr_hi), const(wr_lo), const(br), const(u),
        ],
        out_specs=[tok, tok, route, route, route,
                   pl.BlockSpec((N_EXPERTS, LANES), lambda b, i: (0, 0))],
        out_shape=[
            jax.ShapeDtypeStruct((B, S, D_MODEL), F32),
            jax.ShapeDtypeStruct((B, S, D_MODEL), F32),
            jax.ShapeDtypeStruct((TOP_K, T), jnp.int32),
            jax.ShapeDtypeStruct((TOP_K, T), jnp.int32),
            jax.ShapeDtypeStruct((TOP_K, T), F32),
            jax.ShapeDtypeStruct((N_EXPERTS, LANES), F32),
        ],
        scratch_shapes=[pltpu.VMEM((N_EXPERTS, LANES), F32)],
        compiler_params=pltpu.CompilerParams(
            dimension_semantics=("arbitrary", "arbitrary"), vmem_limit_bytes=VMEM_LIMIT),
        name="postmix",
    )(o, x, gmla, gfox, wo, lnx, wxq, mkv, wxo, lnf, wr_hi, wr_lo, br, u)


def _scatter_kernel(hn_ref, dest_hbm, xs_hbm, dest_smem, idx_sem, row_sem):
    i = pl.program_id(0)
    n_idx = TS * TOP_K
    idx_cp = pltpu.make_async_copy(dest_hbm.at[pl.ds(pl.multiple_of(i * n_idx, n_idx), n_idx)],
                                   dest_smem, idx_sem)
    idx_cp.start()
    idx_cp.wait()

    def issue(t, _):
        for k in range(TOP_K):
            d = dest_smem[t * TOP_K + k]
            pltpu.make_async_copy(hn_ref.at[pl.ds(t, 1), :], xs_hbm.at[pl.ds(d, 1), :], row_sem).start()
        return 0

    lax.fori_loop(0, TS, issue, 0)
    for _ in range(TOP_K):
        pltpu.make_async_copy(hn_ref, xs_hbm.at[pl.ds(0, TS), :], row_sem).wait()


def _scatter(hn, dest_flat):
    T = hn.shape[0]
    return pl.pallas_call(
        _scatter_kernel,
        grid=(T // TS,),
        in_specs=[
            pl.BlockSpec((TS, D_MODEL), lambda i: (i, 0)),
            pl.BlockSpec(memory_space=pl.ANY),
        ],
        out_specs=pl.BlockSpec(memory_space=pl.ANY),
        out_shape=jax.ShapeDtypeStruct((T * TOP_K, D_MODEL), F32),
        scratch_shapes=[pltpu.SMEM((TS * TOP_K,), jnp.int32), pltpu.SemaphoreType.DMA,
                        pltpu.SemaphoreType.DMA],
        compiler_params=pltpu.CompilerParams(
            dimension_semantics=("arbitrary",), vmem_limit_bytes=VMEM_LIMIT),
        name="scatter",
    )(hn, dest_flat)


def _expert_kernel(tile_ref, exp_ref, lo_ref, hi_ref, first_ref,
                   xs_ref, wg_ref, wl_ref, bg_ref, bl_ref, wd_ref, bd_ref, y_ref):
    w = pl.program_id(0)
    lo = lo_ref[w]
    hi = hi_ref[w]

    @pl.when(hi > lo)
    def _():
        x = xs_ref[...].astype(BF16)
        g = _dot(x, wg_ref[0]) + bg_ref[0]
        l = _dot(x, wl_ref[0]) + bl_ref[0]
        glu = jnp.minimum(g, SWIGLU_LIMIT)
        lin = jnp.clip(l, -SWIGLU_LIMIT, SWIGLU_LIMIT)
        act = glu * (1.0 / (1.0 + jnp.exp(-SWIGLU_ALPHA * glu))) * (lin + 1.0)
        y = _dot(act.astype(BF16), wd_ref[0]) + bd_ref[0]
        r = tile_ref[w] * TM_EXP + lax.broadcasted_iota(jnp.int32, (TM_EXP, 1), 0)
        mine = jnp.logical_and(r >= lo, r < hi)
        is_first = first_ref[w] == 1

        @pl.when(is_first)
        def _():
            y_ref[...] = jnp.where(mine, y, 0.0)

        @pl.when(jnp.logical_not(is_first))
        def _():
            y_ref[...] = jnp.where(mine, y, y_ref[...])


def _experts(meta, xs, wg, wl, bg, bl, wd, bd):
    R = xs.shape[0]
    n_items = meta[0].shape[0]
    by_tile = lambda w, tile, exp, lo, hi, first: (tile[w], 0)
    by_exp = lambda w, tile, exp, lo, hi, first: (exp[w], 0, 0)
    grid_spec = pltpu.PrefetchScalarGridSpec(
        num_scalar_prefetch=5,
        grid=(n_items,),
        in_specs=[
            pl.BlockSpec((TM_EXP, D_MODEL), by_tile),
            pl.BlockSpec((1, D_MODEL, D_EXPERT), by_exp),
            pl.BlockSpec((1, D_MODEL, D_EXPERT), by_exp),
            pl.BlockSpec((1, 1, D_EXPERT), by_exp),
            pl.BlockSpec((1, 1, D_EXPERT), by_exp),
            pl.BlockSpec((1, D_EXPERT, D_MODEL), by_exp),
            pl.BlockSpec((1, 1, D_MODEL), by_exp),
        ],
        out_specs=pl.BlockSpec((TM_EXP, D_MODEL), by_tile),
    )
    return pl.pallas_call(
        _expert_kernel,
        grid_spec=grid_spec,
        out_shape=jax.ShapeDtypeStruct((R, D_MODEL), F32),
        compiler_params=pltpu.CompilerParams(
            dimension_semantics=("arbitrary",), vmem_limit_bytes=VMEM_LIMIT),
        name="experts",
    )(*meta, xs, wg, wl, bg, bl, wd, bd)


def _combine_kernel(h2_ref, gate_ref, lnf_ref, dest_hbm, y_hbm, out_ref, dest_smem, buf, idx_sem, row_sem):
    i = pl.program_id(0)
    n_idx = TC * TOP_K
    idx_cp = pltpu.make_async_copy(dest_hbm.at[pl.ds(pl.multiple_of(i * n_idx, n_idx), n_idx)],
                                   dest_smem, idx_sem)
    idx_cp.start()
    idx_cp.wait()

    def issue(t, _):
        for k in range(TOP_K):
            d = dest_smem[t * TOP_K + k]
            pltpu.make_async_copy(y_hbm.at[pl.ds(d, 1), :], buf.at[k, pl.ds(t, 1), :], row_sem).start()
        return 0

    lax.fori_loop(0, TC, issue, 0)
    for k in range(TOP_K):
        pltpu.make_async_copy(y_hbm.at[pl.ds(0, TC), :], buf.at[k], row_sem).wait()

    h = h2_ref[...]
    gates = gate_ref[...]
    for k in range(TOP_K):
        h = h + buf[k] * gates[:, k:k + 1]
    out_ref[...] = _rms(h) * lnf_ref[...]


def _combine(h2, gates_t, lnf, dest_flat, y):
    T = h2.shape[0]
    return pl.pallas_call(
        _combine_kernel,
        grid=(T // TC,),
        in_specs=[
            pl.BlockSpec((TC, D_MODEL), lambda i: (i, 0)),
            pl.BlockSpec((TC, TOP_K), lambda i: (i, 0)),
            pl.BlockSpec(lnf.shape, lambda i: (0, 0)),
            pl.BlockSpec(memory_space=pl.ANY),
            pl.BlockSpec(memory_space=pl.ANY),
        ],
        out_specs=pl.BlockSpec((TC, D_MODEL), lambda i: (i, 0)),
        out_shape=jax.ShapeDtypeStruct((T, D_MODEL), F32),
        scratch_shapes=[pltpu.SMEM((TC * TOP_K,), jnp.int32), pltpu.VMEM((TOP_K, TC, D_MODEL), F32),
                        pltpu.SemaphoreType.DMA, pltpu.SemaphoreType.DMA],
        compiler_params=pltpu.CompilerParams(
            dimension_semantics=("arbitrary",), vmem_limit_bytes=VMEM_LIMIT),
        name="combine",
    )(h2, gates_t, lnf, dest_flat, y)


def _rope_tables(S):
    half = MLA_ROPE // 2
    inv = ROPE_THETA ** (-jnp.arange(half, dtype=F32) / half)
    ang = jnp.arange(S, dtype=F32)[:, None] * inv[None, :]
    cos, sin = jnp.cos(ang), jnp.sin(ang)
    z = lambda n: jnp.zeros((S, n), F32)
    pad = HEAD_LANES - MLA_NOPE - MLA_ROPE
    c_tab = jnp.concatenate([jnp.ones((S, MLA_NOPE), F32), cos, cos, z(pad)], axis=1)
    s_tab = jnp.concatenate([z(MLA_NOPE), sin, sin, z(pad)], axis=1)
    return c_tab, s_tab


def _pad_cols(w, left, width=HEAD_LANES):
    return jnp.pad(w, ((0, 0), (left, width - left - w.shape[1])))


def _pack_mixer_weights(w_in, w_q_up, w_kv_up):
    half = MLA_ROPE // 2
    cuts = np.cumsum([MLA_Q_RANK, MLA_KV_RANK, MLA_ROPE, 512, 512, 512])
    w_ql, w_kvl, w_kpe, w_fq, w_fk, w_fv, w_fl = jnp.split(w_in, cuts, axis=1)
    kpe_swap = jnp.concatenate([-w_kpe[:, half:], w_kpe[:, :half]], axis=1)
    win = jnp.concatenate([
        w_ql, w_kvl, _pad_cols(w_kpe, MLA_NOPE), _pad_cols(kpe_swap, MLA_NOPE),
        w_fq, w_fk, w_fv, _pad_cols(w_fl, 0)], axis=1).astype(BF16)

    dq = MLA_NOPE + MLA_ROPE
    wq = w_q_up.reshape(MLA_Q_RANK, MLA_HEADS, dq)
    zq = lambda n: jnp.zeros((MLA_Q_RANK, MLA_HEADS, n), F32)
    wqm = jnp.concatenate([wq, zq(HEAD_LANES - dq)], axis=2)
    wqs = jnp.concatenate([zq(MLA_NOPE), -wq[:, :, MLA_NOPE + half:], wq[:, :, MLA_NOPE:MLA_NOPE + half],
                           zq(HEAD_LANES - dq)], axis=2)
    wkv = w_kv_up.reshape(MLA_KV_RANK, MLA_HEADS, MLA_NOPE + MLA_V)
    zk = lambda n: jnp.zeros((MLA_KV_RANK, MLA_HEADS, n), F32)
    wkk = jnp.concatenate([wkv[:, :, :MLA_NOPE], zk(HEAD_LANES - MLA_NOPE)], axis=2)
    v_even = jnp.concatenate([wkv[:, :, MLA_NOPE:], zk(HEAD_LANES - MLA_V)], axis=2)
    v_odd = jnp.concatenate([zk(HEAD_LANES - MLA_V), wkv[:, :, MLA_NOPE:]], axis=2)
    odd = (jnp.arange(MLA_HEADS) % 2 == 1)[None, :, None]
    wkvv = jnp.where(odd, v_odd, v_even)
    flat = lambda w: w.reshape(w.shape[0], MLA_HEADS * HEAD_LANES).astype(BF16)
    return win, flat(wqm), flat(wqs), flat(wkk), flat(wkvv)


def _fox_placement():
    eq = np.zeros((3 * LANES, FOX_HEADS * HEAD_LANES), np.float32)
    ek = np.zeros((3 * LANES, FOX_HEADS * HEAD_LANES), np.float32)
    oq = np.zeros((1, FOX_HEADS * HEAD_LANES), np.float32)
    ok = np.zeros((1, FOX_HEADS * HEAD_LANES), np.float32)
    for h in range(FOX_HEADS):
        base = h * HEAD_LANES + (FOX_HEAD_DIM if h % 2 == 0 else 0)
        for part in range(3):
            eq[part * LANES + h, base + part] = 1.0
            ok[0, base + part] = 1.0
            oq[0, base + 3 + part] = 1.0
            ek[part * LANES + h, base + 3 + part] = -1.0
    return jnp.asarray(eq, BF16), jnp.asarray(ek, BF16), jnp.asarray(oq), jnp.asarray(ok)


def _work_items(counts, n_rows):
    n_tiles = n_rows // TM_EXP
    n_items = n_tiles + N_EXPERTS - 1
    offs = jnp.concatenate([jnp.zeros((1,), jnp.int32), jnp.cumsum(counts)]).astype(jnp.int32)
    start, end = offs[:-1], offs[1:]
    first_tile = start // TM_EXP
    last_tile = jnp.maximum(end - 1, 0) // TM_EXP
    per_exp = jnp.where(counts > 0, last_tile - first_tile + 1, 0)
    item_end = jnp.cumsum(per_exp)
    item_start = item_end - per_exp
    total = item_end[-1]
    w = jnp.arange(n_items, dtype=jnp.int32)
    experts = jnp.arange(N_EXPERTS, dtype=jnp.int32)
    e = jnp.minimum(jnp.sum(item_end[None, :] <= w[:, None], axis=1), N_EXPERTS - 1).astype(jnp.int32)
    active = w < total
    last_e = jnp.max(jnp.where(counts > 0, experts, 0))
    e = jnp.where(active, e, last_e)
    hot = e[:, None] == experts[None, :]
    pick = lambda table: jnp.sum(jnp.where(hot, table[None, :], 0), axis=1)
    tile = jnp.where(active, pick(first_tile) + (w - pick(item_start)), n_tiles - 1).astype(jnp.int32)
    lo = jnp.maximum(pick(start), tile * TM_EXP)
    hi = jnp.minimum(pick(end), (tile + 1) * TM_EXP)
    lo = jnp.where(active, lo, 0).astype(jnp.int32)
    hi = jnp.where(active, hi, 0).astype(jnp.int32)
    prev_tile = jnp.concatenate([jnp.full((1,), -1, jnp.int32), tile[:-1]])
    first = (tile != prev_tile).astype(jnp.int32)
    return (tile, e, lo, hi, first), offs


def kernel(x, mem, ln_mix, w_in, q_norm, w_q_up, kv_norm, w_kv_up, b_forget, g_out_mla, g_out_fox, w_o,
           ln_cross, mem_norm, w_xq, w_mem_kv, w_xo, ln_ffn, w_router, b_router, w_gate_up, b_gate_up,
           w_down, b_down, ln_final):
    B, S, _ = x.shape
    T = B * S
    row = lambda v: v.reshape(1, -1).astype(F32)

    win, wqm, wqs, wkk, wkvv = _pack_mixer_weights(w_in[0], w_q_up[0], w_kv_up[0])
    c_tab, s_tab = _rope_tables(S)
    q_scale = (MLA_NOPE + MLA_ROPE) ** -0.5 * LOG2E
    eq, ek, oq, ok = _fox_placement()
    ltri = jnp.asarray(np.tril(np.ones((TM_PRE, TM_PRE), np.float32)), BF16)
    bf = _pad_cols(row(b_forget[0]), 0)
    q, k, v = _premix(x, row(ln_mix[0]), win, row(q_norm[0]), wqm, wqs, row(kv_norm[0]), wkk, wkvv,
                      c_tab * q_scale, s_tab * q_scale, c_tab, s_tab, bf, ltri, eq, ek, oq, ok)
    o = _attention(q, k, v)

    mkv = _memkv(mem, row(mem_norm[0]), w_mem_kv[0].astype(BF16))
    wr = w_router[0].T
    wr_hi = wr.astype(BF16)
    wr_lo = (wr - wr_hi.astype(F32)).astype(BF16)
    br = jnp.broadcast_to(b_router[0].astype(F32)[:, None], (N_EXPERTS, LANES))
    u = jnp.asarray(np.triu(np.ones((TM_POST, TM_POST), np.float32), 1), BF16)
    h2, hn3, idx, rank, gates, cnt = _postmix(
        o, x, row(g_out_mla[0]), row(g_out_fox[0]), w_o[0].astype(BF16), row(ln_cross[0]),
        w_xq[0].astype(BF16), mkv, w_xo[0].astype(BF16), row(ln_ffn[0]), wr_hi, wr_lo, br, u)

    counts = cnt[:, 0].astype(jnp.int32)
    meta, offs = _work_items(counts, T * TOP_K)
    start_of = jnp.sum(jnp.where(idx[..., None] == jnp.arange(N_EXPERTS, dtype=jnp.int32),
                                 offs[:N_EXPERTS], 0), axis=-1)
    dest = start_of + rank
    dest_flat = dest.T.reshape(-1)

    xs = _scatter(hn3.reshape(T, D_MODEL), dest_flat)
    wgu = w_gate_up[0].reshape(N_EXPERTS, D_MODEL, D_EXPERT, 2)
    bgu = b_gate_up[0].reshape(N_EXPERTS, 1, D_EXPERT, 2)
    y = _experts(meta, xs, wgu[..., 0].astype(BF16), wgu[..., 1].astype(BF16),
                 bgu[..., 0].astype(F32), bgu[..., 1].astype(F32),
                 w_down[0].astype(BF16), b_down[0].reshape(N_EXPERTS, 1, D_MODEL).astype(F32))

    out = _combine(h2.reshape(T, D_MODEL), gates.T, row(ln_final), dest_flat, y)
    return out.reshape(B, S, D_MODEL)
```

```python
import functools

import jax
import jax.numpy as jnp
import numpy as np
from jax import lax
from jax.experimental import pallas as pl
from jax.experimental.pallas import tpu as pltpu

F32 = jnp.float32
BF16 = jnp.bfloat16

D_MODEL = 1024
MEM_LEN = 256
MLA_HEADS = 8
MLA_NOPE = 64
MLA_ROPE = 32
MLA_V = 64
MLA_Q_RANK = 256
MLA_KV_RANK = 128
FOX_HEADS = 8
FOX_HEAD_DIM = 64
X_HEADS = 4
X_HEAD_DIM = D_MODEL // X_HEADS
N_EXPERTS = 32
TOP_K = 4
D_EXPERT = D_MODEL
SWIGLU_LIMIT = 7.0
SWIGLU_ALPHA = 1.702
ROPE_THETA = 10000.0
EPS = 1e-6

N_HEADS = MLA_HEADS + FOX_HEADS
HEAD_LANES = 128
LANES = 128
NEG_BIG = -1e30
LOG2E = 1.4426950408889634
V_ONES_EVEN = 64
V_ONES_ODD = 0

_C_QLAT = 0
_C_KVLAT = _C_QLAT + MLA_Q_RANK
_C_KPE = _C_KVLAT + MLA_KV_RANK
_C_KPE_SWAP = _C_KPE + LANES
_C_FQ = _C_KPE_SWAP + LANES
_C_FK = _C_FQ + FOX_HEADS * FOX_HEAD_DIM
_C_FV = _C_FK + FOX_HEADS * FOX_HEAD_DIM
_C_FLOGIT = _C_FV + FOX_HEADS * FOX_HEAD_DIM
IN_COLS_PACKED = _C_FLOGIT + LANES

VMEM_LIMIT = 56 * 1024 * 1024

TM_PRE = 512
TQ = 1024
TK = 1024
TD = 512
TM_POST = 512
TS = 1024
ISSUE_UNROLL = 4
TM_EXP = 512
TC = 256


def _rms(x, eps=EPS):
    return x * lax.rsqrt(jnp.mean(x * x, axis=-1, keepdims=True) + eps)


def _dot(a, b):
    return jnp.dot(a, b, preferred_element_type=F32)


def _dot_nt(a, b):
    return lax.dot_general(a, b, (((1,), (1,)), ((), ())), preferred_element_type=F32)


def _split3(x):
    hi = x.astype(BF16)
    r = x - hi.astype(F32)
    mid = r.astype(BF16)
    lo = (r - mid.astype(F32)).astype(BF16)
    return hi, mid, lo


def _premix_kernel(x_ref, g_ref, win_ref, qn_ref, wqm_ref, wqs_ref, kvn_ref, wkk_ref, wkv_ref,
                   cq_ref, sq_ref, ck_ref, sk_ref, bf_ref, ltri_ref, eq_ref, ek_ref, oq_ref, ok_ref,
                   q_out, k_out, v_out, carry_ref):
    i = pl.program_id(1)

    @pl.when(i == 0)
    def _():
        carry_ref[...] = jnp.zeros_like(carry_ref)

    x = x_ref[0]
    hn = (_rms(x) * g_ref[...]).astype(BF16)
    proj = _dot(hn, win_ref[...])

    qn = (_rms(proj[:, _C_QLAT:_C_QLAT + MLA_Q_RANK]) * qn_ref[...]).astype(BF16)
    qm = _dot(qn, wqm_ref[...])
    qs = _dot(qn, wqs_ref[...])
    cq = cq_ref[...]
    sq = sq_ref[...]
    for h in range(MLA_HEADS):
        sl = slice(h * HEAD_LANES, (h + 1) * HEAD_LANES)
        q_out[0, h] = (qm[:, sl] * cq + qs[:, sl] * sq).astype(BF16)

    kvn = (_rms(proj[:, _C_KVLAT:_C_KVLAT + MLA_KV_RANK]) * kvn_ref[...]).astype(BF16)
    kk = _dot(kvn, wkk_ref[...])
    vv = _dot(kvn, wkv_ref[...])
    kr = (proj[:, _C_KPE:_C_KPE + LANES] * ck_ref[...]
          + proj[:, _C_KPE_SWAP:_C_KPE_SWAP + LANES] * sk_ref[...])
    lane = lax.broadcasted_iota(jnp.int32, (TM_PRE, LANES), 1)
    low = lane < FOX_HEAD_DIM
    v_ones = (jnp.where(lane == V_ONES_EVEN, 1.0, 0.0), jnp.where(lane == V_ONES_ODD, 1.0, 0.0))
    for h in range(MLA_HEADS):
        sl = slice(h * HEAD_LANES, (h + 1) * HEAD_LANES)
        k_out[0, h] = (kk[:, sl] + kr).astype(BF16)
        v_out[0, h] = (vv[:, sl] + v_ones[h % 2]).astype(BF16)

    z = proj[:, _C_FLOGIT:_C_FLOGIT + LANES] + bf_ref[...]
    logf = jnp.minimum(z, 0.0) - jnp.log(1.0 + jnp.exp(-jnp.abs(z)))
    hi, mid, lo = _split3(logf)
    ltri = ltri_ref[...]
    cs = _dot(ltri, hi) + _dot(ltri, mid) + _dot(ltri, lo)
    c = cs + carry_ref[0:1, :]
    carry_ref[...] = jnp.broadcast_to(c[TM_PRE - 1:TM_PRE, :], carry_ref.shape)
    chi, cmid, clo = _split3(c * LOG2E)
    cparts = jnp.concatenate([chi, cmid, clo], axis=1)
    aug_q = _dot(cparts, eq_ref[...]) + oq_ref[...]
    aug_k = _dot(cparts, ek_ref[...]) + ok_ref[...]
    scale = FOX_HEAD_DIM ** -0.5 * LOG2E
    for p in range(FOX_HEADS // 2):
        fq = proj[:, _C_FQ + p * LANES:_C_FQ + (p + 1) * LANES] * scale
        fk = proj[:, _C_FK + p * LANES:_C_FK + (p + 1) * LANES]
        fv = proj[:, _C_FV + p * LANES:_C_FV + (p + 1) * LANES]
        for par in range(2):
            h = 2 * p + par
            keep = low if par == 0 else jnp.logical_not(low)
            sl = slice(h * HEAD_LANES, (h + 1) * HEAD_LANES)
            q_out[0, MLA_HEADS + h] = (jnp.where(keep, fq, 0.0) + aug_q[:, sl]).astype(BF16)
            k_out[0, MLA_HEADS + h] = (jnp.where(keep, fk, 0.0) + aug_k[:, sl]).astype(BF16)
            v_out[0, MLA_HEADS + h] = jnp.where(keep, fv, v_ones[par]).astype(BF16)


def _premix(x, g, win, qn, wqm, wqs, kvn, wkk, wkv, cq, sq, ck, sk, bf, ltri, eq, ek, oq, ok):
    B, S, _ = x.shape
    tm = TM_PRE
    const = lambda shape: pl.BlockSpec(shape, lambda b, i: (0,) * len(shape))
    rows = lambda w: pl.BlockSpec((tm, w), lambda b, i: (i, 0))
    head_out = pl.BlockSpec((1, N_HEADS, tm, HEAD_LANES), lambda b, i: (b, 0, i, 0))
    out_sds = jax.ShapeDtypeStruct((B, N_HEADS, S, HEAD_LANES), BF16)
    return pl.pallas_call(
        _premix_kernel,
        grid=(B, S // tm),
        in_specs=[
            pl.BlockSpec((1, tm, D_MODEL), lambda b, i: (b, i, 0)),
            const(g.shape), const(win.shape), const(qn.shape), const(wqm.shape), const(wqs.shape),
            const(kvn.shape), const(wkk.shape), const(wkv.shape),
            rows(LANES), rows(LANES), rows(LANES), rows(LANES),
            const(bf.shape), const(ltri.shape), const(eq.shape), const(ek.shape),
            const(oq.shape), const(ok.shape),
        ],
        out_specs=[head_out, head_out, head_out],
        out_shape=[out_sds, out_sds, out_sds],
        scratch_shapes=[pltpu.VMEM((8, LANES), F32)],
        compiler_params=pltpu.CompilerParams(
            dimension_semantics=("arbitrary", "arbitrary"), vmem_limit_bytes=VMEM_LIMIT),
        name="premix",
    )(x, g, win, qn, wqm, wqs, kvn, wkk, wkv, cq, sq, ck, sk, bf, ltri, eq, ek, oq, ok)


def _attn_kernel(q_ref, k_ref, v_ref, o_ref, m_sc, acc_sc):
    i = pl.program_id(2)
    r_loc = lax.broadcasted_iota(jnp.int32, (TD, TD), 0)
    c_loc = lax.broadcasted_iota(jnp.int32, (TD, TD), 1)
    m_sc[...] = jnp.full(m_sc.shape, NEG_BIG, F32)
    acc_sc[...] = jnp.zeros(acc_sc.shape, F32)

    def block(hh, r0, nr, key_start, nk, mask):
        rows = slice(r0, r0 + nr)
        k = k_ref[0, hh, pl.ds(key_start, nk), :]
        v = v_ref[0, hh, pl.ds(key_start, nk), :]
        s = _dot_nt(q_ref[0, hh, rows, :], k)
        if mask is not None:
            s = jnp.where(mask, s, NEG_BIG)
        m_old = m_sc[hh, rows, :]
        m_new = jnp.maximum(m_old, jnp.max(s, axis=-1, keepdims=True))
        p = jnp.exp2(s - jnp.concatenate([m_new] * (nk // LANES), axis=1))
        alpha = jnp.exp2(m_old - m_new)
        acc_sc[hh, rows, :] = alpha * acc_sc[hh, rows, :] + _dot(p.astype(BF16), v)
        m_sc[hh, rows, :] = m_new

    def full_tile(j, carry):
        start = pl.multiple_of(j * TK, TK)
        for hh in range(2):
            block(hh, 0, TQ, start, TK, None)
        return carry

    lax.fori_loop(0, i * (TQ // TK), full_tile, 0)

    for kt in range(TQ // TD):
        start = pl.multiple_of(i * TQ + kt * TD, TD)
        for hh in range(2):
            for qt in range(kt, TQ // TD):
                block(hh, qt * TD, TD, start, TD, (c_loc <= r_loc) if qt == kt else None)

    lane = lax.broadcasted_iota(jnp.int32, (TQ, HEAD_LANES), 1)
    a0 = acc_sc[0]
    a1 = acc_sc[1]
    out = jnp.where(lane < V_ONES_EVEN, a0 / a0[:, V_ONES_EVEN:V_ONES_EVEN + 1],
                    a1 / a1[:, V_ONES_ODD:V_ONES_ODD + 1])
    o_ref[0] = out.astype(o_ref.dtype)


def _attention(q, k, v):
    B, H, S, _ = q.shape
    return pl.pallas_call(
        _attn_kernel,
        grid=(B, H // 2, S // TQ),
        in_specs=[
            pl.BlockSpec((1, 2, TQ, HEAD_LANES), lambda b, p, i: (b, p, i, 0)),
            pl.BlockSpec((1, 2, S, HEAD_LANES), lambda b, p, i: (b, p, 0, 0)),
            pl.BlockSpec((1, 2, S, HEAD_LANES), lambda b, p, i: (b, p, 0, 0)),
        ],
        out_specs=pl.BlockSpec((1, TQ, HEAD_LANES), lambda b, p, i: (b, i, p)),
        out_shape=jax.ShapeDtypeStruct((B, S, (H // 2) * HEAD_LANES), BF16),
        scratch_shapes=[pltpu.VMEM((2, TQ, LANES), F32), pltpu.VMEM((2, TQ, HEAD_LANES), F32)],
        compiler_params=pltpu.CompilerParams(
            dimension_semantics=("arbitrary", "arbitrary", "arbitrary"), vmem_limit_bytes=VMEM_LIMIT),
        name="attn",
    )(q, k, v)


def _memkv_kernel(mem_ref, g_ref, w_ref, o_ref):
    mn = (_rms(mem_ref[0]) * g_ref[...]).astype(BF16)
    o_ref[0] = _dot(mn, w_ref[...]).astype(o_ref.dtype)


def _memkv(mem, g, w):
    B, M, _ = mem.shape
    return pl.pallas_call(
        _memkv_kernel,
        grid=(B,),
        in_specs=[
            pl.BlockSpec((1, M, D_MODEL), lambda b: (b, 0, 0)),
            pl.BlockSpec(g.shape, lambda b: (0, 0)),
            pl.BlockSpec(w.shape, lambda b: (0, 0)),
        ],
        out_specs=pl.BlockSpec((1, M, 2 * D_MODEL), lambda b: (b, 0, 0)),
        out_shape=jax.ShapeDtypeStruct((B, M, 2 * D_MODEL), BF16),
        compiler_params=pltpu.CompilerParams(
            dimension_semantics=("arbitrary",), vmem_limit_bytes=VMEM_LIMIT),
        name="memkv",
    )(mem, g, w)


def _postmix_kernel(o_ref, x_ref, gmla_ref, gfox_ref, wo_ref, lnx_ref, wxq_ref, mkv_ref, wxo_ref,
                    lnf_ref, wr_hi_ref, wr_lo_ref, br_ref, u_ref,
                    h2_out, hn_out, idx_out, rank_out, gate_out, cnt_out, cnt_ref):
    first = jnp.logical_and(pl.program_id(0) == 0, pl.program_id(1) == 0)

    @pl.when(first)
    def _():
        cnt_ref[...] = jnp.zeros_like(cnt_ref)

    tm = TM_POST
    half = D_MODEL // 2
    o = o_ref[0].astype(F32)
    on = jnp.concatenate([_rms(o[:, :half]) * gmla_ref[...], _rms(o[:, half:]) * gfox_ref[...]], axis=1)
    h1 = x_ref[0] + _dot(on.astype(BF16), wo_ref[...])

    hn2 = (_rms(h1) * lnx_ref[...]).astype(BF16)
    qx = _dot(hn2, wxq_ref[...]).astype(BF16)
    heads = []
    for h in range(X_HEADS):
        sl = slice(h * X_HEAD_DIM, (h + 1) * X_HEAD_DIM)
        kh = mkv_ref[0, :, sl]
        vh = mkv_ref[0, :, D_MODEL + h * X_HEAD_DIM:D_MODEL + (h + 1) * X_HEAD_DIM]
        s = _dot_nt(qx[:, sl], kh) * (X_HEAD_DIM ** -0.5)
        p = jnp.exp(s - jnp.max(s, axis=-1, keepdims=True))
        p = p / jnp.sum(p, axis=-1, keepdims=True)
        heads.append(_dot(p.astype(BF16), vh))
    ox = jnp.concatenate(heads, axis=1).astype(BF16)
    h2 = h1 + _dot(ox, wxo_ref[...])
    h2_out[0] = h2

    hn3 = _rms(h2) * lnf_ref[...]
    hn_out[0] = hn3
    a_hi = hn3.astype(BF16)
    a_lo = (hn3 - a_hi.astype(F32)).astype(BF16)
    logits = (_dot_nt(wr_hi_ref[...], a_hi) + _dot_nt(wr_hi_ref[...], a_lo)
              + _dot_nt(wr_lo_ref[...], a_hi)) + br_ref[:, 0:1]

    eid = lax.broadcasted_iota(jnp.int32, (N_EXPERTS, tm), 0).astype(F32)
    vals = logits
    top_v, top_i, hots = [], [], []
    for _ in range(TOP_K):
        mx = jnp.max(vals, axis=0, keepdims=True)
        sel = jnp.min(jnp.where(vals == mx, eid, float(N_EXPERTS)), axis=0, keepdims=True)
        hot = eid == sel
        vals = jnp.where(hot, -jnp.inf, vals)
        top_v.append(mx)
        top_i.append(sel.astype(jnp.int32))
        hots.append(hot)
    ex = [jnp.exp(v - top_v[0]) for v in top_v]
    den = ex[0] + ex[1] + ex[2] + ex[3]
    gate_out[...] = jnp.concatenate([e / den for e in ex], axis=0)
    idx_out[...] = jnp.concatenate(top_i, axis=0)

    hot_all = jnp.where(hots[0] | hots[1] | hots[2] | hots[3], 1.0, 0.0)
    before = _dot(hot_all.astype(BF16), u_ref[...]) + cnt_ref[:, 0:1]
    ranks = [jnp.sum(jnp.where(hot, before, 0.0), axis=0, keepdims=True) for hot in hots]
    rank_out[...] = jnp.concatenate(ranks, axis=0).astype(jnp.int32)
    cnt_new = cnt_ref[...] + jnp.sum(hot_all, axis=1, keepdims=True)
    cnt_ref[...] = cnt_new
    cnt_out[...] = cnt_new


def _postmix(o, x, gmla, gfox, wo, lnx, wxq, mkv, wxo, lnf, wr_hi, wr_lo, br, u):
    B, S, _ = x.shape
    tm = TM_POST
    nt = S // tm
    T = B * S
    const = lambda a: pl.BlockSpec(a.shape, lambda b, i: (0,) * a.ndim)
    tok = pl.BlockSpec((1, tm, D_MODEL), lambda b, i: (b, i, 0))
    route = pl.BlockSpec((TOP_K, tm), lambda b, i: (0, b * nt + i))
    return pl.pallas_call(
        _postmix_kernel,
        grid=(B, nt),
        in_specs=[
            tok, tok, const(gmla), const(gfox), const(wo), const(lnx), const(wxq),
            pl.BlockSpec((1, MEM_LEN, 2 * D_MODEL), lambda b, i: (b, 0, 0)),
            const(wxo), const(lnf), const(wr_hi), const(wr_lo), const(br), const(u),
        ],
        out_specs=[tok, tok, route, route, route,
                   pl.BlockSpec((N_EXPERTS, LANES), lambda b, i: (0, 0))],
        out_shape=[
            jax.ShapeDtypeStruct((B, S, D_MODEL), F32),
            jax.ShapeDtypeStruct((B, S, D_MODEL), F32),
            jax.ShapeDtypeStruct((TOP_K, T), jnp.int32),
            jax.ShapeDtypeStruct((TOP_K, T), jnp.int32),
            jax.ShapeDtypeStruct((TOP_K, T), F32),
            jax.ShapeDtypeStruct((N_EXPERTS, LANES), F32),
        ],
        scratch_shapes=[pltpu.VMEM((N_EXPERTS, LANES), F32)],
        compiler_params=pltpu.CompilerParams(
            dimension_semantics=("arbitrary", "arbitrary"), vmem_limit_bytes=VMEM_LIMIT),
        name="postmix",
    )(o, x, gmla, gfox, wo, lnx, wxq, mkv, wxo, lnf, wr_hi, wr_lo, br, u)


def _scatter_kernel(hn_ref, dest_hbm, xs_hbm, dest_smem, idx_sem, row_sem):
    i = pl.program_id(0)
    n_idx = TS * TOP_K
    idx_cp = pltpu.make_async_copy(dest_hbm.at[pl.ds(pl.multiple_of(i * n_idx, n_idx), n_idx)],
                                   dest_smem, idx_sem)
    idx_cp.start()
    idx_cp.wait()

    def issue(t, _):
        for k in range(TOP_K):
            d = dest_smem[t * TOP_K + k]
            pltpu.make_async_copy(hn_ref.at[pl.ds(t, 1), :], xs_hbm.at[pl.ds(d, 1), :],
                                  row_sem).start(priority=k % 2)
        return 0

    lax.fori_loop(0, TS, issue, 0, unroll=ISSUE_UNROLL)
    for _ in range(TOP_K):
        pltpu.make_async_copy(hn_ref, xs_hbm.at[pl.ds(0, TS), :], row_sem).wait()


def _scatter(hn, dest_flat):
    T = hn.shape[0]
    return pl.pallas_call(
        _scatter_kernel,
        grid=(T // TS,),
        in_specs=[
            pl.BlockSpec((TS, D_MODEL), lambda i: (i, 0)),
            pl.BlockSpec(memory_space=pl.ANY),
        ],
        out_specs=pl.BlockSpec(memory_space=pl.ANY),
        out_shape=jax.ShapeDtypeStruct((T * TOP_K, D_MODEL), F32),
        scratch_shapes=[pltpu.SMEM((TS * TOP_K,), jnp.int32), pltpu.SemaphoreType.DMA,
                        pltpu.SemaphoreType.DMA],
        compiler_params=pltpu.CompilerParams(
            dimension_semantics=("arbitrary",), vmem_limit_bytes=VMEM_LIMIT),
        name="scatter",
    )(hn, dest_flat)


def _expert_kernel(tile_ref, exp_ref, lo_ref, hi_ref, first_ref,
                   xs_ref, wg_ref, wl_ref, bg_ref, bl_ref, wd_ref, bd_ref, y_ref):
    w = pl.program_id(0)
    lo = lo_ref[w]
    hi = hi_ref[w]

    @pl.when(hi > lo)
    def _():
        x = xs_ref[...].astype(BF16)
        g = _dot(x, wg_ref[0]) + bg_ref[0]
        l = _dot(x, wl_ref[0]) + bl_ref[0]
        glu = jnp.minimum(g, SWIGLU_LIMIT)
        lin = jnp.clip(l, -SWIGLU_LIMIT, SWIGLU_LIMIT)
        act = glu * (1.0 / (1.0 + jnp.exp(-SWIGLU_ALPHA * glu))) * (lin + 1.0)
        y = _dot(act.astype(BF16), wd_ref[0]) + bd_ref[0]
        r = tile_ref[w] * TM_EXP + lax.broadcasted_iota(jnp.int32, (TM_EXP, 1), 0)
        mine = jnp.logical_and(r >= lo, r < hi)
        is_first = first_ref[w] == 1

        @pl.when(is_first)
        def _():
            y_ref[...] = jnp.where(mine, y, 0.0)

        @pl.when(jnp.logical_not(is_first))
        def _():
            y_ref[...] = jnp.where(mine, y, y_ref[...])


def _experts(meta, xs, wg, wl, bg, bl, wd, bd):
    R = xs.shape[0]
    n_items = meta[0].shape[0]
    by_tile = lambda w, tile, exp, lo, hi, first: (tile[w], 0)
    by_exp = lambda w, tile, exp, lo, hi, first: (exp[w], 0, 0)
    grid_spec = pltpu.PrefetchScalarGridSpec(
        num_scalar_prefetch=5,
        grid=(n_items,),
        in_specs=[
            pl.BlockSpec((TM_EXP, D_MODEL), by_tile),
            pl.BlockSpec((1, D_MODEL, D_EXPERT), by_exp),
            pl.BlockSpec((1, D_MODEL, D_EXPERT), by_exp),
            pl.BlockSpec((1, 1, D_EXPERT), by_exp),
            pl.BlockSpec((1, 1, D_EXPERT), by_exp),
            pl.BlockSpec((1, D_EXPERT, D_MODEL), by_exp),
            pl.BlockSpec((1, 1, D_MODEL), by_exp),
        ],
        out_specs=pl.BlockSpec((TM_EXP, D_MODEL), by_tile),
    )
    return pl.pallas_call(
        _expert_kernel,
        grid_spec=grid_spec,
        out_shape=jax.ShapeDtypeStruct((R, D_MODEL), F32),
        compiler_params=pltpu.CompilerParams(
            dimension_semantics=("arbitrary",), vmem_limit_bytes=VMEM_LIMIT),
        name="experts",
    )(*meta, xs, wg, wl, bg, bl, wd, bd)


def _combine_kernel(h2_ref, gate_ref, lnf_ref, dest_hbm, y_hbm, out_ref, dest_smem, buf, idx_sem, row_sem):
    i = pl.program_id(0)
    n = pl.num_programs(0)
    n_idx = TC * TOP_K
    slot = lax.rem(i, 2)
    nxt = 1 - slot

    def idx_copy(tile, s):
        return pltpu.make_async_copy(dest_hbm.at[pl.ds(pl.multiple_of(tile * n_idx, n_idx), n_idx)],
                                     dest_smem.at[pl.ds(pl.multiple_of(s * n_idx, n_idx), n_idx)],
                                     idx_sem.at[s])

    def issue_rows(s):
        def issue(t, _):
            for k in range(TOP_K):
                d = dest_smem[s * n_idx + t * TOP_K + k]
                pltpu.make_async_copy(y_hbm.at[pl.ds(d, 1), :], buf.at[s, k, pl.ds(t, 1), :],
                                      row_sem.at[s]).start(priority=k % 2)
            return 0
        lax.fori_loop(0, TC, issue, 0, unroll=ISSUE_UNROLL)

    @pl.when(i == 0)
    def _():
        idx_copy(0, 0).start()
        idx_copy(0, 0).wait()
        issue_rows(0)

        @pl.when(n > 1)
        def _():
            idx_copy(1, 1).start()

    @pl.when(i + 1 < n)
    def _():
        idx_copy(i + 1, nxt).wait()
        issue_rows(nxt)

    @pl.when(i + 2 < n)
    def _():
        idx_copy(i + 2, slot).start()

    for k in range(TOP_K):
        pltpu.make_async_copy(y_hbm.at[pl.ds(0, TC), :], buf.at[slot, k], row_sem.at[slot]).wait()

    h = h2_ref[...]
    gates = gate_ref[...]
    for k in range(TOP_K):
        h = h + buf[slot, k] * gates[:, k:k + 1]
    out_ref[...] = _rms(h) * lnf_ref[...]


def _combine(h2, gates_t, lnf, dest_flat, y):
    T = h2.shape[0]
    return pl.pallas_call(
        _combine_kernel,
        grid=(T // TC,),
        in_specs=[
            pl.BlockSpec((TC, D_MODEL), lambda i: (i, 0)),
            pl.BlockSpec((TC, TOP_K), lambda i: (i, 0)),
            pl.BlockSpec(lnf.shape, lambda i: (0, 0)),
            pl.BlockSpec(memory_space=pl.ANY),
            pl.BlockSpec(memory_space=pl.ANY),
        ],
        out_specs=pl.BlockSpec((TC, D_MODEL), lambda i: (i, 0)),
        out_shape=jax.ShapeDtypeStruct((T, D_MODEL), F32),
        scratch_shapes=[pltpu.SMEM((2 * TC * TOP_K,), jnp.int32), pltpu.VMEM((2, TOP_K, TC, D_MODEL), F32),
                        pltpu.SemaphoreType.DMA((2,)), pltpu.SemaphoreType.DMA((2,))],
        compiler_params=pltpu.CompilerParams(
            dimension_semantics=("arbitrary",), vmem_limit_bytes=VMEM_LIMIT),
        name="combine",
    )(h2, gates_t, lnf, dest_flat, y)


def _rope_tables(S):
    half = MLA_ROPE // 2
    inv = ROPE_THETA ** (-jnp.arange(half, dtype=F32) / half)
    ang = jnp.arange(S, dtype=F32)[:, None] * inv[None, :]
    cos, sin = jnp.cos(ang), jnp.sin(ang)
    z = lambda n: jnp.zeros((S, n), F32)
    pad = HEAD_LANES - MLA_NOPE - MLA_ROPE
    c_tab = jnp.concatenate([jnp.ones((S, MLA_NOPE), F32), cos, cos, z(pad)], axis=1)
    s_tab = jnp.concatenate([z(MLA_NOPE), sin, sin, z(pad)], axis=1)
    return c_tab, s_tab


def _pad_cols(w, left, width=HEAD_LANES):
    return jnp.pad(w, ((0, 0), (left, width - left - w.shape[1])))


def _pack_mixer_weights(w_in, w_q_up, w_kv_up):
    half = MLA_ROPE // 2
    cuts = np.cumsum([MLA_Q_RANK, MLA_KV_RANK, MLA_ROPE, 512, 512, 512])
    w_ql, w_kvl, w_kpe, w_fq, w_fk, w_fv, w_fl = jnp.split(w_in, cuts, axis=1)
    kpe_swap = jnp.concatenate([-w_kpe[:, half:], w_kpe[:, :half]], axis=1)
    win = jnp.concatenate([
        w_ql, w_kvl, _pad_cols(w_kpe, MLA_NOPE), _pad_cols(kpe_swap, MLA_NOPE),
        w_fq, w_fk, w_fv, _pad_cols(w_fl, 0)], axis=1).astype(BF16)

    dq = MLA_NOPE + MLA_ROPE
    wq = w_q_up.reshape(MLA_Q_RANK, MLA_HEADS, dq)
    zq = lambda n: jnp.zeros((MLA_Q_RANK, MLA_HEADS, n), F32)
    wqm = jnp.concatenate([wq, zq(HEAD_LANES - dq)], axis=2)
    wqs = jnp.concatenate([zq(MLA_NOPE), -wq[:, :, MLA_NOPE + half:], wq[:, :, MLA_NOPE:MLA_NOPE + half],
                           zq(HEAD_LANES - dq)], axis=2)
    wkv = w_kv_up.reshape(MLA_KV_RANK, MLA_HEADS, MLA_NOPE + MLA_V)
    zk = lambda n: jnp.zeros((MLA_KV_RANK, MLA_HEADS, n), F32)
    wkk = jnp.concatenate([wkv[:, :, :MLA_NOPE], zk(HEAD_LANES - MLA_NOPE)], axis=2)
    v_even = jnp.concatenate([wkv[:, :, MLA_NOPE:], zk(HEAD_LANES - MLA_V)], axis=2)
    v_odd = jnp.concatenate([zk(HEAD_LANES - MLA_V), wkv[:, :, MLA_NOPE:]], axis=2)
    odd = (jnp.arange(MLA_HEADS) % 2 == 1)[None, :, None]
    wkvv = jnp.where(odd, v_odd, v_even)
    flat = lambda w: w.reshape(w.shape[0], MLA_HEADS * HEAD_LANES).astype(BF16)
    return win, flat(wqm), flat(wqs), flat(wkk), flat(wkvv)


def _fox_placement():
    eq = np.zeros((3 * LANES, FOX_HEADS * HEAD_LANES), np.float32)
    ek = np.zeros((3 * LANES, FOX_HEADS * HEAD_LANES), np.float32)
    oq = np.zeros((1, FOX_HEADS * HEAD_LANES), np.float32)
    ok = np.zeros((1, FOX_HEADS * HEAD_LANES), np.float32)
    for h in range(FOX_HEADS):
        base = h * HEAD_LANES + (FOX_HEAD_DIM if h % 2 == 0 else 0)
        for part in range(3):
            eq[part * LANES + h, base + part] = 1.0
            ok[0, base + part] = 1.0
            oq[0, base + 3 + part] = 1.0
            ek[part * LANES + h, base + 3 + part] = -1.0
    return jnp.asarray(eq, BF16), jnp.asarray(ek, BF16), jnp.asarray(oq), jnp.asarray(ok)


def _work_items(counts, n_rows):
    n_tiles = n_rows // TM_EXP
    n_items = n_tiles + N_EXPERTS - 1
    offs = jnp.concatenate([jnp.zeros((1,), jnp.int32), jnp.cumsum(counts)]).astype(jnp.int32)
    start, end = offs[:-1], offs[1:]
    first_tile = start // TM_EXP
    last_tile = jnp.maximum(end - 1, 0) // TM_EXP
    per_exp = jnp.where(counts > 0, last_tile - first_tile + 1, 0)
    item_end = jnp.cumsum(per_exp)
    item_start = item_end - per_exp
    total = item_end[-1]
    w = jnp.arange(n_items, dtype=jnp.int32)
    experts = jnp.arange(N_EXPERTS, dtype=jnp.int32)
    e = jnp.minimum(jnp.sum(item_end[None, :] <= w[:, None], axis=1), N_EXPERTS - 1).astype(jnp.int32)
    active = w < total
    last_e = jnp.max(jnp.where(counts > 0, experts, 0))
    e = jnp.where(active, e, last_e)
    hot = e[:, None] == experts[None, :]
    pick = lambda table: jnp.sum(jnp.where(hot, table[None, :], 0), axis=1)
    tile = jnp.where(active, pick(first_tile) + (w - pick(item_start)), n_tiles - 1).astype(jnp.int32)
    lo = jnp.maximum(pick(start), tile * TM_EXP)
    hi = jnp.minimum(pick(end), (tile + 1) * TM_EXP)
    lo = jnp.where(active, lo, 0).astype(jnp.int32)
    hi = jnp.where(active, hi, 0).astype(jnp.int32)
    prev_tile = jnp.concatenate([jnp.full((1,), -1, jnp.int32), tile[:-1]])
    first = (tile != prev_tile).astype(jnp.int32)
    return (tile, e, lo, hi, first), offs


def kernel(x, mem, ln_mix, w_in, q_norm, w_q_up, kv_norm, w_kv_up, b_forget, g_out_mla, g_out_fox, w_o,
           ln_cross, mem_norm, w_xq, w_mem_kv, w_xo, ln_ffn, w_router, b_router, w_gate_up, b_gate_up,
           w_down, b_down, ln_final):
    B, S, _ = x.shape
    T = B * S
    row = lambda v: v.reshape(1, -1).astype(F32)

    win, wqm, wqs, wkk, wkvv = _pack_mixer_weights(w_in[0], w_q_up[0], w_kv_up[0])
    c_tab, s_tab = _rope_tables(S)
    q_scale = (MLA_NOPE + MLA_ROPE) ** -0.5 * LOG2E
    eq, ek, oq, ok = _fox_placement()
    ltri = jnp.asarray(np.tril(np.ones((TM_PRE, TM_PRE), np.float32)), BF16)
    bf = _pad_cols(row(b_forget[0]), 0)
    q, k, v = _premix(x, row(ln_mix[0]), win, row(q_norm[0]), wqm, wqs, row(kv_norm[0]), wkk, wkvv,
                      c_tab * q_scale, s_tab * q_scale, c_tab, s_tab, bf, ltri, eq, ek, oq, ok)
    o = _attention(q, k, v)

    mkv = _memkv(mem, row(mem_norm[0]), w_mem_kv[0].astype(BF16))
    wr = w_router[0].T
    wr_hi = wr.astype(BF16)
    wr_lo = (wr - wr_hi.astype(F32)).astype(BF16)
    br = jnp.broadcast_to(b_router[0].astype(F32)[:, None], (N_EXPERTS, LANES))
    u = jnp.asarray(np.triu(np.ones((TM_POST, TM_POST), np.float32), 1), BF16)
    h2, hn3, idx, rank, gates, cnt = _postmix(
        o, x, row(g_out_mla[0]), row(g_out_fox[0]), w_o[0].astype(BF16), row(ln_cross[0]),
        w_xq[0].astype(BF16), mkv, w_xo[0].astype(BF16), row(ln_ffn[0]), wr_hi, wr_lo, br, u)

    counts = cnt[:, 0].astype(jnp.int32)
    meta, offs = _work_items(counts, T * TOP_K)
    start_of = jnp.sum(jnp.where(idx[..., None] == jnp.arange(N_EXPERTS, dtype=jnp.int32),
                                 offs[:N_EXPERTS], 0), axis=-1)
    dest = start_of + rank
    dest_flat = dest.T.reshape(-1)

    xs = _scatter(hn3.reshape(T, D_MODEL), dest_flat)
    wgu = w_gate_up[0].reshape(N_EXPERTS, D_MODEL, D_EXPERT, 2)
    bgu = b_gate_up[0].reshape(N_EXPERTS, 1, D_EXPERT, 2)
    y = _experts(meta, xs, wgu[..., 0].astype(BF16), wgu[..., 1].astype(BF16),
                 bgu[..., 0].astype(F32), bgu[..., 1].astype(F32),
                 w_down[0].astype(BF16), b_down[0].reshape(N_EXPERTS, 1, D_MODEL).astype(F32))

    out = _combine(h2.reshape(T, D_MODEL), gates.T, row(ln_final), dest_flat, y)
    return out.reshape(B, S, D_MODEL)
```

```python
import functools

import jax
import jax.numpy as jnp
import numpy as np
from jax import lax
from jax.experimental import pallas as pl
from jax.experimental.pallas import tpu as pltpu

F32 = jnp.float32
BF16 = jnp.bfloat16

D_MODEL = 1024
MEM_LEN = 256
MLA_HEADS = 8
MLA_NOPE = 64
MLA_ROPE = 32
MLA_V = 64
MLA_Q_RANK = 256
MLA_KV_RANK = 128
FOX_HEADS = 8
FOX_HEAD_DIM = 64
X_HEADS = 4
X_HEAD_DIM = D_MODEL // X_HEADS
N_EXPERTS = 32
TOP_K = 4
D_EXPERT = D_MODEL
SWIGLU_LIMIT = 7.0
SWIGLU_ALPHA = 1.702
ROPE_THETA = 10000.0
EPS = 1e-6

N_HEADS = MLA_HEADS + FOX_HEADS
HEAD_LANES = 128
LANES = 128
NEG_BIG = -1e30
LOG2E = 1.4426950408889634
V_ONES_EVEN = 64
V_ONES_ODD = 0

_C_QLAT = 0
_C_KVLAT = _C_QLAT + MLA_Q_RANK
_C_KPE = _C_KVLAT + MLA_KV_RANK
_C_KPE_SWAP = _C_KPE + LANES
_C_FQ = _C_KPE_SWAP + LANES
_C_FK = _C_FQ + FOX_HEADS * FOX_HEAD_DIM
_C_FV = _C_FK + FOX_HEADS * FOX_HEAD_DIM
_C_FLOGIT = _C_FV + FOX_HEADS * FOX_HEAD_DIM
IN_COLS_PACKED = _C_FLOGIT + LANES

VMEM_LIMIT = 56 * 1024 * 1024

TM_PRE = 512
TQ = 1024
TK = 1024
TD = 512
TM_POST = 512
TS = 1024
ISSUE_UNROLL = 4
TM_EXP = 512
TC = 256


def _rms(x, eps=EPS):
    return x * lax.rsqrt(jnp.mean(x * x, axis=-1, keepdims=True) + eps)


def _dot(a, b):
    return jnp.dot(a, b, preferred_element_type=F32)


def _dot_nt(a, b):
    return lax.dot_general(a, b, (((1,), (1,)), ((), ())), preferred_element_type=F32)


def _split3(x):
    hi = x.astype(BF16)
    r = x - hi.astype(F32)
    mid = r.astype(BF16)
    lo = (r - mid.astype(F32)).astype(BF16)
    return hi, mid, lo


def _premix_kernel(x_ref, g_ref, win_ref, qn_ref, wqm_ref, wqs_ref, kvn_ref, wkk_ref, wkv_ref,
                   cq_ref, sq_ref, ck_ref, sk_ref, bf_ref, ltri_ref, eq_ref, ek_ref, oq_ref, ok_ref,
                   q_out, k_out, v_out, carry_ref):
    i = pl.program_id(1)

    @pl.when(i == 0)
    def _():
        carry_ref[...] = jnp.zeros_like(carry_ref)

    x = x_ref[0]
    hn = (_rms(x) * g_ref[...]).astype(BF16)
    proj = _dot(hn, win_ref[...])

    qn = (_rms(proj[:, _C_QLAT:_C_QLAT + MLA_Q_RANK]) * qn_ref[...]).astype(BF16)
    qm = _dot(qn, wqm_ref[...])
    qs = _dot(qn, wqs_ref[...])
    cq = cq_ref[...]
    sq = sq_ref[...]
    for h in range(MLA_HEADS):
        sl = slice(h * HEAD_LANES, (h + 1) * HEAD_LANES)
        q_out[0, h] = (qm[:, sl] * cq + qs[:, sl] * sq).astype(BF16)

    kvn = (_rms(proj[:, _C_KVLAT:_C_KVLAT + MLA_KV_RANK]) * kvn_ref[...]).astype(BF16)
    kk = _dot(kvn, wkk_ref[...])
    vv = _dot(kvn, wkv_ref[...])
    kr = (proj[:, _C_KPE:_C_KPE + LANES] * ck_ref[...]
          + proj[:, _C_KPE_SWAP:_C_KPE_SWAP + LANES] * sk_ref[...])
    lane = lax.broadcasted_iota(jnp.int32, (TM_PRE, LANES), 1)
    low = lane < FOX_HEAD_DIM
    v_ones = (jnp.where(lane == V_ONES_EVEN, 1.0, 0.0), jnp.where(lane == V_ONES_ODD, 1.0, 0.0))
    for h in range(MLA_HEADS):
        sl = slice(h * HEAD_LANES, (h + 1) * HEAD_LANES)
        k_out[0, h] = (kk[:, sl] + kr).astype(BF16)
        v_out[0, h] = (vv[:, sl] + v_ones[h % 2]).astype(BF16)

    z = proj[:, _C_FLOGIT:_C_FLOGIT + LANES] + bf_ref[...]
    logf = jnp.minimum(z, 0.0) - jnp.log(1.0 + jnp.exp(-jnp.abs(z)))
    hi, mid, lo = _split3(logf)
    ltri = ltri_ref[...]
    cs = _dot(ltri, hi) + _dot(ltri, mid) + _dot(ltri, lo)
    c = cs + carry_ref[0:1, :]
    carry_ref[...] = jnp.broadcast_to(c[TM_PRE - 1:TM_PRE, :], carry_ref.shape)
    chi, cmid, clo = _split3(c * LOG2E)
    cparts = jnp.concatenate([chi, cmid, clo], axis=1)
    aug_q = _dot(cparts, eq_ref[...]) + oq_ref[...]
    aug_k = _dot(cparts, ek_ref[...]) + ok_ref[...]
    scale = FOX_HEAD_DIM ** -0.5 * LOG2E
    for p in range(FOX_HEADS // 2):
        fq = proj[:, _C_FQ + p * LANES:_C_FQ + (p + 1) * LANES] * scale
        fk = proj[:, _C_FK + p * LANES:_C_FK + (p + 1) * LANES]
        fv = proj[:, _C_FV + p * LANES:_C_FV + (p + 1) * LANES]
        for par in range(2):
            h = 2 * p + par
            keep = low if par == 0 else jnp.logical_not(low)
            sl = slice(h * HEAD_LANES, (h + 1) * HEAD_LANES)
            q_out[0, MLA_HEADS + h] = (jnp.where(keep, fq, 0.0) + aug_q[:, sl]).astype(BF16)
            k_out[0, MLA_HEADS + h] = (jnp.where(keep, fk, 0.0) + aug_k[:, sl]).astype(BF16)
            v_out[0, MLA_HEADS + h] = jnp.where(keep, fv, v_ones[par]).astype(BF16)


def _premix(x, g, win, qn, wqm, wqs, kvn, wkk, wkv, cq, sq, ck, sk, bf, ltri, eq, ek, oq, ok):
    B, S, _ = x.shape
    tm = TM_PRE
    const = lambda shape: pl.BlockSpec(shape, lambda b, i: (0,) * len(shape))
    rows = lambda w: pl.BlockSpec((tm, w), lambda b, i: (i, 0))
    head_out = pl.BlockSpec((1, N_HEADS, tm, HEAD_LANES), lambda b, i: (b, 0, i, 0))
    out_sds = jax.ShapeDtypeStruct((B, N_HEADS, S, HEAD_LANES), BF16)
    return pl.pallas_call(
        _premix_kernel,
        grid=(B, S // tm),
        in_specs=[
            pl.BlockSpec((1, tm, D_MODEL), lambda b, i: (b, i, 0)),
            const(g.shape), const(win.shape), const(qn.shape), const(wqm.shape), const(wqs.shape),
            const(kvn.shape), const(wkk.shape), const(wkv.shape),
            rows(LANES), rows(LANES), rows(LANES), rows(LANES),
            const(bf.shape), const(ltri.shape), const(eq.shape), const(ek.shape),
            const(oq.shape), const(ok.shape),
        ],
        out_specs=[head_out, head_out, head_out],
        out_shape=[out_sds, out_sds, out_sds],
        scratch_shapes=[pltpu.VMEM((8, LANES), F32)],
        compiler_params=pltpu.CompilerParams(
            dimension_semantics=("arbitrary", "arbitrary"), vmem_limit_bytes=VMEM_LIMIT),
        name="premix",
    )(x, g, win, qn, wqm, wqs, kvn, wkk, wkv, cq, sq, ck, sk, bf, ltri, eq, ek, oq, ok)


def _attn_kernel(q_ref, k_ref, v_ref, o_ref, m_sc, acc_sc):
    i = pl.program_id(2)
    r_loc = lax.broadcasted_iota(jnp.int32, (TD, TD), 0)
    c_loc = lax.broadcasted_iota(jnp.int32, (TD, TD), 1)
    m_sc[...] = jnp.full(m_sc.shape, NEG_BIG, F32)
    acc_sc[...] = jnp.zeros(acc_sc.shape, F32)

    def block(hh, r0, nr, key_start, nk, mask):
        rows = slice(r0, r0 + nr)
        k = k_ref[0, hh, pl.ds(key_start, nk), :]
        v = v_ref[0, hh, pl.ds(key_start, nk), :]
        s = _dot_nt(q_ref[0, hh, rows, :], k)
        if mask is not None:
            s = jnp.where(mask, s, NEG_BIG)
        m_old = m_sc[hh, rows, :]
        m_new = jnp.maximum(m_old, jnp.max(s, axis=-1, keepdims=True))
        p = jnp.exp2(s - jnp.concatenate([m_new] * (nk // LANES), axis=1))
        alpha = jnp.exp2(m_old - m_new)
        acc_sc[hh, rows, :] = alpha * acc_sc[hh, rows, :] + _dot(p.astype(BF16), v)
        m_sc[hh, rows, :] = m_new

    def full_tile(j, carry):
        start = pl.multiple_of(j * TK, TK)
        for hh in range(2):
            block(hh, 0, TQ, start, TK, None)
        return carry

    lax.fori_loop(0, i * (TQ // TK), full_tile, 0)

    for kt in range(TQ // TD):
        start = pl.multiple_of(i * TQ + kt * TD, TD)
        for hh in range(2):
            for qt in range(kt, TQ // TD):
                block(hh, qt * TD, TD, start, TD, (c_loc <= r_loc) if qt == kt else None)

    lane = lax.broadcasted_iota(jnp.int32, (TQ, HEAD_LANES), 1)
    a0 = acc_sc[0]
    a1 = acc_sc[1]
    out = jnp.where(lane < V_ONES_EVEN, a0 / a0[:, V_ONES_EVEN:V_ONES_EVEN + 1],
                    a1 / a1[:, V_ONES_ODD:V_ONES_ODD + 1])
    o_ref[0] = out.astype(o_ref.dtype)


def _attention(q, k, v):
    B, H, S, _ = q.shape
    return pl.pallas_call(
        _attn_kernel,
        grid=(B, H // 2, S // TQ),
        in_specs=[
            pl.BlockSpec((1, 2, TQ, HEAD_LANES), lambda b, p, i: (b, p, i, 0)),
            pl.BlockSpec((1, 2, S, HEAD_LANES), lambda b, p, i: (b, p, 0, 0)),
            pl.BlockSpec((1, 2, S, HEAD_LANES), lambda b, p, i: (b, p, 0, 0)),
        ],
        out_specs=pl.BlockSpec((1, TQ, HEAD_LANES), lambda b, p, i: (b, i, p)),
        out_shape=jax.ShapeDtypeStruct((B, S, (H // 2) * HEAD_LANES), BF16),
        scratch_shapes=[pltpu.VMEM((2, TQ, LANES), F32), pltpu.VMEM((2, TQ, HEAD_LANES), F32)],
        compiler_params=pltpu.CompilerParams(
            dimension_semantics=("arbitrary", "arbitrary", "arbitrary"), vmem_limit_bytes=VMEM_LIMIT),
        name="attn",
    )(q, k, v)


def _memkv_kernel(mem_ref, g_ref, w_ref, o_ref):
    mn = (_rms(mem_ref[0]) * g_ref[...]).astype(BF16)
    o_ref[0] = _dot(mn, w_ref[...]).astype(o_ref.dtype)


def _memkv(mem, g, w):
    B, M, _ = mem.shape
    return pl.pallas_call(
        _memkv_kernel,
        grid=(B,),
        in_specs=[
            pl.BlockSpec((1, M, D_MODEL), lambda b: (b, 0, 0)),
            pl.BlockSpec(g.shape, lambda b: (0, 0)),
            pl.BlockSpec(w.shape, lambda b: (0, 0)),
        ],
        out_specs=pl.BlockSpec((1, M, 2 * D_MODEL), lambda b: (b, 0, 0)),
        out_shape=jax.ShapeDtypeStruct((B, M, 2 * D_MODEL), BF16),
        compiler_params=pltpu.CompilerParams(
            dimension_semantics=("arbitrary",), vmem_limit_bytes=VMEM_LIMIT),
        name="memkv",
    )(mem, g, w)


def _postmix_kernel(o_ref, x_ref, gmla_ref, gfox_ref, wo_ref, lnx_ref, wxq_ref, mkv_ref, wxo_ref,
                    lnf_ref, wr_hi_ref, wr_lo_ref, br_ref, u_ref,
                    h2_out, hn_out, idx_out, rank_out, gate_out, cnt_out, cnt_ref):
    first = jnp.logical_and(pl.program_id(0) == 0, pl.program_id(1) == 0)

    @pl.when(first)
    def _():
        cnt_ref[...] = jnp.zeros_like(cnt_ref)

    tm = TM_POST
    half = D_MODEL // 2
    o = o_ref[0].astype(F32)
    on = jnp.concatenate([_rms(o[:, :half]) * gmla_ref[...], _rms(o[:, half:]) * gfox_ref[...]], axis=1)
    h1 = x_ref[0] + _dot(on.astype(BF16), wo_ref[...])

    hn2 = (_rms(h1) * lnx_ref[...]).astype(BF16)
    qx = _dot(hn2, wxq_ref[...]).astype(BF16)
    heads = []
    for h in range(X_HEADS):
        sl = slice(h * X_HEAD_DIM, (h + 1) * X_HEAD_DIM)
        kh = mkv_ref[0, :, sl]
        vh = mkv_ref[0, :, D_MODEL + h * X_HEAD_DIM:D_MODEL + (h + 1) * X_HEAD_DIM]
        s = _dot_nt(qx[:, sl], kh) * (X_HEAD_DIM ** -0.5)
        p = jnp.exp(s - jnp.max(s, axis=-1, keepdims=True))
        p = p / jnp.sum(p, axis=-1, keepdims=True)
        heads.append(_dot(p.astype(BF16), vh))
    ox = jnp.concatenate(heads, axis=1).astype(BF16)
    h2 = h1 + _dot(ox, wxo_ref[...])
    h2_out[0] = h2

    hn3 = _rms(h2) * lnf_ref[...]
    hn_out[0] = hn3
    a_hi = hn3.astype(BF16)
    a_lo = (hn3 - a_hi.astype(F32)).astype(BF16)
    logits = (_dot_nt(wr_hi_ref[...], a_hi) + _dot_nt(wr_hi_ref[...], a_lo)
              + _dot_nt(wr_lo_ref[...], a_hi)) + br_ref[:, 0:1]

    eid = lax.broadcasted_iota(jnp.int32, (N_EXPERTS, tm), 0).astype(F32)
    vals = logits
    top_v, top_i, hots = [], [], []
    for _ in range(TOP_K):
        mx = jnp.max(vals, axis=0, keepdims=True)
        sel = jnp.min(jnp.where(vals == mx, eid, float(N_EXPERTS)), axis=0, keepdims=True)
        hot = eid == sel
        vals = jnp.where(hot, -jnp.inf, vals)
        top_v.append(mx)
        top_i.append(sel.astype(jnp.int32))
        hots.append(hot)
    ex = [jnp.exp(v - top_v[0]) for v in top_v]
    den = ex[0] + ex[1] + ex[2] + ex[3]
    gate_out[...] = jnp.concatenate([e / den for e in ex], axis=0)
    idx_out[...] = jnp.concatenate(top_i, axis=0)

    hot_all = jnp.where(hots[0] | hots[1] | hots[2] | hots[3], 1.0, 0.0)
    before = _dot(hot_all.astype(BF16), u_ref[...]) + cnt_ref[:, 0:1]
    ranks = [jnp.sum(jnp.where(hot, before, 0.0), axis=0, keepdims=True) for hot in hots]
    rank_out[...] = jnp.concatenate(ranks, axis=0).astype(jnp.int32)
    cnt_new = cnt_ref[...] + jnp.sum(hot_all, axis=1, keepdims=True)
    cnt_ref[...] = cnt_new
    cnt_out[...] = cnt_new


def _postmix(o, x, gmla, gfox, wo, lnx, wxq, mkv, wxo, lnf, wr_hi, wr_lo, br, u):
    B, S, _ = x.shape
    tm = TM_POST
    nt = S // tm
    T = B * S
    const = lambda a: pl.BlockSpec(a.shape, lambda b, i: (0,) * a.ndim)
    tok = pl.BlockSpec((1, tm, D_MODEL), lambda b, i: (b, i, 0))
    route = pl.BlockSpec((TOP_K, tm), lambda b, i: (0, b * nt + i))
    return pl.pallas_call(
        _postmix_kernel,
        grid=(B, nt),
        in_specs=[
            tok, tok, const(gmla), const(gfox), const(wo), const(lnx), const(wxq),
            pl.BlockSpec((1, MEM_LEN, 2 * D_MODEL), lambda b, i: (b, 0, 0)),
            const(wxo), const(lnf), const(wr_hi), const(wr_lo), const(br), const(u),
        ],
        out_specs=[tok, tok, route, route, route,
                   pl.BlockSpec((N_EXPERTS, LANES), lambda b, i: (0, 0))],
        out_shape=[
            jax.ShapeDtypeStruct((B, S, D_MODEL), F32),
            jax.ShapeDtypeStruct((B, S, D_MODEL), F32),
            jax.ShapeDtypeStruct((TOP_K, T), jnp.int32),
            jax.ShapeDtypeStruct((TOP_K, T), jnp.int32),
            jax.ShapeDtypeStruct((TOP_K, T), F32),
            jax.ShapeDtypeStruct((N_EXPERTS, LANES), F32),
        ],
        scratch_shapes=[pltpu.VMEM((N_EXPERTS, LANES), F32)],
        compiler_params=pltpu.CompilerParams(
            dimension_semantics=("arbitrary", "arbitrary"), vmem_limit_bytes=VMEM_LIMIT),
        name="postmix",
    )(o, x, gmla, gfox, wo, lnx, wxq, mkv, wxo, lnf, wr_hi, wr_lo, br, u)


def _scatter_kernel(hn_ref, dest_hbm, xs_hbm, dest_smem, idx_sem, row_sem):
    i = pl.program_id(0)
    n_idx = TS * TOP_K
    idx_cp = pltpu.make_async_copy(dest_hbm.at[pl.ds(pl.multiple_of(i * n_idx, n_idx), n_idx)],
                                   dest_smem, idx_sem)
    idx_cp.start()
    idx_cp.wait()

    def issue(t, _):
        for k in range(TOP_K):
            d = dest_smem[t * TOP_K + k]
            pltpu.make_async_copy(hn_ref.at[pl.ds(t, 1), :], xs_hbm.at[pl.ds(d, 1), :],
                                  row_sem).start(priority=k % 2)
        return 0

    lax.fori_loop(0, TS, issue, 0, unroll=ISSUE_UNROLL)
    for _ in range(TOP_K):
        pltpu.make_async_copy(hn_ref, xs_hbm.at[pl.ds(0, TS), :], row_sem).wait()


def _scatter(hn, dest_flat):
    T = hn.shape[0]
    return pl.pallas_call(
        _scatter_kernel,
        grid=(T // TS,),
        in_specs=[
            pl.BlockSpec((TS, D_MODEL), lambda i: (i, 0)),
            pl.BlockSpec(memory_space=pl.ANY),
        ],
        out_specs=pl.BlockSpec(memory_space=pl.ANY),
        out_shape=jax.ShapeDtypeStruct((T * TOP_K, D_MODEL), F32),
        scratch_shapes=[pltpu.SMEM((TS * TOP_K,), jnp.int32), pltpu.SemaphoreType.DMA,
                        pltpu.SemaphoreType.DMA],
        compiler_params=pltpu.CompilerParams(
            dimension_semantics=("arbitrary",), vmem_limit_bytes=VMEM_LIMIT),
        name="scatter",
    )(hn, dest_flat)


def _expert_kernel(tile_ref, exp_ref, lo_ref, hi_ref, first_ref, newexp_ref,
                   xs_ref, wgu_ref, bg_ref, bl_ref, wd_ref, bd_ref, perm_ref, y_ref, wg_s, wl_s, wd_s):
    w = pl.program_id(0)
    lo = lo_ref[w]
    hi = hi_ref[w]

    @pl.when(newexp_ref[w] == 1)
    def _():
        perm = perm_ref[...]
        for grp in range(2 * D_EXPERT // (2 * LANES)):
            blk = wgu_ref[0, :, grp * 2 * LANES:(grp + 1) * 2 * LANES].astype(BF16)
            res = _dot(blk, perm)
            wg_s[:, grp * LANES:(grp + 1) * LANES] = res[:, :LANES].astype(BF16)
            wl_s[:, grp * LANES:(grp + 1) * LANES] = res[:, LANES:].astype(BF16)
        wd_s[...] = wd_ref[0].astype(BF16)

    @pl.when(hi > lo)
    def _():
        x = xs_ref[...].astype(BF16)
        g = _dot(x, wg_s[...]) + bg_ref[0]
        l = _dot(x, wl_s[...]) + bl_ref[0]
        glu = jnp.minimum(g, SWIGLU_LIMIT)
        lin = jnp.clip(l, -SWIGLU_LIMIT, SWIGLU_LIMIT)
        act = glu * (1.0 / (1.0 + jnp.exp(-SWIGLU_ALPHA * glu))) * (lin + 1.0)
        y = _dot(act.astype(BF16), wd_s[...]) + bd_ref[0]
        r = tile_ref[w] * TM_EXP + lax.broadcasted_iota(jnp.int32, (TM_EXP, 1), 0)
        mine = jnp.logical_and(r >= lo, r < hi)
        is_first = first_ref[w] == 1

        @pl.when(is_first)
        def _():
            y_ref[...] = jnp.where(mine, y, 0.0)

        @pl.when(jnp.logical_not(is_first))
        def _():
            y_ref[...] = jnp.where(mine, y, y_ref[...])


def _experts(meta, xs, wgu, bg, bl, wd, bd):
    R = xs.shape[0]
    n_items = meta[0].shape[0]
    by_tile = lambda w, tile, exp, lo, hi, first, newexp: (tile[w], 0)
    by_exp = lambda w, tile, exp, lo, hi, first, newexp: (exp[w], 0, 0)
    perm = np.zeros((2 * LANES, 2 * LANES), np.float32)
    perm[2 * np.arange(LANES), np.arange(LANES)] = 1.0
    perm[2 * np.arange(LANES) + 1, LANES + np.arange(LANES)] = 1.0
    perm = jnp.asarray(perm, BF16)
    grid_spec = pltpu.PrefetchScalarGridSpec(
        num_scalar_prefetch=6,
        grid=(n_items,),
        in_specs=[
            pl.BlockSpec((TM_EXP, D_MODEL), by_tile),
            pl.BlockSpec((1, D_MODEL, 2 * D_EXPERT), by_exp),
            pl.BlockSpec((1, 1, D_EXPERT), by_exp),
            pl.BlockSpec((1, 1, D_EXPERT), by_exp),
            pl.BlockSpec((1, D_EXPERT, D_MODEL), by_exp),
            pl.BlockSpec((1, 1, D_MODEL), by_exp),
            pl.BlockSpec(perm.shape, lambda w, *_: (0, 0)),
        ],
        out_specs=pl.BlockSpec((TM_EXP, D_MODEL), by_tile),
        scratch_shapes=[pltpu.VMEM((D_MODEL, D_EXPERT), BF16), pltpu.VMEM((D_MODEL, D_EXPERT), BF16),
                        pltpu.VMEM((D_EXPERT, D_MODEL), BF16)],
    )
    return pl.pallas_call(
        _expert_kernel,
        grid_spec=grid_spec,
        out_shape=jax.ShapeDtypeStruct((R, D_MODEL), F32),
        compiler_params=pltpu.CompilerParams(
            dimension_semantics=("arbitrary",), vmem_limit_bytes=VMEM_LIMIT),
        name="experts",
    )(*meta, xs, wgu, bg, bl, wd, bd, perm)


def _combine_kernel(h2_ref, gate_ref, lnf_ref, dest_hbm, y_hbm, out_ref, dest_smem, buf, idx_sem, row_sem):
    i = pl.program_id(0)
    n = pl.num_programs(0)
    n_idx = TC * TOP_K
    slot = lax.rem(i, 2)
    nxt = 1 - slot

    def idx_copy(tile, s):
        return pltpu.make_async_copy(dest_hbm.at[pl.ds(pl.multiple_of(tile * n_idx, n_idx), n_idx)],
                                     dest_smem.at[pl.ds(pl.multiple_of(s * n_idx, n_idx), n_idx)],
                                     idx_sem.at[s])

    def issue_rows(s):
        def issue(t, _):
            for k in range(TOP_K):
                d = dest_smem[s * n_idx + t * TOP_K + k]
                pltpu.make_async_copy(y_hbm.at[pl.ds(d, 1), :], buf.at[s, k, pl.ds(t, 1), :],
                                      row_sem.at[s]).start(priority=k % 2)
            return 0
        lax.fori_loop(0, TC, issue, 0, unroll=ISSUE_UNROLL)

    @pl.when(i == 0)
    def _():
        idx_copy(0, 0).start()
        idx_copy(0, 0).wait()
        issue_rows(0)

        @pl.when(n > 1)
        def _():
            idx_copy(1, 1).start()

    @pl.when(i + 1 < n)
    def _():
        idx_copy(i + 1, nxt).wait()
        issue_rows(nxt)

    @pl.when(i + 2 < n)
    def _():
        idx_copy(i + 2, slot).start()

    for k in range(TOP_K):
        pltpu.make_async_copy(y_hbm.at[pl.ds(0, TC), :], buf.at[slot, k], row_sem.at[slot]).wait()

    h = h2_ref[...]
    gates = gate_ref[...]
    for k in range(TOP_K):
        h = h + buf[slot, k] * gates[:, k:k + 1]
    out_ref[...] = _rms(h) * lnf_ref[...]


def _combine(h2, gates_t, lnf, dest_flat, y):
    T = h2.shape[0]
    return pl.pallas_call(
        _combine_kernel,
        grid=(T // TC,),
        in_specs=[
            pl.BlockSpec((TC, D_MODEL), lambda i: (i, 0)),
            pl.BlockSpec((TC, TOP_K), lambda i: (i, 0)),
            pl.BlockSpec(lnf.shape, lambda i: (0, 0)),
            pl.BlockSpec(memory_space=pl.ANY),
            pl.BlockSpec(memory_space=pl.ANY),
        ],
        out_specs=pl.BlockSpec((TC, D_MODEL), lambda i: (i, 0)),
        out_shape=jax.ShapeDtypeStruct((T, D_MODEL), F32),
        scratch_shapes=[pltpu.SMEM((2 * TC * TOP_K,), jnp.int32), pltpu.VMEM((2, TOP_K, TC, D_MODEL), F32),
                        pltpu.SemaphoreType.DMA((2,)), pltpu.SemaphoreType.DMA((2,))],
        compiler_params=pltpu.CompilerParams(
            dimension_semantics=("arbitrary",), vmem_limit_bytes=VMEM_LIMIT),
        name="combine",
    )(h2, gates_t, lnf, dest_flat, y)


def _rope_tables(S):
    half = MLA_ROPE // 2
    inv = ROPE_THETA ** (-jnp.arange(half, dtype=F32) / half)
    ang = jnp.arange(S, dtype=F32)[:, None] * inv[None, :]
    cos, sin = jnp.cos(ang), jnp.sin(ang)
    z = lambda n: jnp.zeros((S, n), F32)
    pad = HEAD_LANES - MLA_NOPE - MLA_ROPE
    c_tab = jnp.concatenate([jnp.ones((S, MLA_NOPE), F32), cos, cos, z(pad)], axis=1)
    s_tab = jnp.concatenate([z(MLA_NOPE), sin, sin, z(pad)], axis=1)
    return c_tab, s_tab


def _pad_cols(w, left, width=HEAD_LANES):
    return jnp.pad(w, ((0, 0), (left, width - left - w.shape[1])))


def _pack_mixer_weights(w_in, w_q_up, w_kv_up):
    half = MLA_ROPE // 2
    cuts = np.cumsum([MLA_Q_RANK, MLA_KV_RANK, MLA_ROPE, 512, 512, 512])
    w_ql, w_kvl, w_kpe, w_fq, w_fk, w_fv, w_fl = jnp.split(w_in, cuts, axis=1)
    kpe_swap = jnp.concatenate([-w_kpe[:, half:], w_kpe[:, :half]], axis=1)
    win = jnp.concatenate([
        w_ql, w_kvl, _pad_cols(w_kpe, MLA_NOPE), _pad_cols(kpe_swap, MLA_NOPE),
        w_fq, w_fk, w_fv, _pad_cols(w_fl, 0)], axis=1).astype(BF16)

    dq = MLA_NOPE + MLA_ROPE
    wq = w_q_up.reshape(MLA_Q_RANK, MLA_HEADS, dq)
    zq = lambda n: jnp.zeros((MLA_Q_RANK, MLA_HEADS, n), F32)
    wqm = jnp.concatenate([wq, zq(HEAD_LANES - dq)], axis=2)
    wqs = jnp.concatenate([zq(MLA_NOPE), -wq[:, :, MLA_NOPE + half:], wq[:, :, MLA_NOPE:MLA_NOPE + half],
                           zq(HEAD_LANES - dq)], axis=2)
    wkv = w_kv_up.reshape(MLA_KV_RANK, MLA_HEADS, MLA_NOPE + MLA_V)
    zk = lambda n: jnp.zeros((MLA_KV_RANK, MLA_HEADS, n), F32)
    wkk = jnp.concatenate([wkv[:, :, :MLA_NOPE], zk(HEAD_LANES - MLA_NOPE)], axis=2)
    v_even = jnp.concatenate([wkv[:, :, MLA_NOPE:], zk(HEAD_LANES - MLA_V)], axis=2)
    v_odd = jnp.concatenate([zk(HEAD_LANES - MLA_V), wkv[:, :, MLA_NOPE:]], axis=2)
    odd = (jnp.arange(MLA_HEADS) % 2 == 1)[None, :, None]
    wkvv = jnp.where(odd, v_odd, v_even)
    flat = lambda w: w.reshape(w.shape[0], MLA_HEADS * HEAD_LANES).astype(BF16)
    return win, flat(wqm), flat(wqs), flat(wkk), flat(wkvv)


def _fox_placement():
    eq = np.zeros((3 * LANES, FOX_HEADS * HEAD_LANES), np.float32)
    ek = np.zeros((3 * LANES, FOX_HEADS * HEAD_LANES), np.float32)
    oq = np.zeros((1, FOX_HEADS * HEAD_LANES), np.float32)
    ok = np.zeros((1, FOX_HEADS * HEAD_LANES), np.float32)
    for h in range(FOX_HEADS):
        base = h * HEAD_LANES + (FOX_HEAD_DIM if h % 2 == 0 else 0)
        for part in range(3):
            eq[part * LANES + h, base + part] = 1.0
            ok[0, base + part] = 1.0
            oq[0, base + 3 + part] = 1.0
            ek[part * LANES + h, base + 3 + part] = -1.0
    return jnp.asarray(eq, BF16), jnp.asarray(ek, BF16), jnp.asarray(oq), jnp.asarray(ok)


def _work_items(counts, n_rows):
    n_tiles = n_rows // TM_EXP
    n_items = n_tiles + N_EXPERTS - 1
    offs = jnp.concatenate([jnp.zeros((1,), jnp.int32), jnp.cumsum(counts)]).astype(jnp.int32)
    start, end = offs[:-1], offs[1:]
    first_tile = start // TM_EXP
    last_tile = jnp.maximum(end - 1, 0) // TM_EXP
    per_exp = jnp.where(counts > 0, last_tile - first_tile + 1, 0)
    item_end = jnp.cumsum(per_exp)
    item_start = item_end - per_exp
    total = item_end[-1]
    w = jnp.arange(n_items, dtype=jnp.int32)
    experts = jnp.arange(N_EXPERTS, dtype=jnp.int32)
    e = jnp.minimum(jnp.sum(item_end[None, :] <= w[:, None], axis=1), N_EXPERTS - 1).astype(jnp.int32)
    active = w < total
    last_e = jnp.max(jnp.where(counts > 0, experts, 0))
    e = jnp.where(active, e, last_e)
    hot = e[:, None] == experts[None, :]
    pick = lambda table: jnp.sum(jnp.where(hot, table[None, :], 0), axis=1)
    tile = jnp.where(active, pick(first_tile) + (w - pick(item_start)), n_tiles - 1).astype(jnp.int32)
    lo = jnp.maximum(pick(start), tile * TM_EXP)
    hi = jnp.minimum(pick(end), (tile + 1) * TM_EXP)
    lo = jnp.where(active, lo, 0).astype(jnp.int32)
    hi = jnp.where(active, hi, 0).astype(jnp.int32)
    prev_tile = jnp.concatenate([jnp.full((1,), -1, jnp.int32), tile[:-1]])
    first = (tile != prev_tile).astype(jnp.int32)
    prev_e = jnp.concatenate([jnp.full((1,), -1, jnp.int32), e[:-1]])
    newexp = (e != prev_e).astype(jnp.int32)
    return (tile, e, lo, hi, first, newexp), offs


def kernel(x, mem, ln_mix, w_in, q_norm, w_q_up, kv_norm, w_kv_up, b_forget, g_out_mla, g_out_fox, w_o,
           ln_cross, mem_norm, w_xq, w_mem_kv, w_xo, ln_ffn, w_router, b_router, w_gate_up, b_gate_up,
           w_down, b_down, ln_final):
    B, S, _ = x.shape
    T = B * S
    row = lambda v: v.reshape(1, -1).astype(F32)

    win, wqm, wqs, wkk, wkvv = _pack_mixer_weights(w_in[0], w_q_up[0], w_kv_up[0])
    c_tab, s_tab = _rope_tables(S)
    q_scale = (MLA_NOPE + MLA_ROPE) ** -0.5 * LOG2E
    eq, ek, oq, ok = _fox_placement()
    ltri = jnp.asarray(np.tril(np.ones((TM_PRE, TM_PRE), np.float32)), BF16)
    bf = _pad_cols(row(b_forget[0]), 0)
    q, k, v = _premix(x, row(ln_mix[0]), win, row(q_norm[0]), wqm, wqs, row(kv_norm[0]), wkk, wkvv,
                      c_tab * q_scale, s_tab * q_scale, c_tab, s_tab, bf, ltri, eq, ek, oq, ok)
    o = _attention(q, k, v)

    mkv = _memkv(mem, row(mem_norm[0]), w_mem_kv[0].astype(BF16))
    wr = w_router[0].T
    wr_hi = wr.astype(BF16)
    wr_lo = (wr - wr_hi.astype(F32)).astype(BF16)
    br = jnp.broadcast_to(b_router[0].astype(F32)[:, None], (N_EXPERTS, LANES))
    u = jnp.asarray(np.triu(np.ones((TM_POST, TM_POST), np.float32), 1), BF16)
    h2, hn3, idx, rank, gates, cnt = _postmix(
        o, x, row(g_out_mla[0]), row(g_out_fox[0]), w_o[0].astype(BF16), row(ln_cross[0]),
        w_xq[0].astype(BF16), mkv, w_xo[0].astype(BF16), row(ln_ffn[0]), wr_hi, wr_lo, br, u)

    counts = cnt[:, 0].astype(jnp.int32)
    meta, offs = _work_items(counts, T * TOP_K)
    start_of = jnp.sum(jnp.where(idx[..., None] == jnp.arange(N_EXPERTS, dtype=jnp.int32),
                                 offs[:N_EXPERTS], 0), axis=-1)
    dest = start_of + rank
    dest_flat = dest.T.reshape(-1)

    xs = _scatter(hn3.reshape(T, D_MODEL), dest_flat)
    bgu = b_gate_up[0].reshape(N_EXPERTS, 1, D_EXPERT, 2).astype(F32)
    y = _experts(meta, xs, w_gate_up[0], bgu[..., 0], bgu[..., 1],
                 w_down[0], b_down[0].reshape(N_EXPERTS, 1, D_MODEL).astype(F32))

    out = _combine(h2.reshape(T, D_MODEL), gates.T, row(ln_final), dest_flat, y)
    return out.reshape(B, S, D_MODEL)
```

```python
import functools

import jax
import jax.numpy as jnp
import numpy as np
from jax import lax
from jax.experimental import pallas as pl
from jax.experimental.pallas import tpu as pltpu

F32 = jnp.float32
BF16 = jnp.bfloat16

D_MODEL = 1024
MEM_LEN = 256
MLA_HEADS = 8
MLA_NOPE = 64
MLA_ROPE = 32
MLA_V = 64
MLA_Q_RANK = 256
MLA_KV_RANK = 128
FOX_HEADS = 8
FOX_HEAD_DIM = 64
X_HEADS = 4
X_HEAD_DIM = D_MODEL // X_HEADS
N_EXPERTS = 32
TOP_K = 4
D_EXPERT = D_MODEL
SWIGLU_LIMIT = 7.0
SWIGLU_ALPHA = 1.702
ROPE_THETA = 10000.0
EPS = 1e-6

N_HEADS = MLA_HEADS + FOX_HEADS
HEAD_LANES = 128
LANES = 128
NEG_BIG = -1e30
LOG2E = 1.4426950408889634
V_ONES_EVEN = 64
V_ONES_ODD = 0

_C_QLAT = 0
_C_KVLAT = _C_QLAT + MLA_Q_RANK
_C_KPE = _C_KVLAT + MLA_KV_RANK
_C_KPE_SWAP = _C_KPE + LANES
_C_FQ = _C_KPE_SWAP + LANES
_C_FK = _C_FQ + FOX_HEADS * FOX_HEAD_DIM
_C_FV = _C_FK + FOX_HEADS * FOX_HEAD_DIM
_C_FLOGIT = _C_FV + FOX_HEADS * FOX_HEAD_DIM
IN_COLS_PACKED = _C_FLOGIT + LANES

VMEM_LIMIT = 56 * 1024 * 1024

TM_PRE = 512
TQ = 1024
TK = 1024
TD = 512
TM_POST = 512
TM_EXP = 512
RUN_ALIGN = 8
RUN_SIZES = (512, 256, 128, 64, 32, 16, 8)
STAGE_ROWS = TOP_K * TM_POST + N_EXPERTS * RUN_ALIGN


def _rms(x, eps=EPS):
    return x * lax.rsqrt(jnp.mean(x * x, axis=-1, keepdims=True) + eps)


def _dot(a, b):
    return jnp.dot(a, b, preferred_element_type=F32)


def _dot_nt(a, b):
    return lax.dot_general(a, b, (((1,), (1,)), ((), ())), preferred_element_type=F32)


def _split3(x):
    hi = x.astype(BF16)
    r = x - hi.astype(F32)
    mid = r.astype(BF16)
    lo = (r - mid.astype(F32)).astype(BF16)
    return hi, mid, lo


def _premix_kernel(x_ref, g_ref, win_ref, qn_ref, wqm_ref, wqs_ref, kvn_ref, wkk_ref, wkv_ref,
                   cq_ref, sq_ref, ck_ref, sk_ref, bf_ref, ltri_ref, eq_ref, ek_ref, oq_ref, ok_ref,
                   q_out, k_out, v_out, carry_ref):
    i = pl.program_id(1)

    @pl.when(i == 0)
    def _():
        carry_ref[...] = jnp.zeros_like(carry_ref)

    x = x_ref[0]
    hn = (_rms(x) * g_ref[...]).astype(BF16)
    proj = _dot(hn, win_ref[...])

    qn = (_rms(proj[:, _C_QLAT:_C_QLAT + MLA_Q_RANK]) * qn_ref[...]).astype(BF16)
    qm = _dot(qn, wqm_ref[...])
    qs = _dot(qn, wqs_ref[...])
    cq = cq_ref[...]
    sq = sq_ref[...]
    for h in range(MLA_HEADS):
        sl = slice(h * HEAD_LANES, (h + 1) * HEAD_LANES)
        q_out[0, h] = (qm[:, sl] * cq + qs[:, sl] * sq).astype(BF16)

    kvn = (_rms(proj[:, _C_KVLAT:_C_KVLAT + MLA_KV_RANK]) * kvn_ref[...]).astype(BF16)
    kk = _dot(kvn, wkk_ref[...])
    vv = _dot(kvn, wkv_ref[...])
    kr = (proj[:, _C_KPE:_C_KPE + LANES] * ck_ref[...]
          + proj[:, _C_KPE_SWAP:_C_KPE_SWAP + LANES] * sk_ref[...])
    lane = lax.broadcasted_iota(jnp.int32, (TM_PRE, LANES), 1)
    low = lane < FOX_HEAD_DIM
    v_ones = (jnp.where(lane == V_ONES_EVEN, 1.0, 0.0), jnp.where(lane == V_ONES_ODD, 1.0, 0.0))
    for h in range(MLA_HEADS):
        sl = slice(h * HEAD_LANES, (h + 1) * HEAD_LANES)
        k_out[0, h] = (kk[:, sl] + kr).astype(BF16)
        v_out[0, h] = (vv[:, sl] + v_ones[h % 2]).astype(BF16)

    z = proj[:, _C_FLOGIT:_C_FLOGIT + LANES] + bf_ref[...]
    logf = jnp.minimum(z, 0.0) - jnp.log(1.0 + jnp.exp(-jnp.abs(z)))
    hi, mid, lo = _split3(logf)
    ltri = ltri_ref[...]
    cs = _dot(ltri, hi) + _dot(ltri, mid) + _dot(ltri, lo)
    c = cs + carry_ref[0:1, :]
    carry_ref[...] = jnp.broadcast_to(c[TM_PRE - 1:TM_PRE, :], carry_ref.shape)
    chi, cmid, clo = _split3(c * LOG2E)
    cparts = jnp.concatenate([chi, cmid, clo], axis=1)
    aug_q = _dot(cparts, eq_ref[...]) + oq_ref[...]
    aug_k = _dot(cparts, ek_ref[...]) + ok_ref[...]
    scale = FOX_HEAD_DIM ** -0.5 * LOG2E
    for p in range(FOX_HEADS // 2):
        fq = proj[:, _C_FQ + p * LANES:_C_FQ + (p + 1) * LANES] * scale
        fk = proj[:, _C_FK + p * LANES:_C_FK + (p + 1) * LANES]
        fv = proj[:, _C_FV + p * LANES:_C_FV + (p + 1) * LANES]
        for par in range(2):
            h = 2 * p + par
            keep = low if par == 0 else jnp.logical_not(low)
            sl = slice(h * HEAD_LANES, (h + 1) * HEAD_LANES)
            q_out[0, MLA_HEADS + h] = (jnp.where(keep, fq, 0.0) + aug_q[:, sl]).astype(BF16)
            k_out[0, MLA_HEADS + h] = (jnp.where(keep, fk, 0.0) + aug_k[:, sl]).astype(BF16)
            v_out[0, MLA_HEADS + h] = jnp.where(keep, fv, v_ones[par]).astype(BF16)


def _premix(x, g, win, qn, wqm, wqs, kvn, wkk, wkv, cq, sq, ck, sk, bf, ltri, eq, ek, oq, ok):
    B, S, _ = x.shape
    tm = TM_PRE
    const = lambda shape: pl.BlockSpec(shape, lambda b, i: (0,) * len(shape))
    rows = lambda w: pl.BlockSpec((tm, w), lambda b, i: (i, 0))
    head_out = pl.BlockSpec((1, N_HEADS, tm, HEAD_LANES), lambda b, i: (b, 0, i, 0))
    out_sds = jax.ShapeDtypeStruct((B, N_HEADS, S, HEAD_LANES), BF16)
    return pl.pallas_call(
        _premix_kernel,
        grid=(B, S // tm),
        in_specs=[
            pl.BlockSpec((1, tm, D_MODEL), lambda b, i: (b, i, 0)),
            const(g.shape), const(win.shape), const(qn.shape), const(wqm.shape), const(wqs.shape),
            const(kvn.shape), const(wkk.shape), const(wkv.shape),
            rows(LANES), rows(LANES), rows(LANES), rows(LANES),
            const(bf.shape), const(ltri.shape), const(eq.shape), const(ek.shape),
            const(oq.shape), const(ok.shape),
        ],
        out_specs=[head_out, head_out, head_out],
        out_shape=[out_sds, out_sds, out_sds],
        scratch_shapes=[pltpu.VMEM((8, LANES), F32)],
        compiler_params=pltpu.CompilerParams(
            dimension_semantics=("arbitrary", "arbitrary"), vmem_limit_bytes=VMEM_LIMIT),
        name="premix",
    )(x, g, win, qn, wqm, wqs, kvn, wkk, wkv, cq, sq, ck, sk, bf, ltri, eq, ek, oq, ok)


def _attn_kernel(q_ref, k_ref, v_ref, o_ref, m_sc, acc_sc):
    i = pl.program_id(2)
    r_loc = lax.broadcasted_iota(jnp.int32, (TD, TD), 0)
    c_loc = lax.broadcasted_iota(jnp.int32, (TD, TD), 1)
    m_sc[...] = jnp.full(m_sc.shape, NEG_BIG, F32)
    acc_sc[...] = jnp.zeros(acc_sc.shape, F32)

    def block(hh, r0, nr, key_start, nk, mask):
        rows = slice(r0, r0 + nr)
        k = k_ref[0, hh, pl.ds(key_start, nk), :]
        v = v_ref[0, hh, pl.ds(key_start, nk), :]
        s = _dot_nt(q_ref[0, hh, rows, :], k)
        if mask is not None:
            s = jnp.where(mask, s, NEG_BIG)
        m_old = m_sc[hh, rows, :]
        m_new = jnp.maximum(m_old, jnp.max(s, axis=-1, keepdims=True))
        p = jnp.exp2(s - jnp.concatenate([m_new] * (nk // LANES), axis=1))
        alpha = jnp.exp2(m_old - m_new)
        acc_sc[hh, rows, :] = alpha * acc_sc[hh, rows, :] + _dot(p.astype(BF16), v)
        m_sc[hh, rows, :] = m_new

    def full_tile(j, carry):
        start = pl.multiple_of(j * TK, TK)
        for hh in range(2):
            block(hh, 0, TQ, start, TK, None)
        return carry

    lax.fori_loop(0, i * (TQ // TK), full_tile, 0)

    for kt in range(TQ // TD):
        start = pl.multiple_of(i * TQ + kt * TD, TD)
        for hh in range(2):
            for qt in range(kt, TQ // TD):
                block(hh, qt * TD, TD, start, TD, (c_loc <= r_loc) if qt == kt else None)

    lane = lax.broadcasted_iota(jnp.int32, (TQ, HEAD_LANES), 1)
    a0 = acc_sc[0]
    a1 = acc_sc[1]
    out = jnp.where(lane < V_ONES_EVEN, a0 / a0[:, V_ONES_EVEN:V_ONES_EVEN + 1],
                    a1 / a1[:, V_ONES_ODD:V_ONES_ODD + 1])
    o_ref[0] = out.astype(o_ref.dtype)


def _attention(q, k, v):
    B, H, S, _ = q.shape
    return pl.pallas_call(
        _attn_kernel,
        grid=(B, H // 2, S // TQ),
        in_specs=[
            pl.BlockSpec((1, 2, TQ, HEAD_LANES), lambda b, p, i: (b, p, i, 0)),
            pl.BlockSpec((1, 2, S, HEAD_LANES), lambda b, p, i: (b, p, 0, 0)),
            pl.BlockSpec((1, 2, S, HEAD_LANES), lambda b, p, i: (b, p, 0, 0)),
        ],
        out_specs=pl.BlockSpec((1, TQ, HEAD_LANES), lambda b, p, i: (b, i, p)),
        out_shape=jax.ShapeDtypeStruct((B, S, (H // 2) * HEAD_LANES), BF16),
        scratch_shapes=[pltpu.VMEM((2, TQ, LANES), F32), pltpu.VMEM((2, TQ, HEAD_LANES), F32)],
        compiler_params=pltpu.CompilerParams(
            dimension_semantics=("arbitrary", "arbitrary", "arbitrary"), vmem_limit_bytes=VMEM_LIMIT),
        name="attn",
    )(q, k, v)


def _memkv_kernel(mem_ref, g_ref, w_ref, o_ref):
    mn = (_rms(mem_ref[0]) * g_ref[...]).astype(BF16)
    o_ref[0] = _dot(mn, w_ref[...]).astype(o_ref.dtype)


def _memkv(mem, g, w):
    B, M, _ = mem.shape
    return pl.pallas_call(
        _memkv_kernel,
        grid=(B,),
        in_specs=[
            pl.BlockSpec((1, M, D_MODEL), lambda b: (b, 0, 0)),
            pl.BlockSpec(g.shape, lambda b: (0, 0)),
            pl.BlockSpec(w.shape, lambda b: (0, 0)),
        ],
        out_specs=pl.BlockSpec((1, M, 2 * D_MODEL), lambda b: (b, 0, 0)),
        out_shape=jax.ShapeDtypeStruct((B, M, 2 * D_MODEL), BF16),
        compiler_params=pltpu.CompilerParams(
            dimension_semantics=("arbitrary",), vmem_limit_bytes=VMEM_LIMIT),
        name="memkv",
    )(mem, g, w)


def _postmix_kernel(o_ref, x_ref, gmla_ref, gfox_ref, wo_ref, lnx_ref, wxq_ref, mkv_ref, wxo_ref,
                    lnf_ref, wr_hi_ref, wr_lo_ref, br_ref, u_ref,
                    h2_out, hn_out, idx_out, rank_out, gate_out, cnt_out):
    tm = TM_POST
    half = D_MODEL // 2
    o = o_ref[0].astype(F32)
    on = jnp.concatenate([_rms(o[:, :half]) * gmla_ref[...], _rms(o[:, half:]) * gfox_ref[...]], axis=1)
    h1 = x_ref[0] + _dot(on.astype(BF16), wo_ref[...])

    hn2 = (_rms(h1) * lnx_ref[...]).astype(BF16)
    qx = _dot(hn2, wxq_ref[...]).astype(BF16)
    heads = []
    for h in range(X_HEADS):
        sl = slice(h * X_HEAD_DIM, (h + 1) * X_HEAD_DIM)
        kh = mkv_ref[0, :, sl]
        vh = mkv_ref[0, :, D_MODEL + h * X_HEAD_DIM:D_MODEL + (h + 1) * X_HEAD_DIM]
        s = _dot_nt(qx[:, sl], kh) * (X_HEAD_DIM ** -0.5)
        p = jnp.exp(s - jnp.max(s, axis=-1, keepdims=True))
        p = p / jnp.sum(p, axis=-1, keepdims=True)
        heads.append(_dot(p.astype(BF16), vh))
    ox = jnp.concatenate(heads, axis=1).astype(BF16)
    h2 = h1 + _dot(ox, wxo_ref[...])
    h2_out[0] = h2

    hn3 = _rms(h2) * lnf_ref[...]
    hn_out[0] = hn3
    a_hi = hn3.astype(BF16)
    a_lo = (hn3 - a_hi.astype(F32)).astype(BF16)
    logits = (_dot_nt(wr_hi_ref[...], a_hi) + _dot_nt(wr_hi_ref[...], a_lo)
              + _dot_nt(wr_lo_ref[...], a_hi)) + br_ref[:, 0:1]

    eid = lax.broadcasted_iota(jnp.int32, (N_EXPERTS, tm), 0).astype(F32)
    vals = logits
    top_v, top_i, hots = [], [], []
    for _ in range(TOP_K):
        mx = jnp.max(vals, axis=0, keepdims=True)
        sel = jnp.min(jnp.where(vals == mx, eid, float(N_EXPERTS)), axis=0, keepdims=True)
        hot = eid == sel
        vals = jnp.where(hot, -jnp.inf, vals)
        top_v.append(mx)
        top_i.append(sel.astype(jnp.int32))
        hots.append(hot)
    ex = [jnp.exp(v - top_v[0]) for v in top_v]
    den = ex[0] + ex[1] + ex[2] + ex[3]
    gate_out[...] = jnp.concatenate([e / den for e in ex], axis=0)
    idx_out[...] = jnp.concatenate(top_i, axis=0)

    hot_all = jnp.where(hots[0] | hots[1] | hots[2] | hots[3], 1.0, 0.0)
    before = _dot(hot_all.astype(BF16), u_ref[...])
    ranks = [jnp.sum(jnp.where(hot, before, 0.0), axis=0, keepdims=True) for hot in hots]
    rank_out[...] = jnp.concatenate(ranks, axis=0).astype(jnp.int32)
    cnt_out[0] = jnp.broadcast_to(jnp.sum(hot_all, axis=1, keepdims=True), (N_EXPERTS, LANES))


def _postmix(o, x, gmla, gfox, wo, lnx, wxq, mkv, wxo, lnf, wr_hi, wr_lo, br, u):
    B, S, _ = x.shape
    tm = TM_POST
    nt = S // tm
    T = B * S
    const = lambda a: pl.BlockSpec(a.shape, lambda b, i: (0,) * a.ndim)
    tok = pl.BlockSpec((1, tm, D_MODEL), lambda b, i: (b, i, 0))
    route = pl.BlockSpec((TOP_K, tm), lambda b, i: (0, b * nt + i))
    return pl.pallas_call(
        _postmix_kernel,
        grid=(B, nt),
        in_specs=[
            tok, tok, const(gmla), const(gfox), const(wo), const(lnx), const(wxq),
            pl.BlockSpec((1, MEM_LEN, 2 * D_MODEL), lambda b, i: (b, 0, 0)),
            const(wxo), const(lnf), const(wr_hi), const(wr_lo), const(br), const(u),
        ],
        out_specs=[tok, tok, route, route, route,
                   pl.BlockSpec((1, N_EXPERTS, LANES), lambda b, i: (b * nt + i, 0, 0))],
        out_shape=[
            jax.ShapeDtypeStruct((B, S, D_MODEL), F32),
            jax.ShapeDtypeStruct((B, S, D_MODEL), F32),
            jax.ShapeDtypeStruct((TOP_K, T), jnp.int32),
            jax.ShapeDtypeStruct((TOP_K, T), jnp.int32),
            jax.ShapeDtypeStruct((TOP_K, T), F32),
            jax.ShapeDtypeStruct((B * nt, N_EXPERTS, LANES), F32),
        ],
        compiler_params=pltpu.CompilerParams(
            dimension_semantics=("arbitrary", "arbitrary"), vmem_limit_bytes=VMEM_LIMIT),
        name="postmix",
    )(o, x, gmla, gfox, wo, lnx, wxq, mkv, wxo, lnf, wr_hi, wr_lo, br, u)


def _for_each_chunk(c8_ref, lb_ref, gb_ref, tile, fn):
    for e in range(N_EXPERTS):
        c = c8_ref[tile * N_EXPERTS + e]
        l = lb_ref[tile * N_EXPERTS + e]
        g = gb_ref[tile * N_EXPERTS + e]
        for size in RUN_SIZES:
            @pl.when(jnp.bitwise_and(c, size) != 0)
            def _(size=size, c=c, l=l, g=g):
                done = jnp.bitwise_and(c, -2 * size)
                fn(pl.multiple_of(l + done, RUN_ALIGN), pl.multiple_of(g + done, RUN_ALIGN), size)


def _scatter_kernel(c8_ref, lb_ref, gb_ref, tail_ref, hn_ref, pos_ref, xs_hbm, stage, zeros, sem):
    i = pl.program_id(0)
    rows = lax.broadcasted_iota(jnp.int32, (STAGE_ROWS, TM_POST), 0)
    pm = jnp.zeros((STAGE_ROWS, TM_POST), F32)
    for k in range(TOP_K):
        pm = jnp.where(rows == pos_ref[k:k + 1, :], 1.0, pm)
    stage[...] = _dot(pm.astype(BF16), hn_ref[...].astype(BF16))

    def copy(l, g, size):
        return pltpu.make_async_copy(stage.at[pl.ds(l, size), :], xs_hbm.at[pl.ds(g, size), :], sem)

    _for_each_chunk(c8_ref, lb_ref, gb_ref, i, lambda l, g, size: copy(l, g, size).start())
    _for_each_chunk(c8_ref, lb_ref, gb_ref, i, lambda l, g, size: copy(l, g, size).wait())

    @pl.when(i == pl.num_programs(0) - 1)
    def _():
        zeros[...] = jnp.zeros(zeros.shape, F32)
        end = tail_ref[0]
        pad = tail_ref[1]

        def spare_tile(t, carry):
            cp = pltpu.make_async_copy(zeros, xs_hbm.at[pl.ds(pl.multiple_of(t * TM_EXP, TM_EXP), TM_EXP), :], sem)
            cp.start()
            cp.wait()
            return carry

        lax.fori_loop((end + pad) // TM_EXP, xs_hbm.shape[0] // TM_EXP, spare_tile, 0)

        def tail_copy(size):
            done = jnp.bitwise_and(pad, -2 * size)
            return pltpu.make_async_copy(zeros.at[pl.ds(0, size), :],
                                         xs_hbm.at[pl.ds(pl.multiple_of(end + done, RUN_ALIGN), size), :], sem)

        for size in RUN_SIZES[1:]:
            @pl.when(jnp.bitwise_and(pad, size) != 0)
            def _(size=size):
                tail_copy(size).start()
        for size in RUN_SIZES[1:]:
            @pl.when(jnp.bitwise_and(pad, size) != 0)
            def _(size=size):
                tail_copy(size).wait()


def _scatter(run_meta, tail, hn, pos, n_rows):
    T = hn.shape[0]
    grid_spec = pltpu.PrefetchScalarGridSpec(
        num_scalar_prefetch=4,
        grid=(T // TM_POST,),
        in_specs=[
            pl.BlockSpec((TM_POST, D_MODEL), lambda i, *_: (i, 0)),
            pl.BlockSpec((TOP_K, TM_POST), lambda i, *_: (0, i)),
        ],
        out_specs=pl.BlockSpec(memory_space=pl.ANY),
        scratch_shapes=[pltpu.VMEM((STAGE_ROWS, D_MODEL), F32), pltpu.VMEM((TM_EXP, D_MODEL), F32),
                        pltpu.SemaphoreType.DMA],
    )
    return pl.pallas_call(
        _scatter_kernel,
        grid_spec=grid_spec,
        out_shape=jax.ShapeDtypeStruct((n_rows, D_MODEL), F32),
        compiler_params=pltpu.CompilerParams(
            dimension_semantics=("arbitrary",), vmem_limit_bytes=VMEM_LIMIT),
        name="scatter",
    )(*run_meta, tail, hn, pos)


def _expert_kernel(tile_ref, exp_ref, lo_ref, hi_ref, first_ref, newexp_ref,
                   xs_ref, wgu_ref, bg_ref, bl_ref, wd_ref, bd_ref, perm_ref, y_ref, wg_s, wl_s, wd_s):
    w = pl.program_id(0)
    lo = lo_ref[w]
    hi = hi_ref[w]

    @pl.when(newexp_ref[w] == 1)
    def _():
        perm = perm_ref[...]
        for grp in range(2 * D_EXPERT // (2 * LANES)):
            blk = wgu_ref[0, :, grp * 2 * LANES:(grp + 1) * 2 * LANES].astype(BF16)
            res = _dot(blk, perm)
            wg_s[:, grp * LANES:(grp + 1) * LANES] = res[:, :LANES].astype(BF16)
            wl_s[:, grp * LANES:(grp + 1) * LANES] = res[:, LANES:].astype(BF16)
        wd_s[...] = wd_ref[0].astype(BF16)

    @pl.when(jnp.logical_and(hi <= lo, first_ref[w] == 1))
    def _():
        y_ref[...] = jnp.zeros(y_ref.shape, F32)

    @pl.when(hi > lo)
    def _():
        x = xs_ref[...].astype(BF16)
        g = _dot(x, wg_s[...]) + bg_ref[0]
        l = _dot(x, wl_s[...]) + bl_ref[0]
        glu = jnp.minimum(g, SWIGLU_LIMIT)
        lin = jnp.clip(l, -SWIGLU_LIMIT, SWIGLU_LIMIT)
        act = glu * (1.0 / (1.0 + jnp.exp(-SWIGLU_ALPHA * glu))) * (lin + 1.0)
        y = _dot(act.astype(BF16), wd_s[...]) + bd_ref[0]
        r = tile_ref[w] * TM_EXP + lax.broadcasted_iota(jnp.int32, (TM_EXP, 1), 0)
        mine = jnp.logical_and(r >= lo, r < hi)
        is_first = first_ref[w] == 1

        @pl.when(is_first)
        def _():
            y_ref[...] = jnp.where(mine, y, 0.0)

        @pl.when(jnp.logical_not(is_first))
        def _():
            y_ref[...] = jnp.where(mine, y, y_ref[...])


def _experts(meta, xs, wgu, bg, bl, wd, bd):
    R = xs.shape[0]
    n_items = meta[0].shape[0]
    by_tile = lambda w, tile, exp, lo, hi, first, newexp: (tile[w], 0)
    by_exp = lambda w, tile, exp, lo, hi, first, newexp: (exp[w], 0, 0)
    perm = np.zeros((2 * LANES, 2 * LANES), np.float32)
    perm[2 * np.arange(LANES), np.arange(LANES)] = 1.0
    perm[2 * np.arange(LANES) + 1, LANES + np.arange(LANES)] = 1.0
    perm = jnp.asarray(perm, BF16)
    grid_spec = pltpu.PrefetchScalarGridSpec(
        num_scalar_prefetch=6,
        grid=(n_items,),
        in_specs=[
            pl.BlockSpec((TM_EXP, D_MODEL), by_tile),
            pl.BlockSpec((1, D_MODEL, 2 * D_EXPERT), by_exp),
            pl.BlockSpec((1, 1, D_EXPERT), by_exp),
            pl.BlockSpec((1, 1, D_EXPERT), by_exp),
            pl.BlockSpec((1, D_EXPERT, D_MODEL), by_exp),
            pl.BlockSpec((1, 1, D_MODEL), by_exp),
            pl.BlockSpec(perm.shape, lambda w, *_: (0, 0)),
        ],
        out_specs=pl.BlockSpec((TM_EXP, D_MODEL), by_tile),
        scratch_shapes=[pltpu.VMEM((D_MODEL, D_EXPERT), BF16), pltpu.VMEM((D_MODEL, D_EXPERT), BF16),
                        pltpu.VMEM((D_EXPERT, D_MODEL), BF16)],
    )
    return pl.pallas_call(
        _expert_kernel,
        grid_spec=grid_spec,
        out_shape=jax.ShapeDtypeStruct((R, D_MODEL), F32),
        compiler_params=pltpu.CompilerParams(
            dimension_semantics=("arbitrary",), vmem_limit_bytes=VMEM_LIMIT),
        name="experts",
    )(*meta, xs, wgu, bg, bl, wd, bd, perm)


def _combine_kernel(c8_ref, lb_ref, gb_ref, h2_ref, gate_ref, pos_ref, lnf_ref, y_hbm, out_ref, stage, sem):
    i = pl.program_id(0)
    n = pl.num_programs(0)
    slot = lax.rem(i, 2)
    nxt = 1 - slot

    def copy(s, l, g, size):
        return pltpu.make_async_copy(y_hbm.at[pl.ds(g, size), :], stage.at[s, pl.ds(l, size), :], sem.at[s])

    @pl.when(i == 0)
    def _():
        stage[...] = jnp.zeros(stage.shape, F32)
        _for_each_chunk(c8_ref, lb_ref, gb_ref, 0, lambda l, g, size: copy(0, l, g, size).start())

    @pl.when(i + 1 < n)
    def _():
        _for_each_chunk(c8_ref, lb_ref, gb_ref, i + 1, lambda l, g, size: copy(nxt, l, g, size).start())

    _for_each_chunk(c8_ref, lb_ref, gb_ref, i, lambda l, g, size: copy(slot, l, g, size).wait())

    cols = lax.broadcasted_iota(jnp.int32, (TM_POST, STAGE_ROWS), 1)
    pos = pos_ref[...]
    gates = gate_ref[...]
    w = jnp.zeros((TM_POST, STAGE_ROWS), F32)
    for k in range(TOP_K):
        w = jnp.where(cols == pos[:, k:k + 1], gates[:, k:k + 1], w)
    moe = _dot(w.astype(BF16), stage[slot].astype(BF16))
    out_ref[...] = _rms(h2_ref[...] + moe) * lnf_ref[...]


def _combine(run_meta, h2, gates_t, pos_t, lnf, y):
    T = h2.shape[0]
    grid_spec = pltpu.PrefetchScalarGridSpec(
        num_scalar_prefetch=3,
        grid=(T // TM_POST,),
        in_specs=[
            pl.BlockSpec((TM_POST, D_MODEL), lambda i, *_: (i, 0)),
            pl.BlockSpec((TM_POST, TOP_K), lambda i, *_: (i, 0)),
            pl.BlockSpec((TM_POST, TOP_K), lambda i, *_: (i, 0)),
            pl.BlockSpec(lnf.shape, lambda i, *_: (0, 0)),
            pl.BlockSpec(memory_space=pl.ANY),
        ],
        out_specs=pl.BlockSpec((TM_POST, D_MODEL), lambda i, *_: (i, 0)),
        scratch_shapes=[pltpu.VMEM((2, STAGE_ROWS, D_MODEL), F32), pltpu.SemaphoreType.DMA((2,))],
    )
    return pl.pallas_call(
        _combine_kernel,
        grid_spec=grid_spec,
        out_shape=jax.ShapeDtypeStruct((T, D_MODEL), F32),
        compiler_params=pltpu.CompilerParams(
            dimension_semantics=("arbitrary",), vmem_limit_bytes=VMEM_LIMIT),
        name="combine",
    )(*run_meta, h2, gates_t, pos_t, lnf, y)


def _rope_tables(S):
    half = MLA_ROPE // 2
    inv = ROPE_THETA ** (-jnp.arange(half, dtype=F32) / half)
    ang = jnp.arange(S, dtype=F32)[:, None] * inv[None, :]
    cos, sin = jnp.cos(ang), jnp.sin(ang)
    z = lambda n: jnp.zeros((S, n), F32)
    pad = HEAD_LANES - MLA_NOPE - MLA_ROPE
    c_tab = jnp.concatenate([jnp.ones((S, MLA_NOPE), F32), cos, cos, z(pad)], axis=1)
    s_tab = jnp.concatenate([z(MLA_NOPE), sin, sin, z(pad)], axis=1)
    return c_tab, s_tab


def _pad_cols(w, left, width=HEAD_LANES):
    return jnp.pad(w, ((0, 0), (left, width - left - w.shape[1])))


def _pack_mixer_weights(w_in, w_q_up, w_kv_up):
    half = MLA_ROPE // 2
    cuts = np.cumsum([MLA_Q_RANK, MLA_KV_RANK, MLA_ROPE, 512, 512, 512])
    w_ql, w_kvl, w_kpe, w_fq, w_fk, w_fv, w_fl = jnp.split(w_in, cuts, axis=1)
    kpe_swap = jnp.concatenate([-w_kpe[:, half:], w_kpe[:, :half]], axis=1)
    win = jnp.concatenate([
        w_ql, w_kvl, _pad_cols(w_kpe, MLA_NOPE), _pad_cols(kpe_swap, MLA_NOPE),
        w_fq, w_fk, w_fv, _pad_cols(w_fl, 0)], axis=1).astype(BF16)

    dq = MLA_NOPE + MLA_ROPE
    wq = w_q_up.reshape(MLA_Q_RANK, MLA_HEADS, dq)
    zq = lambda n: jnp.zeros((MLA_Q_RANK, MLA_HEADS, n), F32)
    wqm = jnp.concatenate([wq, zq(HEAD_LANES - dq)], axis=2)
    wqs = jnp.concatenate([zq(MLA_NOPE), -wq[:, :, MLA_NOPE + half:], wq[:, :, MLA_NOPE:MLA_NOPE + half],
                           zq(HEAD_LANES - dq)], axis=2)
    wkv = w_kv_up.reshape(MLA_KV_RANK, MLA_HEADS, MLA_NOPE + MLA_V)
    zk = lambda n: jnp.zeros((MLA_KV_RANK, MLA_HEADS, n), F32)
    wkk = jnp.concatenate([wkv[:, :, :MLA_NOPE], zk(HEAD_LANES - MLA_NOPE)], axis=2)
    v_even = jnp.concatenate([wkv[:, :, MLA_NOPE:], zk(HEAD_LANES - MLA_V)], axis=2)
    v_odd = jnp.concatenate([zk(HEAD_LANES - MLA_V), wkv[:, :, MLA_NOPE:]], axis=2)
    odd = (jnp.arange(MLA_HEADS) % 2 == 1)[None, :, None]
    wkvv = jnp.where(odd, v_odd, v_even)
    flat = lambda w: w.reshape(w.shape[0], MLA_HEADS * HEAD_LANES).astype(BF16)
    return win, flat(wqm), flat(wqs), flat(wkk), flat(wkvv)


def _fox_placement():
    eq = np.zeros((3 * LANES, FOX_HEADS * HEAD_LANES), np.float32)
    ek = np.zeros((3 * LANES, FOX_HEADS * HEAD_LANES), np.float32)
    oq = np.zeros((1, FOX_HEADS * HEAD_LANES), np.float32)
    ok = np.zeros((1, FOX_HEADS * HEAD_LANES), np.float32)
    for h in range(FOX_HEADS):
        base = h * HEAD_LANES + (FOX_HEAD_DIM if h % 2 == 0 else 0)
        for part in range(3):
            eq[part * LANES + h, base + part] = 1.0
            ok[0, base + part] = 1.0
            oq[0, base + 3 + part] = 1.0
            ek[part * LANES + h, base + 3 + part] = -1.0
    return jnp.asarray(eq, BF16), jnp.asarray(ek, BF16), jnp.asarray(oq), jnp.asarray(ok)


def _work_items(counts, n_rows):
    n_tiles = n_rows // TM_EXP
    n_items = n_tiles + N_EXPERTS - 1
    offs = jnp.concatenate([jnp.zeros((1,), jnp.int32), jnp.cumsum(counts)]).astype(jnp.int32)
    start, end = offs[:-1], offs[1:]
    first_tile = start // TM_EXP
    last_tile = jnp.maximum(end - 1, 0) // TM_EXP
    per_exp = jnp.where(counts > 0, last_tile - first_tile + 1, 0)
    item_end = jnp.cumsum(per_exp)
    item_start = item_end - per_exp
    total = item_end[-1]
    w = jnp.arange(n_items, dtype=jnp.int32)
    experts = jnp.arange(N_EXPERTS, dtype=jnp.int32)
    e = jnp.minimum(jnp.sum(item_end[None, :] <= w[:, None], axis=1), N_EXPERTS - 1).astype(jnp.int32)
    active = w < total
    last_e = jnp.max(jnp.where(counts > 0, experts, 0))
    e = jnp.where(active, e, last_e)
    hot = e[:, None] == experts[None, :]
    pick = lambda table: jnp.sum(jnp.where(hot, table[None, :], 0), axis=1)
    used_tiles = (offs[-1] + TM_EXP - 1) // TM_EXP
    spare = jnp.minimum(used_tiles + (w - total), n_tiles - 1)
    tile = jnp.where(active, pick(first_tile) + (w - pick(item_start)), spare).astype(jnp.int32)
    lo = jnp.maximum(pick(start), tile * TM_EXP)
    hi = jnp.minimum(pick(end), (tile + 1) * TM_EXP)
    lo = jnp.where(active, lo, 0).astype(jnp.int32)
    hi = jnp.where(active, hi, 0).astype(jnp.int32)
    prev_tile = jnp.concatenate([jnp.full((1,), -1, jnp.int32), tile[:-1]])
    first = (tile != prev_tile).astype(jnp.int32)
    prev_e = jnp.concatenate([jnp.full((1,), -1, jnp.int32), e[:-1]])
    newexp = (e != prev_e).astype(jnp.int32)
    return (tile, e, lo, hi, first, newexp), offs


def kernel(x, mem, ln_mix, w_in, q_norm, w_q_up, kv_norm, w_kv_up, b_forget, g_out_mla, g_out_fox, w_o,
           ln_cross, mem_norm, w_xq, w_mem_kv, w_xo, ln_ffn, w_router, b_router, w_gate_up, b_gate_up,
           w_down, b_down, ln_final):
    B, S, _ = x.shape
    T = B * S
    row = lambda v: v.reshape(1, -1).astype(F32)

    win, wqm, wqs, wkk, wkvv = _pack_mixer_weights(w_in[0], w_q_up[0], w_kv_up[0])
    c_tab, s_tab = _rope_tables(S)
    q_scale = (MLA_NOPE + MLA_ROPE) ** -0.5 * LOG2E
    eq, ek, oq, ok = _fox_placement()
    ltri = jnp.asarray(np.tril(np.ones((TM_PRE, TM_PRE), np.float32)), BF16)
    bf = _pad_cols(row(b_forget[0]), 0)
    q, k, v = _premix(x, row(ln_mix[0]), win, row(q_norm[0]), wqm, wqs, row(kv_norm[0]), wkk, wkvv,
                      c_tab * q_scale, s_tab * q_scale, c_tab, s_tab, bf, ltri, eq, ek, oq, ok)
    o = _attention(q, k, v)

    mkv = _memkv(mem, row(mem_norm[0]), w_mem_kv[0].astype(BF16))
    wr = w_router[0].T
    wr_hi = wr.astype(BF16)
    wr_lo = (wr - wr_hi.astype(F32)).astype(BF16)
    br = jnp.broadcast_to(b_router[0].astype(F32)[:, None], (N_EXPERTS, LANES))
    u = jnp.asarray(np.triu(np.ones((TM_POST, TM_POST), np.float32), 1), BF16)
    h2, hn3, idx, rank, gates, tcnt = _postmix(
        o, x, row(g_out_mla[0]), row(g_out_fox[0]), w_o[0].astype(BF16), row(ln_cross[0]),
        w_xq[0].astype(BF16), mkv, w_xo[0].astype(BF16), row(ln_ffn[0]), wr_hi, wr_lo, br, u)

    counts = tcnt[:, :, 0].astype(jnp.int32)
    c8 = (counts + RUN_ALIGN - 1) // RUN_ALIGN * RUN_ALIGN
    lb = jnp.cumsum(c8, axis=1) - c8
    n_rows = -(-(T * TOP_K + (T // TM_POST) * N_EXPERTS * (RUN_ALIGN - 1)) // TM_EXP) * TM_EXP
    meta, offs = _work_items(jnp.sum(c8, axis=0), n_rows)
    gb = offs[:N_EXPERTS][None, :] + jnp.cumsum(c8, axis=0) - c8
    run_meta = (c8.reshape(-1), lb.reshape(-1), gb.reshape(-1))
    end = offs[N_EXPERTS]
    tail = jnp.stack([end, (-end) % TM_EXP]).astype(jnp.int32)
    lb_tok = jnp.repeat(lb, TM_POST, axis=0)
    pos = rank + jnp.sum(jnp.where(idx[..., None] == jnp.arange(N_EXPERTS, dtype=jnp.int32),
                                   lb_tok[None], 0), axis=-1)

    xs = _scatter(run_meta, tail, hn3.reshape(T, D_MODEL), pos, n_rows)
    bgu = b_gate_up[0].reshape(N_EXPERTS, 1, D_EXPERT, 2).astype(F32)
    y = _experts(meta, xs, w_gate_up[0], bgu[..., 0], bgu[..., 1],
                 w_down[0], b_down[0].reshape(N_EXPERTS, 1, D_MODEL).astype(F32))

    out = _combine(run_meta, h2.reshape(T, D_MODEL), gates.T, pos.T, row(ln_final), y)
    return out.reshape(B, S, D_MODEL)
```

```python
import functools

import jax
import jax.numpy as jnp
import numpy as np
from jax import lax
from jax.experimental import pallas as pl
from jax.experimental.pallas import tpu as pltpu

F32 = jnp.float32
BF16 = jnp.bfloat16

D_MODEL = 1024
MEM_LEN = 256
MLA_HEADS = 8
MLA_NOPE = 64
MLA_ROPE = 32
MLA_V = 64
MLA_Q_RANK = 256
MLA_KV_RANK = 128
FOX_HEADS = 8
FOX_HEAD_DIM = 64
X_HEADS = 4
X_HEAD_DIM = D_MODEL // X_HEADS
N_EXPERTS = 32
TOP_K = 4
D_EXPERT = D_MODEL
SWIGLU_LIMIT = 7.0
SWIGLU_ALPHA = 1.702
ROPE_THETA = 10000.0
EPS = 1e-6

N_HEADS = MLA_HEADS + FOX_HEADS
HEAD_LANES = 128
LANES = 128
NEG_BIG = -1e30
LOG2E = 1.4426950408889634
V_ONES_EVEN = 64
V_ONES_ODD = 0

_C_QLAT = 0
_C_KVLAT = _C_QLAT + MLA_Q_RANK
_C_KPE = _C_KVLAT + MLA_KV_RANK
_C_KPE_SWAP = _C_KPE + LANES
_C_FQ = _C_KPE_SWAP + LANES
_C_FK = _C_FQ + FOX_HEADS * FOX_HEAD_DIM
_C_FV = _C_FK + FOX_HEADS * FOX_HEAD_DIM
_C_FLOGIT = _C_FV + FOX_HEADS * FOX_HEAD_DIM
IN_COLS_PACKED = _C_FLOGIT + LANES

VMEM_LIMIT = 56 * 1024 * 1024

TM_PRE = 512
TQ = 1024
TK = 1024
TD = 512
TM_POST = 512
TM_EXP = 512
RUN_ALIGN = 8
RUN_SIZES = (512, 256, 128, 64, 32, 16, 8)
STAGE_ROWS = TOP_K * TM_POST + N_EXPERTS * RUN_ALIGN


def _rms(x, eps=EPS):
    return x * lax.rsqrt(jnp.mean(x * x, axis=-1, keepdims=True) + eps)


def _dot(a, b):
    return jnp.dot(a, b, preferred_element_type=F32)


def _dot_nt(a, b):
    return lax.dot_general(a, b, (((1,), (1,)), ((), ())), preferred_element_type=F32)


def _split3(x):
    hi = x.astype(BF16)
    r = x - hi.astype(F32)
    mid = r.astype(BF16)
    lo = (r - mid.astype(F32)).astype(BF16)
    return hi, mid, lo


def _premix_kernel(x_ref, g_ref, win_ref, qn_ref, wqm_ref, wqs_ref, kvn_ref, wkk_ref, wkv_ref,
                   cq_ref, sq_ref, ck_ref, sk_ref, bf_ref, ltri_ref, eq_ref, ek_ref, oq_ref, ok_ref,
                   q_out, k_out, v_out, carry_ref):
    i = pl.program_id(1)

    @pl.when(i == 0)
    def _():
        carry_ref[...] = jnp.zeros_like(carry_ref)

    x = x_ref[0]
    hn = (_rms(x) * g_ref[...]).astype(BF16)
    proj = _dot(hn, win_ref[...])

    qn = (_rms(proj[:, _C_QLAT:_C_QLAT + MLA_Q_RANK]) * qn_ref[...]).astype(BF16)
    qm = _dot(qn, wqm_ref[...])
    qs = _dot(qn, wqs_ref[...])
    cq = cq_ref[...]
    sq = sq_ref[...]
    for h in range(MLA_HEADS):
        sl = slice(h * HEAD_LANES, (h + 1) * HEAD_LANES)
        q_out[0, h] = (qm[:, sl] * cq + qs[:, sl] * sq).astype(BF16)

    kvn = (_rms(proj[:, _C_KVLAT:_C_KVLAT + MLA_KV_RANK]) * kvn_ref[...]).astype(BF16)
    kk = _dot(kvn, wkk_ref[...])
    vv = _dot(kvn, wkv_ref[...])
    kr = (proj[:, _C_KPE:_C_KPE + LANES] * ck_ref[...]
          + proj[:, _C_KPE_SWAP:_C_KPE_SWAP + LANES] * sk_ref[...])
    lane = lax.broadcasted_iota(jnp.int32, (TM_PRE, LANES), 1)
    low = lane < FOX_HEAD_DIM
    v_ones = (jnp.where(lane == V_ONES_EVEN, 1.0, 0.0), jnp.where(lane == V_ONES_ODD, 1.0, 0.0))
    for h in range(MLA_HEADS):
        sl = slice(h * HEAD_LANES, (h + 1) * HEAD_LANES)
        k_out[0, h] = (kk[:, sl] + kr).astype(BF16)
        v_out[0, h] = (vv[:, sl] + v_ones[h % 2]).astype(BF16)

    z = proj[:, _C_FLOGIT:_C_FLOGIT + LANES] + bf_ref[...]
    logf = jnp.minimum(z, 0.0) - jnp.log(1.0 + jnp.exp(-jnp.abs(z)))
    hi, mid, lo = _split3(logf)
    ltri = ltri_ref[...]
    cs = _dot(ltri, hi) + _dot(ltri, mid) + _dot(ltri, lo)
    c = cs + carry_ref[0:1, :]
    carry_ref[...] = jnp.broadcast_to(c[TM_PRE - 1:TM_PRE, :], carry_ref.shape)
    chi, cmid, clo = _split3(c * LOG2E)
    cparts = jnp.concatenate([chi, cmid, clo], axis=1)
    aug_q = _dot(cparts, eq_ref[...]) + oq_ref[...]
    aug_k = _dot(cparts, ek_ref[...]) + ok_ref[...]
    scale = FOX_HEAD_DIM ** -0.5 * LOG2E
    for p in range(FOX_HEADS // 2):
        fq = proj[:, _C_FQ + p * LANES:_C_FQ + (p + 1) * LANES] * scale
        fk = proj[:, _C_FK + p * LANES:_C_FK + (p + 1) * LANES]
        fv = proj[:, _C_FV + p * LANES:_C_FV + (p + 1) * LANES]
        for par in range(2):
            h = 2 * p + par
            keep = low if par == 0 else jnp.logical_not(low)
            sl = slice(h * HEAD_LANES, (h + 1) * HEAD_LANES)
            q_out[0, MLA_HEADS + h] = (jnp.where(keep, fq, 0.0) + aug_q[:, sl]).astype(BF16)
            k_out[0, MLA_HEADS + h] = (jnp.where(keep, fk, 0.0) + aug_k[:, sl]).astype(BF16)
            v_out[0, MLA_HEADS + h] = jnp.where(keep, fv, v_ones[par]).astype(BF16)


def _premix(x, g, win, qn, wqm, wqs, kvn, wkk, wkv, cq, sq, ck, sk, bf, ltri, eq, ek, oq, ok):
    B, S, _ = x.shape
    tm = TM_PRE
    const = lambda shape: pl.BlockSpec(shape, lambda b, i: (0,) * len(shape))
    rows = lambda w: pl.BlockSpec((tm, w), lambda b, i: (i, 0))
    head_out = pl.BlockSpec((1, N_HEADS, tm, HEAD_LANES), lambda b, i: (b, 0, i, 0))
    out_sds = jax.ShapeDtypeStruct((B, N_HEADS, S, HEAD_LANES), BF16)
    return pl.pallas_call(
        _premix_kernel,
        grid=(B, S // tm),
        in_specs=[
            pl.BlockSpec((1, tm, D_MODEL), lambda b, i: (b, i, 0)),
            const(g.shape), const(win.shape), const(qn.shape), const(wqm.shape), const(wqs.shape),
            const(kvn.shape), const(wkk.shape), const(wkv.shape),
            rows(LANES), rows(LANES), rows(LANES), rows(LANES),
            const(bf.shape), const(ltri.shape), const(eq.shape), const(ek.shape),
            const(oq.shape), const(ok.shape),
        ],
        out_specs=[head_out, head_out, head_out],
        out_shape=[out_sds, out_sds, out_sds],
        scratch_shapes=[pltpu.VMEM((8, LANES), F32)],
        compiler_params=pltpu.CompilerParams(
            dimension_semantics=("arbitrary", "arbitrary"), vmem_limit_bytes=VMEM_LIMIT),
        name="premix",
    )(x, g, win, qn, wqm, wqs, kvn, wkk, wkv, cq, sq, ck, sk, bf, ltri, eq, ek, oq, ok)


def _attn_kernel(q_ref, k_ref, v_ref, o_ref, m_sc, acc_sc):
    i = pl.program_id(2)
    r_loc = lax.broadcasted_iota(jnp.int32, (TD, TD), 0)
    c_loc = lax.broadcasted_iota(jnp.int32, (TD, TD), 1)
    m_sc[...] = jnp.full(m_sc.shape, NEG_BIG, F32)
    acc_sc[...] = jnp.zeros(acc_sc.shape, F32)

    def block(hh, r0, nr, key_start, nk, mask):
        rows = slice(r0, r0 + nr)
        k = k_ref[0, hh, pl.ds(key_start, nk), :]
        v = v_ref[0, hh, pl.ds(key_start, nk), :]
        s = _dot_nt(q_ref[0, hh, rows, :], k)
        if mask is not None:
            s = jnp.where(mask, s, NEG_BIG)
        m_old = m_sc[hh, rows, :]
        m_new = jnp.maximum(m_old, jnp.max(s, axis=-1, keepdims=True))
        p = jnp.exp2(s - jnp.concatenate([m_new] * (nk // LANES), axis=1))
        alpha = jnp.exp2(m_old - m_new)
        acc_sc[hh, rows, :] = alpha * acc_sc[hh, rows, :] + _dot(p.astype(BF16), v)
        m_sc[hh, rows, :] = m_new

    def full_tile(j, carry):
        start = pl.multiple_of(j * TK, TK)
        for hh in range(2):
            block(hh, 0, TQ, start, TK, None)
        return carry

    lax.fori_loop(0, i * (TQ // TK), full_tile, 0)

    for kt in range(TQ // TD):
        start = pl.multiple_of(i * TQ + kt * TD, TD)
        for hh in range(2):
            for qt in range(kt, TQ // TD):
                block(hh, qt * TD, TD, start, TD, (c_loc <= r_loc) if qt == kt else None)

    lane = lax.broadcasted_iota(jnp.int32, (TQ, HEAD_LANES), 1)
    a0 = acc_sc[0]
    a1 = acc_sc[1]
    out = jnp.where(lane < V_ONES_EVEN, a0 / a0[:, V_ONES_EVEN:V_ONES_EVEN + 1],
                    a1 / a1[:, V_ONES_ODD:V_ONES_ODD + 1])
    o_ref[0] = out.astype(o_ref.dtype)


def _attention(q, k, v):
    B, H, S, _ = q.shape
    return pl.pallas_call(
        _attn_kernel,
        grid=(B, H // 2, S // TQ),
        in_specs=[
            pl.BlockSpec((1, 2, TQ, HEAD_LANES), lambda b, p, i: (b, p, i, 0)),
            pl.BlockSpec((1, 2, S, HEAD_LANES), lambda b, p, i: (b, p, 0, 0)),
            pl.BlockSpec((1, 2, S, HEAD_LANES), lambda b, p, i: (b, p, 0, 0)),
        ],
        out_specs=pl.BlockSpec((1, TQ, HEAD_LANES), lambda b, p, i: (b, i, p)),
        out_shape=jax.ShapeDtypeStruct((B, S, (H // 2) * HEAD_LANES), BF16),
        scratch_shapes=[pltpu.VMEM((2, TQ, LANES), F32), pltpu.VMEM((2, TQ, HEAD_LANES), F32)],
        compiler_params=pltpu.CompilerParams(
            dimension_semantics=("arbitrary", "arbitrary", "arbitrary"), vmem_limit_bytes=VMEM_LIMIT),
        name="attn",
    )(q, k, v)


def _memkv_kernel(mem_ref, g_ref, w_ref, o_ref):
    mn = (_rms(mem_ref[0]) * g_ref[...]).astype(BF16)
    o_ref[0] = _dot(mn, w_ref[...]).astype(o_ref.dtype)


def _memkv(mem, g, w):
    B, M, _ = mem.shape
    return pl.pallas_call(
        _memkv_kernel,
        grid=(B,),
        in_specs=[
            pl.BlockSpec((1, M, D_MODEL), lambda b: (b, 0, 0)),
            pl.BlockSpec(g.shape, lambda b: (0, 0)),
            pl.BlockSpec(w.shape, lambda b: (0, 0)),
        ],
        out_specs=pl.BlockSpec((1, M, 2 * D_MODEL), lambda b: (b, 0, 0)),
        out_shape=jax.ShapeDtypeStruct((B, M, 2 * D_MODEL), BF16),
        compiler_params=pltpu.CompilerParams(
            dimension_semantics=("arbitrary",), vmem_limit_bytes=VMEM_LIMIT),
        name="memkv",
    )(mem, g, w)


def _postmix_kernel(o_ref, x_ref, gmla_ref, gfox_ref, wo_ref, lnx_ref, wxq_ref, mkv_ref, wxo_ref,
                    lnf_ref, wr_hi_ref, wr_lo_ref, br_ref, u_ref,
                    h2_out, hn_out, idx_out, rank_out, gate_out, cnt_out):
    tm = TM_POST
    half = D_MODEL // 2
    o = o_ref[0].astype(F32)
    on = jnp.concatenate([_rms(o[:, :half]) * gmla_ref[...], _rms(o[:, half:]) * gfox_ref[...]], axis=1)
    h1 = x_ref[0] + _dot(on.astype(BF16), wo_ref[...])

    hn2 = (_rms(h1) * lnx_ref[...]).astype(BF16)
    qx = _dot(hn2, wxq_ref[...]).astype(BF16)
    heads = []
    for h in range(X_HEADS):
        sl = slice(h * X_HEAD_DIM, (h + 1) * X_HEAD_DIM)
        kh = mkv_ref[0, :, sl]
        vh = mkv_ref[0, :, D_MODEL + h * X_HEAD_DIM:D_MODEL + (h + 1) * X_HEAD_DIM]
        s = _dot_nt(qx[:, sl], kh) * (X_HEAD_DIM ** -0.5)
        p = jnp.exp(s - jnp.max(s, axis=-1, keepdims=True))
        p = p / jnp.sum(p, axis=-1, keepdims=True)
        heads.append(_dot(p.astype(BF16), vh))
    ox = jnp.concatenate(heads, axis=1).astype(BF16)
    h2 = h1 + _dot(ox, wxo_ref[...])
    h2_out[0] = h2

    hn3 = _rms(h2) * lnf_ref[...]
    hn_out[0] = hn3
    a_hi = hn3.astype(BF16)
    a_lo = (hn3 - a_hi.astype(F32)).astype(BF16)
    logits = (_dot_nt(wr_hi_ref[...], a_hi) + _dot_nt(wr_hi_ref[...], a_lo)
              + _dot_nt(wr_lo_ref[...], a_hi)) + br_ref[:, 0:1]

    eid = lax.broadcasted_iota(jnp.int32, (N_EXPERTS, tm), 0).astype(F32)
    vals = logits
    top_v, top_i, hots = [], [], []
    for _ in range(TOP_K):
        mx = jnp.max(vals, axis=0, keepdims=True)
        sel = jnp.min(jnp.where(vals == mx, eid, float(N_EXPERTS)), axis=0, keepdims=True)
        hot = eid == sel
        vals = jnp.where(hot, -jnp.inf, vals)
        top_v.append(mx)
        top_i.append(sel.astype(jnp.int32))
        hots.append(hot)
    ex = [jnp.exp(v - top_v[0]) for v in top_v]
    den = ex[0] + ex[1] + ex[2] + ex[3]
    gate_out[...] = jnp.concatenate([e / den for e in ex], axis=0)
    idx_out[...] = jnp.concatenate(top_i, axis=0)

    hot_all = jnp.where(hots[0] | hots[1] | hots[2] | hots[3], 1.0, 0.0)
    before = _dot(hot_all.astype(BF16), u_ref[...])
    ranks = [jnp.sum(jnp.where(hot, before, 0.0), axis=0, keepdims=True) for hot in hots]
    rank_out[...] = jnp.concatenate(ranks, axis=0).astype(jnp.int32)
    cnt_out[0] = jnp.broadcast_to(jnp.sum(hot_all, axis=1, keepdims=True), (N_EXPERTS, LANES))


def _postmix(o, x, gmla, gfox, wo, lnx, wxq, mkv, wxo, lnf, wr_hi, wr_lo, br, u):
    B, S, _ = x.shape
    tm = TM_POST
    nt = S // tm
    T = B * S
    const = lambda a: pl.BlockSpec(a.shape, lambda b, i: (0,) * a.ndim)
    tok = pl.BlockSpec((1, tm, D_MODEL), lambda b, i: (b, i, 0))
    route = pl.BlockSpec((TOP_K, tm), lambda b, i: (0, b * nt + i))
    return pl.pallas_call(
        _postmix_kernel,
        grid=(B, nt),
        in_specs=[
            tok, tok, const(gmla), const(gfox), const(wo), const(lnx), const(wxq),
            pl.BlockSpec((1, MEM_LEN, 2 * D_MODEL), lambda b, i: (b, 0, 0)),
            const(wxo), const(lnf), const(wr_hi), const(wr_lo), const(br), const(u),
        ],
        out_specs=[tok, tok, route, route, route,
                   pl.BlockSpec((1, N_EXPERTS, LANES), lambda b, i: (b * nt + i, 0, 0))],
        out_shape=[
            jax.ShapeDtypeStruct((B, S, D_MODEL), F32),
            jax.ShapeDtypeStruct((B, S, D_MODEL), F32),
            jax.ShapeDtypeStruct((TOP_K, T), jnp.int32),
            jax.ShapeDtypeStruct((TOP_K, T), jnp.int32),
            jax.ShapeDtypeStruct((TOP_K, T), F32),
            jax.ShapeDtypeStruct((B * nt, N_EXPERTS, LANES), F32),
        ],
        compiler_params=pltpu.CompilerParams(
            dimension_semantics=("arbitrary", "arbitrary"), vmem_limit_bytes=VMEM_LIMIT),
        name="postmix",
    )(o, x, gmla, gfox, wo, lnx, wxq, mkv, wxo, lnf, wr_hi, wr_lo, br, u)


def _for_each_chunk(c8_ref, lb_ref, gb_ref, tile, fn):
    for e in range(N_EXPERTS):
        c = c8_ref[tile * N_EXPERTS + e]
        l = lb_ref[tile * N_EXPERTS + e]
        g = gb_ref[tile * N_EXPERTS + e]
        for size in RUN_SIZES:
            @pl.when(jnp.bitwise_and(c, size) != 0)
            def _(size=size, c=c, l=l, g=g):
                done = jnp.bitwise_and(c, -2 * size)
                fn(pl.multiple_of(l + done, RUN_ALIGN), pl.multiple_of(g + done, RUN_ALIGN), size)


def _scatter_kernel(c8_ref, lb_ref, gb_ref, tail_ref, hn_ref, pos_ref, xs_hbm, stage, zeros, sems):
    i = pl.program_id(0)
    last = pl.num_programs(0) - 1
    slot = lax.rem(i, 2)
    rows = lax.broadcasted_iota(jnp.int32, (STAGE_ROWS, TM_POST), 0)
    pm = jnp.zeros((STAGE_ROWS, TM_POST), F32)
    for k in range(TOP_K):
        pm = jnp.where(rows == pos_ref[k:k + 1, :], 1.0, pm)
    stage[slot] = _dot(pm.astype(BF16), hn_ref[...].astype(BF16))

    def copy(s, l, g, size):
        return pltpu.make_async_copy(stage.at[s, pl.ds(l, size), :], xs_hbm.at[pl.ds(g, size), :], sems.at[s])

    _for_each_chunk(c8_ref, lb_ref, gb_ref, i, lambda l, g, size: copy(slot, l, g, size).start())

    @pl.when(i > 0)
    def _():
        _for_each_chunk(c8_ref, lb_ref, gb_ref, i - 1, lambda l, g, size: copy(1 - slot, l, g, size).wait())

    @pl.when(i == last)
    def _():
        _for_each_chunk(c8_ref, lb_ref, gb_ref, i, lambda l, g, size: copy(slot, l, g, size).wait())
        sem = sems.at[0]
        zeros[...] = jnp.zeros(zeros.shape, F32)

        def spare_tile(t, carry):
            cp = pltpu.make_async_copy(zeros, xs_hbm.at[pl.ds(pl.multiple_of(t * TM_EXP, TM_EXP), TM_EXP), :], sem)
            cp.start()
            cp.wait()
            return carry

        lax.fori_loop(tail_ref[2 * N_EXPERTS], xs_hbm.shape[0] // TM_EXP, spare_tile, 0)

        def group_tails(do):
            for e in range(N_EXPERTS):
                end = tail_ref[e]
                pad = tail_ref[N_EXPERTS + e]
                for size in RUN_SIZES[1:]:
                    @pl.when(jnp.bitwise_and(pad, size) != 0)
                    def _(size=size, end=end, pad=pad):
                        done = jnp.bitwise_and(pad, -2 * size)
                        do(pltpu.make_async_copy(
                            zeros.at[pl.ds(0, size), :],
                            xs_hbm.at[pl.ds(pl.multiple_of(end + done, RUN_ALIGN), size), :], sem))

        group_tails(lambda cp: cp.start())
        group_tails(lambda cp: cp.wait())


def _scatter(run_meta, tail, hn, pos, n_rows):
    T = hn.shape[0]
    grid_spec = pltpu.PrefetchScalarGridSpec(
        num_scalar_prefetch=4,
        grid=(T // TM_POST,),
        in_specs=[
            pl.BlockSpec((TM_POST, D_MODEL), lambda i, *_: (i, 0)),
            pl.BlockSpec((TOP_K, TM_POST), lambda i, *_: (0, i)),
        ],
        out_specs=pl.BlockSpec(memory_space=pl.ANY),
        scratch_shapes=[pltpu.VMEM((2, STAGE_ROWS, D_MODEL), F32), pltpu.VMEM((TM_EXP, D_MODEL), F32),
                        pltpu.SemaphoreType.DMA((2,))],
    )
    return pl.pallas_call(
        _scatter_kernel,
        grid_spec=grid_spec,
        out_shape=jax.ShapeDtypeStruct((n_rows, D_MODEL), F32),
        compiler_params=pltpu.CompilerParams(
            dimension_semantics=("arbitrary",), vmem_limit_bytes=VMEM_LIMIT),
        name="scatter",
    )(*run_meta, tail, hn, pos)


def _expert_kernel(exp_ref, active_ref, newexp_ref,
                   xs_ref, wgu_ref, bg_ref, bl_ref, wd_ref, bd_ref, perm_ref, y_ref, wg_s, wl_s, wd_s):
    w = pl.program_id(0)

    @pl.when(newexp_ref[w] == 1)
    def _():
        perm = perm_ref[...]
        for grp in range(2 * D_EXPERT // (2 * LANES)):
            blk = wgu_ref[0, :, grp * 2 * LANES:(grp + 1) * 2 * LANES].astype(BF16)
            res = _dot(blk, perm)
            wg_s[:, grp * LANES:(grp + 1) * LANES] = res[:, :LANES].astype(BF16)
            wl_s[:, grp * LANES:(grp + 1) * LANES] = res[:, LANES:].astype(BF16)
        wd_s[...] = wd_ref[0].astype(BF16)

    @pl.when(active_ref[w] == 0)
    def _():
        y_ref[...] = jnp.zeros(y_ref.shape, F32)

    @pl.when(active_ref[w] == 1)
    def _():
        x = xs_ref[...].astype(BF16)
        g = _dot(x, wg_s[...]) + bg_ref[0]
        l = _dot(x, wl_s[...]) + bl_ref[0]
        glu = jnp.minimum(g, SWIGLU_LIMIT)
        lin = jnp.clip(l, -SWIGLU_LIMIT, SWIGLU_LIMIT)
        act = glu * (1.0 / (1.0 + jnp.exp(-SWIGLU_ALPHA * glu))) * (lin + 1.0)
        y_ref[...] = _dot(act.astype(BF16), wd_s[...]) + bd_ref[0]


def _experts(meta, xs, wgu, bg, bl, wd, bd):
    R = xs.shape[0]
    n_items = R // TM_EXP
    by_tile = lambda w, exp, active, newexp: (w, 0)
    by_exp = lambda w, exp, active, newexp: (exp[w], 0, 0)
    perm = np.zeros((2 * LANES, 2 * LANES), np.float32)
    perm[2 * np.arange(LANES), np.arange(LANES)] = 1.0
    perm[2 * np.arange(LANES) + 1, LANES + np.arange(LANES)] = 1.0
    perm = jnp.asarray(perm, BF16)
    grid_spec = pltpu.PrefetchScalarGridSpec(
        num_scalar_prefetch=3,
        grid=(n_items,),
        in_specs=[
            pl.BlockSpec((TM_EXP, D_MODEL), by_tile),
            pl.BlockSpec((1, D_MODEL, 2 * D_EXPERT), by_exp),
            pl.BlockSpec((1, 1, D_EXPERT), by_exp),
            pl.BlockSpec((1, 1, D_EXPERT), by_exp),
            pl.BlockSpec((1, D_EXPERT, D_MODEL), by_exp),
            pl.BlockSpec((1, 1, D_MODEL), by_exp),
            pl.BlockSpec(perm.shape, lambda w, *_: (0, 0)),
        ],
        out_specs=pl.BlockSpec((TM_EXP, D_MODEL), by_tile),
        scratch_shapes=[pltpu.VMEM((D_MODEL, D_EXPERT), BF16), pltpu.VMEM((D_MODEL, D_EXPERT), BF16),
                        pltpu.VMEM((D_EXPERT, D_MODEL), BF16)],
    )
    return pl.pallas_call(
        _expert_kernel,
        grid_spec=grid_spec,
        out_shape=jax.ShapeDtypeStruct((R, D_MODEL), F32),
        compiler_params=pltpu.CompilerParams(
            dimension_semantics=("arbitrary",), vmem_limit_bytes=VMEM_LIMIT),
        name="experts",
    )(*meta, xs, wgu, bg, bl, wd, bd, perm)


def _combine_kernel(c8_ref, lb_ref, gb_ref, h2_ref, gate_ref, pos_ref, lnf_ref, y_hbm, out_ref, stage, sem):
    i = pl.program_id(0)
    n = pl.num_programs(0)
    slot = lax.rem(i, 2)
    nxt = 1 - slot

    def copy(s, l, g, size):
        return pltpu.make_async_copy(y_hbm.at[pl.ds(g, size), :], stage.at[s, pl.ds(l, size), :], sem.at[s])

    @pl.when(i == 0)
    def _():
        stage[...] = jnp.zeros(stage.shape, F32)
        _for_each_chunk(c8_ref, lb_ref, gb_ref, 0, lambda l, g, size: copy(0, l, g, size).start())

    @pl.when(i + 1 < n)
    def _():
        _for_each_chunk(c8_ref, lb_ref, gb_ref, i + 1, lambda l, g, size: copy(nxt, l, g, size).start())

    _for_each_chunk(c8_ref, lb_ref, gb_ref, i, lambda l, g, size: copy(slot, l, g, size).wait())

    cols = lax.broadcasted_iota(jnp.int32, (TM_POST, STAGE_ROWS), 1)
    pos = pos_ref[...]
    gates = gate_ref[...]
    w = jnp.zeros((TM_POST, STAGE_ROWS), F32)
    for k in range(TOP_K):
        w = jnp.where(cols == pos[:, k:k + 1], gates[:, k:k + 1], w)
    moe = _dot(w.astype(BF16), stage[slot].astype(BF16))
    out_ref[...] = _rms(h2_ref[...] + moe) * lnf_ref[...]


def _combine(run_meta, h2, gates_t, pos_t, lnf, y):
    T = h2.shape[0]
    grid_spec = pltpu.PrefetchScalarGridSpec(
        num_scalar_prefetch=3,
        grid=(T // TM_POST,),
        in_specs=[
            pl.BlockSpec((TM_POST, D_MODEL), lambda i, *_: (i, 0)),
            pl.BlockSpec((TM_POST, TOP_K), lambda i, *_: (i, 0)),
            pl.BlockSpec((TM_POST, TOP_K), lambda i, *_: (i, 0)),
            pl.BlockSpec(lnf.shape, lambda i, *_: (0, 0)),
            pl.BlockSpec(memory_space=pl.ANY),
        ],
        out_specs=pl.BlockSpec((TM_POST, D_MODEL), lambda i, *_: (i, 0)),
        scratch_shapes=[pltpu.VMEM((2, STAGE_ROWS, D_MODEL), F32), pltpu.SemaphoreType.DMA((2,))],
    )
    return pl.pallas_call(
        _combine_kernel,
        grid_spec=grid_spec,
        out_shape=jax.ShapeDtypeStruct((T, D_MODEL), F32),
        compiler_params=pltpu.CompilerParams(
            dimension_semantics=("arbitrary",), vmem_limit_bytes=VMEM_LIMIT),
        name="combine",
    )(*run_meta, h2, gates_t, pos_t, lnf, y)


def _rope_tables(S):
    half = MLA_ROPE // 2
    inv = ROPE_THETA ** (-jnp.arange(half, dtype=F32) / half)
    ang = jnp.arange(S, dtype=F32)[:, None] * inv[None, :]
    cos, sin = jnp.cos(ang), jnp.sin(ang)
    z = lambda n: jnp.zeros((S, n), F32)
    pad = HEAD_LANES - MLA_NOPE - MLA_ROPE
    c_tab = jnp.concatenate([jnp.ones((S, MLA_NOPE), F32), cos, cos, z(pad)], axis=1)
    s_tab = jnp.concatenate([z(MLA_NOPE), sin, sin, z(pad)], axis=1)
    return c_tab, s_tab


def _pad_cols(w, left, width=HEAD_LANES):
    return jnp.pad(w, ((0, 0), (left, width - left - w.shape[1])))


def _pack_mixer_weights(w_in, w_q_up, w_kv_up):
    half = MLA_ROPE // 2
    cuts = np.cumsum([MLA_Q_RANK, MLA_KV_RANK, MLA_ROPE, 512, 512, 512])
    w_ql, w_kvl, w_kpe, w_fq, w_fk, w_fv, w_fl = jnp.split(w_in, cuts, axis=1)
    kpe_swap = jnp.concatenate([-w_kpe[:, half:], w_kpe[:, :half]], axis=1)
    win = jnp.concatenate([
        w_ql, w_kvl, _pad_cols(w_kpe, MLA_NOPE), _pad_cols(kpe_swap, MLA_NOPE),
        w_fq, w_fk, w_fv, _pad_cols(w_fl, 0)], axis=1).astype(BF16)

    dq = MLA_NOPE + MLA_ROPE
    wq = w_q_up.reshape(MLA_Q_RANK, MLA_HEADS, dq)
    zq = lambda n: jnp.zeros((MLA_Q_RANK, MLA_HEADS, n), F32)
    wqm = jnp.concatenate([wq, zq(HEAD_LANES - dq)], axis=2)
    wqs = jnp.concatenate([zq(MLA_NOPE), -wq[:, :, MLA_NOPE + half:], wq[:, :, MLA_NOPE:MLA_NOPE + half],
                           zq(HEAD_LANES - dq)], axis=2)
    wkv = w_kv_up.reshape(MLA_KV_RANK, MLA_HEADS, MLA_NOPE + MLA_V)
    zk = lambda n: jnp.zeros((MLA_KV_RANK, MLA_HEADS, n), F32)
    wkk = jnp.concatenate([wkv[:, :, :MLA_NOPE], zk(HEAD_LANES - MLA_NOPE)], axis=2)
    v_even = jnp.concatenate([wkv[:, :, MLA_NOPE:], zk(HEAD_LANES - MLA_V)], axis=2)
    v_odd = jnp.concatenate([zk(HEAD_LANES - MLA_V), wkv[:, :, MLA_NOPE:]], axis=2)
    odd = (jnp.arange(MLA_HEADS) % 2 == 1)[None, :, None]
    wkvv = jnp.where(odd, v_odd, v_even)
    flat = lambda w: w.reshape(w.shape[0], MLA_HEADS * HEAD_LANES).astype(BF16)
    return win, flat(wqm), flat(wqs), flat(wkk), flat(wkvv)


def _fox_placement():
    eq = np.zeros((3 * LANES, FOX_HEADS * HEAD_LANES), np.float32)
    ek = np.zeros((3 * LANES, FOX_HEADS * HEAD_LANES), np.float32)
    oq = np.zeros((1, FOX_HEADS * HEAD_LANES), np.float32)
    ok = np.zeros((1, FOX_HEADS * HEAD_LANES), np.float32)
    for h in range(FOX_HEADS):
        base = h * HEAD_LANES + (FOX_HEAD_DIM if h % 2 == 0 else 0)
        for part in range(3):
            eq[part * LANES + h, base + part] = 1.0
            ok[0, base + part] = 1.0
            oq[0, base + 3 + part] = 1.0
            ek[part * LANES + h, base + 3 + part] = -1.0
    return jnp.asarray(eq, BF16), jnp.asarray(ek, BF16), jnp.asarray(oq), jnp.asarray(ok)


def _row_tiles(rows_per_expert, n_rows):
    n_tiles = n_rows // TM_EXP
    tiles_per = (rows_per_expert + TM_EXP - 1) // TM_EXP
    tile_end = jnp.cumsum(tiles_per)
    starts = ((tile_end - tiles_per) * TM_EXP).astype(jnp.int32)
    w = jnp.arange(n_tiles, dtype=jnp.int32)
    experts = jnp.arange(N_EXPERTS, dtype=jnp.int32)
    e = jnp.minimum(jnp.sum(tile_end[None, :] <= w[:, None], axis=1), N_EXPERTS - 1).astype(jnp.int32)
    active = w < tile_end[-1]
    last_e = jnp.max(jnp.where(rows_per_expert > 0, experts, 0))
    e = jnp.where(active, e, last_e)
    prev_e = jnp.concatenate([jnp.full((1,), -1, jnp.int32), e[:-1]])
    newexp = (e != prev_e).astype(jnp.int32)
    return (e, active.astype(jnp.int32), newexp), starts


def kernel(x, mem, ln_mix, w_in, q_norm, w_q_up, kv_norm, w_kv_up, b_forget, g_out_mla, g_out_fox, w_o,
           ln_cross, mem_norm, w_xq, w_mem_kv, w_xo, ln_ffn, w_router, b_router, w_gate_up, b_gate_up,
           w_down, b_down, ln_final):
    B, S, _ = x.shape
    T = B * S
    row = lambda v: v.reshape(1, -1).astype(F32)

    win, wqm, wqs, wkk, wkvv = _pack_mixer_weights(w_in[0], w_q_up[0], w_kv_up[0])
    c_tab, s_tab = _rope_tables(S)
    q_scale = (MLA_NOPE + MLA_ROPE) ** -0.5 * LOG2E
    eq, ek, oq, ok = _fox_placement()
    ltri = jnp.asarray(np.tril(np.ones((TM_PRE, TM_PRE), np.float32)), BF16)
    bf = _pad_cols(row(b_forget[0]), 0)
    q, k, v = _premix(x, row(ln_mix[0]), win, row(q_norm[0]), wqm, wqs, row(kv_norm[0]), wkk, wkvv,
                      c_tab * q_scale, s_tab * q_scale, c_tab, s_tab, bf, ltri, eq, ek, oq, ok)
    o = _attention(q, k, v)

    mkv = _memkv(mem, row(mem_norm[0]), w_mem_kv[0].astype(BF16))
    wr = w_router[0].T
    wr_hi = wr.astype(BF16)
    wr_lo = (wr - wr_hi.astype(F32)).astype(BF16)
    br = jnp.broadcast_to(b_router[0].astype(F32)[:, None], (N_EXPERTS, LANES))
    u = jnp.asarray(np.triu(np.ones((TM_POST, TM_POST), np.float32), 1), BF16)
    h2, hn3, idx, rank, gates, tcnt = _postmix(
        o, x, row(g_out_mla[0]), row(g_out_fox[0]), w_o[0].astype(BF16), row(ln_cross[0]),
        w_xq[0].astype(BF16), mkv, w_xo[0].astype(BF16), row(ln_ffn[0]), wr_hi, wr_lo, br, u)

    counts = tcnt[:, :, 0].astype(jnp.int32)
    c8 = (counts + RUN_ALIGN - 1) // RUN_ALIGN * RUN_ALIGN
    lb = jnp.cumsum(c8, axis=1) - c8
    worst_rows = T * TOP_K + (T // TM_POST) * N_EXPERTS * (RUN_ALIGN - 1) + N_EXPERTS * (TM_EXP - RUN_ALIGN)
    n_rows = -(-worst_rows // TM_EXP) * TM_EXP
    group_rows = jnp.sum(c8, axis=0)
    meta, starts = _row_tiles(group_rows, n_rows)
    gb = starts[None, :] + jnp.cumsum(c8, axis=0) - c8
    run_meta = (c8.reshape(-1), lb.reshape(-1), gb.reshape(-1))
    ends = starts + group_rows
    used_tiles = jnp.sum((group_rows + TM_EXP - 1) // TM_EXP)
    tail = jnp.concatenate([ends, (-ends) % TM_EXP, used_tiles[None]]).astype(jnp.int32)
    lb_tok = jnp.repeat(lb, TM_POST, axis=0)
    pos = rank + jnp.sum(jnp.where(idx[..., None] == jnp.arange(N_EXPERTS, dtype=jnp.int32),
                                   lb_tok[None], 0), axis=-1)

    xs = _scatter(run_meta, tail, hn3.reshape(T, D_MODEL), pos, n_rows)
    bgu = b_gate_up[0].reshape(N_EXPERTS, 1, D_EXPERT, 2).astype(F32)
    y = _experts(meta, xs, w_gate_up[0], bgu[..., 0], bgu[..., 1],
                 w_down[0], b_down[0].reshape(N_EXPERTS, 1, D_MODEL).astype(F32))

    out = _combine(run_meta, h2.reshape(T, D_MODEL), gates.T, pos.T, row(ln_final), y)
    return out.reshape(B, S, D_MODEL)
```

```python
import functools

import jax
import jax.numpy as jnp
import numpy as np
from jax import lax
from jax.experimental import pallas as pl
from jax.experimental.pallas import tpu as pltpu

F32 = jnp.float32
BF16 = jnp.bfloat16

D_MODEL = 1024
MEM_LEN = 256
MLA_HEADS = 8
MLA_NOPE = 64
MLA_ROPE = 32
MLA_V = 64
MLA_Q_RANK = 256
MLA_KV_RANK = 128
FOX_HEADS = 8
FOX_HEAD_DIM = 64
X_HEADS = 4
X_HEAD_DIM = D_MODEL // X_HEADS
N_EXPERTS = 32
TOP_K = 4
D_EXPERT = D_MODEL
SWIGLU_LIMIT = 7.0
SWIGLU_ALPHA = 1.702
ROPE_THETA = 10000.0
EPS = 1e-6

N_HEADS = MLA_HEADS + FOX_HEADS
HEAD_LANES = 128
LANES = 128
NEG_BIG = -1e30
LOG2E = 1.4426950408889634
V_ONES_EVEN = 64
V_ONES_ODD = 0

_C_QLAT = 0
_C_KVLAT = _C_QLAT + MLA_Q_RANK
_C_KPE = _C_KVLAT + MLA_KV_RANK
_C_KPE_SWAP = _C_KPE + LANES
_C_FQ = _C_KPE_SWAP + LANES
_C_FK = _C_FQ + FOX_HEADS * FOX_HEAD_DIM
_C_FV = _C_FK + FOX_HEADS * FOX_HEAD_DIM
_C_FLOGIT = _C_FV + FOX_HEADS * FOX_HEAD_DIM
IN_COLS_PACKED = _C_FLOGIT + LANES

VMEM_LIMIT = 56 * 1024 * 1024

TM_PRE = 512
TQ = 1024
TK = 1024
TD = 512
TM_POST = 512
TM_EXP = 512
RUN_ALIGN = 8
RUN_SIZES = (512, 256, 128, 64, 32, 16, 8)
RUN_SIZES_RARE, RUN_SIZES_COMMON = RUN_SIZES[:3], RUN_SIZES[3:]
STAGE_ROWS = TOP_K * TM_POST + N_EXPERTS * RUN_ALIGN


def _rms(x, eps=EPS):
    return x * lax.rsqrt(jnp.mean(x * x, axis=-1, keepdims=True) + eps)


def _dot(a, b):
    return jnp.dot(a, b, preferred_element_type=F32)


def _dot_nt(a, b):
    return lax.dot_general(a, b, (((1,), (1,)), ((), ())), preferred_element_type=F32)


def _split3(x):
    hi = x.astype(BF16)
    r = x - hi.astype(F32)
    mid = r.astype(BF16)
    lo = (r - mid.astype(F32)).astype(BF16)
    return hi, mid, lo


def _premix_kernel(x_ref, g_ref, win_ref, qn_ref, wqm_ref, wqs_ref, kvn_ref, wkk_ref, wkv_ref,
                   cq_ref, sq_ref, ck_ref, sk_ref, bf_ref, ltri_ref, eq_ref, ek_ref, oq_ref, ok_ref,
                   q_out, k_out, v_out, carry_ref):
    i = pl.program_id(1)

    @pl.when(i == 0)
    def _():
        carry_ref[...] = jnp.zeros_like(carry_ref)

    x = x_ref[0]
    hn = (_rms(x) * g_ref[...]).astype(BF16)
    proj = _dot(hn, win_ref[...])

    qn = (_rms(proj[:, _C_QLAT:_C_QLAT + MLA_Q_RANK]) * qn_ref[...]).astype(BF16)
    qm = _dot(qn, wqm_ref[...])
    qs = _dot(qn, wqs_ref[...])
    cq = cq_ref[...]
    sq = sq_ref[...]
    for h in range(MLA_HEADS):
        sl = slice(h * HEAD_LANES, (h + 1) * HEAD_LANES)
        q_out[0, h] = (qm[:, sl] * cq + qs[:, sl] * sq).astype(BF16)

    kvn = (_rms(proj[:, _C_KVLAT:_C_KVLAT + MLA_KV_RANK]) * kvn_ref[...]).astype(BF16)
    kk = _dot(kvn, wkk_ref[...])
    vv = _dot(kvn, wkv_ref[...])
    kr = (proj[:, _C_KPE:_C_KPE + LANES] * ck_ref[...]
          + proj[:, _C_KPE_SWAP:_C_KPE_SWAP + LANES] * sk_ref[...])
    lane = lax.broadcasted_iota(jnp.int32, (TM_PRE, LANES), 1)
    low = lane < FOX_HEAD_DIM
    v_ones = (jnp.where(lane == V_ONES_EVEN, 1.0, 0.0), jnp.where(lane == V_ONES_ODD, 1.0, 0.0))
    for h in range(MLA_HEADS):
        sl = slice(h * HEAD_LANES, (h + 1) * HEAD_LANES)
        k_out[0, h] = (kk[:, sl] + kr).astype(BF16)
        v_out[0, h] = (vv[:, sl] + v_ones[h % 2]).astype(BF16)

    z = proj[:, _C_FLOGIT:_C_FLOGIT + LANES] + bf_ref[...]
    logf = jnp.minimum(z, 0.0) - jnp.log(1.0 + jnp.exp(-jnp.abs(z)))
    heads_only = lambda part: jnp.where(lane < FOX_HEADS, part.astype(F32), 0.0)

    def pack3(x):
        hi, mid, lo = _split3(x)
        return (heads_only(hi) + pltpu.roll(heads_only(mid), FOX_HEADS, 1)
                + pltpu.roll(heads_only(lo), 2 * FOX_HEADS, 1)).astype(BF16)

    r = _dot(ltri_ref[...], pack3(logf))
    cs = r + pltpu.roll(r, LANES - FOX_HEADS, 1) + pltpu.roll(r, LANES - 2 * FOX_HEADS, 1)
    c = cs + carry_ref[0:1, :]
    carry_ref[...] = jnp.broadcast_to(c[TM_PRE - 1:TM_PRE, :], carry_ref.shape)
    cparts = pack3(c * LOG2E)
    aug_q = _dot(cparts, eq_ref[...]) + oq_ref[...]
    aug_k = _dot(cparts, ek_ref[...]) + ok_ref[...]
    scale = FOX_HEAD_DIM ** -0.5 * LOG2E
    for p in range(FOX_HEADS // 2):
        fq = proj[:, _C_FQ + p * LANES:_C_FQ + (p + 1) * LANES] * scale
        fk = proj[:, _C_FK + p * LANES:_C_FK + (p + 1) * LANES]
        fv = proj[:, _C_FV + p * LANES:_C_FV + (p + 1) * LANES]
        for par in range(2):
            h = 2 * p + par
            keep = low if par == 0 else jnp.logical_not(low)
            sl = slice(h * HEAD_LANES, (h + 1) * HEAD_LANES)
            q_out[0, MLA_HEADS + h] = (jnp.where(keep, fq, 0.0) + aug_q[:, sl]).astype(BF16)
            k_out[0, MLA_HEADS + h] = (jnp.where(keep, fk, 0.0) + aug_k[:, sl]).astype(BF16)
            v_out[0, MLA_HEADS + h] = jnp.where(keep, fv, v_ones[par]).astype(BF16)


def _premix(x, g, win, qn, wqm, wqs, kvn, wkk, wkv, cq, sq, ck, sk, bf, ltri, eq, ek, oq, ok):
    B, S, _ = x.shape
    tm = TM_PRE
    const = lambda shape: pl.BlockSpec(shape, lambda b, i: (0,) * len(shape))
    rows = lambda w: pl.BlockSpec((tm, w), lambda b, i: (i, 0))
    head_out = pl.BlockSpec((1, N_HEADS, tm, HEAD_LANES), lambda b, i: (b, 0, i, 0))
    out_sds = jax.ShapeDtypeStruct((B, N_HEADS, S, HEAD_LANES), BF16)
    return pl.pallas_call(
        _premix_kernel,
        grid=(B, S // tm),
        in_specs=[
            pl.BlockSpec((1, tm, D_MODEL), lambda b, i: (b, i, 0)),
            const(g.shape), const(win.shape), const(qn.shape), const(wqm.shape), const(wqs.shape),
            const(kvn.shape), const(wkk.shape), const(wkv.shape),
            rows(LANES), rows(LANES), rows(LANES), rows(LANES),
            const(bf.shape), const(ltri.shape), const(eq.shape), const(ek.shape),
            const(oq.shape), const(ok.shape),
        ],
        out_specs=[head_out, head_out, head_out],
        out_shape=[out_sds, out_sds, out_sds],
        scratch_shapes=[pltpu.VMEM((8, LANES), F32)],
        compiler_params=pltpu.CompilerParams(
            dimension_semantics=("arbitrary", "arbitrary"), vmem_limit_bytes=VMEM_LIMIT),
        name="premix",
    )(x, g, win, qn, wqm, wqs, kvn, wkk, wkv, cq, sq, ck, sk, bf, ltri, eq, ek, oq, ok)


def _attn_kernel(q_ref, k_ref, v_ref, o_ref, m_sc, acc_sc):
    i = pl.program_id(2)
    m_sc[...] = jnp.full(m_sc.shape, NEG_BIG, F32)
    acc_sc[...] = jnp.zeros(acc_sc.shape, F32)

    def block(hh, r0, nr, key_start, nk, mask):
        rows = slice(r0, r0 + nr)
        k = k_ref[0, hh, pl.ds(key_start, nk), :]
        v = v_ref[0, hh, pl.ds(key_start, nk), :]
        s = _dot_nt(q_ref[0, hh, rows, :], k)
        if mask is not None:
            s = jnp.where(mask, s, NEG_BIG)
        m_old = m_sc[hh, rows, :]
        m_new = jnp.maximum(m_old, jnp.max(s, axis=-1, keepdims=True))
        p = jnp.exp2(s - jnp.concatenate([m_new] * (nk // LANES), axis=1))
        alpha = jnp.exp2(m_old - m_new)
        acc_sc[hh, rows, :] = alpha * acc_sc[hh, rows, :] + _dot(p.astype(BF16), v)
        m_sc[hh, rows, :] = m_new

    def full_tile(j, carry):
        start = pl.multiple_of(j * TK, TK)
        for hh in range(2):
            block(hh, 0, TQ, start, TK, None)
        return carry

    lax.fori_loop(0, i * (TQ // TK), full_tile, 0)

    for kt in range(TQ // TD):
        start = pl.multiple_of(i * TQ + kt * TD, TD)
        nr = TQ - kt * TD
        causal = (lax.broadcasted_iota(jnp.int32, (nr, TD), 1) <= lax.broadcasted_iota(jnp.int32, (nr, TD), 0))
        for hh in range(2):
            block(hh, kt * TD, nr, start, TD, causal)

    lane = lax.broadcasted_iota(jnp.int32, (TQ, HEAD_LANES), 1)
    a0 = acc_sc[0]
    a1 = acc_sc[1]
    out = jnp.where(lane < V_ONES_EVEN, a0 / a0[:, V_ONES_EVEN:V_ONES_EVEN + 1],
                    a1 / a1[:, V_ONES_ODD:V_ONES_ODD + 1])
    o_ref[0] = out.astype(o_ref.dtype)


def _attention(q, k, v):
    B, H, S, _ = q.shape
    return pl.pallas_call(
        _attn_kernel,
        grid=(B, H // 2, S // TQ),
        in_specs=[
            pl.BlockSpec((1, 2, TQ, HEAD_LANES), lambda b, p, i: (b, p, i, 0)),
            pl.BlockSpec((1, 2, S, HEAD_LANES), lambda b, p, i: (b, p, 0, 0)),
            pl.BlockSpec((1, 2, S, HEAD_LANES), lambda b, p, i: (b, p, 0, 0)),
        ],
        out_specs=pl.BlockSpec((1, TQ, HEAD_LANES), lambda b, p, i: (b, i, p)),
        out_shape=jax.ShapeDtypeStruct((B, S, (H // 2) * HEAD_LANES), BF16),
        scratch_shapes=[pltpu.VMEM((2, TQ, LANES), F32), pltpu.VMEM((2, TQ, HEAD_LANES), F32)],
        compiler_params=pltpu.CompilerParams(
            dimension_semantics=("arbitrary", "arbitrary", "arbitrary"), vmem_limit_bytes=VMEM_LIMIT),
        name="attn",
    )(q, k, v)


def _memkv_kernel(mem_ref, g_ref, w_ref, o_ref):
    mn = (_rms(mem_ref[0]) * g_ref[...]).astype(BF16)
    o_ref[0] = _dot(mn, w_ref[...]).astype(o_ref.dtype)


def _memkv(mem, g, w):
    B, M, _ = mem.shape
    return pl.pallas_call(
        _memkv_kernel,
        grid=(B,),
        in_specs=[
            pl.BlockSpec((1, M, D_MODEL), lambda b: (b, 0, 0)),
            pl.BlockSpec(g.shape, lambda b: (0, 0)),
            pl.BlockSpec(w.shape, lambda b: (0, 0)),
        ],
        out_specs=pl.BlockSpec((1, M, 2 * D_MODEL), lambda b: (b, 0, 0)),
        out_shape=jax.ShapeDtypeStruct((B, M, 2 * D_MODEL), BF16),
        compiler_params=pltpu.CompilerParams(
            dimension_semantics=("arbitrary",), vmem_limit_bytes=VMEM_LIMIT),
        name="memkv",
    )(mem, g, w)


def _postmix_kernel(o_ref, x_ref, gmla_ref, gfox_ref, wo_ref, lnx_ref, wxq_ref, mkv_ref, wxo_ref,
                    lnf_ref, wr_hi_ref, wr_lo_ref, br_ref, u_ref,
                    h2_out, hn_out, idx_out, rank_out, gate_out, cnt_out):
    tm = TM_POST
    half = D_MODEL // 2
    o = o_ref[0].astype(F32)
    on = jnp.concatenate([_rms(o[:, :half]) * gmla_ref[...], _rms(o[:, half:]) * gfox_ref[...]], axis=1)
    h1 = x_ref[0] + _dot(on.astype(BF16), wo_ref[...])

    hn2 = (_rms(h1) * lnx_ref[...]).astype(BF16)
    qx = _dot(hn2, wxq_ref[...]).astype(BF16)
    heads = []
    for h in range(X_HEADS):
        sl = slice(h * X_HEAD_DIM, (h + 1) * X_HEAD_DIM)
        kh = mkv_ref[0, :, sl]
        vh = mkv_ref[0, :, D_MODEL + h * X_HEAD_DIM:D_MODEL + (h + 1) * X_HEAD_DIM]
        s = _dot_nt(qx[:, sl], kh) * (X_HEAD_DIM ** -0.5)
        p = jnp.exp(s - jnp.max(s, axis=-1, keepdims=True))
        p = p / jnp.sum(p, axis=-1, keepdims=True)
        heads.append(_dot(p.astype(BF16), vh))
    ox = jnp.concatenate(heads, axis=1).astype(BF16)
    h2 = h1 + _dot(ox, wxo_ref[...])
    h2_out[0] = h2

    hn3 = _rms(h2) * lnf_ref[...]
    hn_out[0] = hn3
    a_hi = hn3.astype(BF16)
    a_lo = (hn3 - a_hi.astype(F32)).astype(BF16)
    logits = (_dot_nt(wr_hi_ref[...], a_hi) + _dot_nt(wr_hi_ref[...], a_lo)
              + _dot_nt(wr_lo_ref[...], a_hi)) + br_ref[:, 0:1]

    eid = lax.broadcasted_iota(jnp.int32, (N_EXPERTS, tm), 0).astype(F32)
    vals = logits
    top_v, top_i, hots = [], [], []
    for _ in range(TOP_K):
        mx = jnp.max(vals, axis=0, keepdims=True)
        sel = jnp.min(jnp.where(vals == mx, eid, float(N_EXPERTS)), axis=0, keepdims=True)
        hot = eid == sel
        vals = jnp.where(hot, -jnp.inf, vals)
        top_v.append(mx)
        top_i.append(sel.astype(jnp.int32))
        hots.append(hot)
    ex = [jnp.exp(v - top_v[0]) for v in top_v]
    den = ex[0] + ex[1] + ex[2] + ex[3]
    gate_out[...] = jnp.concatenate([e / den for e in ex], axis=0)
    idx_out[...] = jnp.concatenate(top_i, axis=0)

    hot_all = jnp.where(hots[0] | hots[1] | hots[2] | hots[3], 1.0, 0.0)
    before = _dot(hot_all.astype(BF16), u_ref[...])
    ranks = [jnp.sum(jnp.where(hot, before, 0.0), axis=0, keepdims=True) for hot in hots]
    rank_out[...] = jnp.concatenate(ranks, axis=0).astype(jnp.int32)
    cnt_out[0] = jnp.broadcast_to(jnp.sum(hot_all, axis=1, keepdims=True), (N_EXPERTS, LANES))


def _postmix(o, x, gmla, gfox, wo, lnx, wxq, mkv, wxo, lnf, wr_hi, wr_lo, br, u):
    B, S, _ = x.shape
    tm = TM_POST
    nt = S // tm
    T = B * S
    const = lambda a: pl.BlockSpec(a.shape, lambda b, i: (0,) * a.ndim)
    tok = pl.BlockSpec((1, tm, D_MODEL), lambda b, i: (b, i, 0))
    route = pl.BlockSpec((TOP_K, tm), lambda b, i: (0, b * nt + i))
    return pl.pallas_call(
        _postmix_kernel,
        grid=(B, nt),
        in_specs=[
            tok, tok, const(gmla), const(gfox), const(wo), const(lnx), const(wxq),
            pl.BlockSpec((1, MEM_LEN, 2 * D_MODEL), lambda b, i: (b, 0, 0)),
            const(wxo), const(lnf), const(wr_hi), const(wr_lo), const(br), const(u),
        ],
        out_specs=[tok, tok, route, route, route,
                   pl.BlockSpec((1, N_EXPERTS, LANES), lambda b, i: (b * nt + i, 0, 0))],
        out_shape=[
            jax.ShapeDtypeStruct((B, S, D_MODEL), F32),
            jax.ShapeDtypeStruct((B, S, D_MODEL), F32),
            jax.ShapeDtypeStruct((TOP_K, T), jnp.int32),
            jax.ShapeDtypeStruct((TOP_K, T), jnp.int32),
            jax.ShapeDtypeStruct((TOP_K, T), F32),
            jax.ShapeDtypeStruct((B * nt, N_EXPERTS, LANES), F32),
        ],
        compiler_params=pltpu.CompilerParams(
            dimension_semantics=("arbitrary", "arbitrary"), vmem_limit_bytes=VMEM_LIMIT),
        name="postmix",
    )(o, x, gmla, gfox, wo, lnx, wxq, mkv, wxo, lnf, wr_hi, wr_lo, br, u)


def _for_each_chunk(c8_ref, lb_ref, gb_ref, tile, fn):
    for e in range(N_EXPERTS):
        c = c8_ref[tile * N_EXPERTS + e]
        l = lb_ref[tile * N_EXPERTS + e]
        g = gb_ref[tile * N_EXPERTS + e]
        def chunks(sizes, c=c, l=l, g=g):
            for size in sizes:
                @pl.when(jnp.bitwise_and(c, size) != 0)
                def _(size=size):
                    done = jnp.bitwise_and(c, -2 * size)
                    fn(pl.multiple_of(l + done, RUN_ALIGN), pl.multiple_of(g + done, RUN_ALIGN), size)

        pl.when(c >= RUN_SIZES_COMMON[0] * 2)(functools.partial(chunks, RUN_SIZES_RARE))
        chunks(RUN_SIZES_COMMON)


def _scatter_kernel(c8_ref, lb_ref, gb_ref, tail_ref, hn_ref, pos_ref, xs_hbm, stage, zeros, sems):
    i = pl.program_id(0)
    last = pl.num_programs(0) - 1
    slot = lax.rem(i, 2)
    rows = lax.broadcasted_iota(jnp.int32, (STAGE_ROWS, TM_POST), 0)
    pm = jnp.zeros((STAGE_ROWS, TM_POST), F32)
    for k in range(TOP_K):
        pm = jnp.where(rows == pos_ref[k:k + 1, :], 1.0, pm)
    stage[slot] = _dot(pm.astype(BF16), hn_ref[...].astype(BF16))

    def copy(s, l, g, size):
        return pltpu.make_async_copy(stage.at[s, pl.ds(l, size), :], xs_hbm.at[pl.ds(g, size), :], sems.at[s])

    _for_each_chunk(c8_ref, lb_ref, gb_ref, i, lambda l, g, size: copy(slot, l, g, size).start())

    @pl.when(i > 0)
    def _():
        _for_each_chunk(c8_ref, lb_ref, gb_ref, i - 1, lambda l, g, size: copy(1 - slot, l, g, size).wait())

    @pl.when(i == last)
    def _():
        _for_each_chunk(c8_ref, lb_ref, gb_ref, i, lambda l, g, size: copy(slot, l, g, size).wait())
        sem = sems.at[0]
        zeros[...] = jnp.zeros(zeros.shape, F32)

        def spare_tile(t, carry):
            cp = pltpu.make_async_copy(zeros, xs_hbm.at[pl.ds(pl.multiple_of(t * TM_EXP, TM_EXP), TM_EXP), :], sem)
            cp.start()
            cp.wait()
            return carry

        lax.fori_loop(tail_ref[2 * N_EXPERTS], xs_hbm.shape[0] // TM_EXP, spare_tile, 0)

        def group_tails(do):
            for e in range(N_EXPERTS):
                end = tail_ref[e]
                pad = tail_ref[N_EXPERTS + e]
                for size in RUN_SIZES[1:]:
                    @pl.when(jnp.bitwise_and(pad, size) != 0)
                    def _(size=size, end=end, pad=pad):
                        done = jnp.bitwise_and(pad, -2 * size)
                        do(pltpu.make_async_copy(
                            zeros.at[pl.ds(0, size), :],
                            xs_hbm.at[pl.ds(pl.multiple_of(end + done, RUN_ALIGN), size), :], sem))

        group_tails(lambda cp: cp.start())
        group_tails(lambda cp: cp.wait())


def _scatter(run_meta, tail, hn, pos, n_rows):
    T = hn.shape[0]
    grid_spec = pltpu.PrefetchScalarGridSpec(
        num_scalar_prefetch=4,
        grid=(T // TM_POST,),
        in_specs=[
            pl.BlockSpec((TM_POST, D_MODEL), lambda i, *_: (i, 0)),
            pl.BlockSpec((TOP_K, TM_POST), lambda i, *_: (0, i)),
        ],
        out_specs=pl.BlockSpec(memory_space=pl.ANY),
        scratch_shapes=[pltpu.VMEM((2, STAGE_ROWS, D_MODEL), F32), pltpu.VMEM((TM_EXP, D_MODEL), F32),
                        pltpu.SemaphoreType.DMA((2,))],
    )
    return pl.pallas_call(
        _scatter_kernel,
        grid_spec=grid_spec,
        out_shape=jax.ShapeDtypeStruct((n_rows, D_MODEL), F32),
        compiler_params=pltpu.CompilerParams(
            dimension_semantics=("arbitrary",), vmem_limit_bytes=VMEM_LIMIT),
        name="scatter",
    )(*run_meta, tail, hn, pos)


def _expert_kernel(exp_ref, active_ref, newexp_ref,
                   xs_ref, wgu_ref, bg_ref, bl_ref, wd_ref, bd_ref, perm_ref, y_ref, wg_s, wl_s, wd_s):
    w = pl.program_id(0)

    @pl.when(newexp_ref[w] == 1)
    def _():
        perm = perm_ref[...]
        for grp in range(2 * D_EXPERT // (2 * LANES)):
            blk = wgu_ref[0, :, grp * 2 * LANES:(grp + 1) * 2 * LANES].astype(BF16)
            res = _dot(blk, perm)
            wg_s[:, grp * LANES:(grp + 1) * LANES] = res[:, :LANES].astype(BF16)
            wl_s[:, grp * LANES:(grp + 1) * LANES] = res[:, LANES:].astype(BF16)
        wd_s[...] = wd_ref[0].astype(BF16)

    @pl.when(active_ref[w] == 0)
    def _():
        y_ref[...] = jnp.zeros(y_ref.shape, F32)

    @pl.when(active_ref[w] == 1)
    def _():
        x = xs_ref[...].astype(BF16)
        g = _dot(x, wg_s[...]) + bg_ref[0]
        l = _dot(x, wl_s[...]) + bl_ref[0]
        glu = jnp.minimum(g, SWIGLU_LIMIT)
        lin = jnp.clip(l, -SWIGLU_LIMIT, SWIGLU_LIMIT)
        act = glu * (1.0 / (1.0 + jnp.exp(-SWIGLU_ALPHA * glu))) * (lin + 1.0)
        y_ref[...] = _dot(act.astype(BF16), wd_s[...]) + bd_ref[0]


def _experts(meta, xs, wgu, bg, bl, wd, bd):
    R = xs.shape[0]
    n_items = R // TM_EXP
    by_tile = lambda w, exp, active, newexp: (w, 0)
    by_exp = lambda w, exp, active, newexp: (exp[w], 0, 0)
    perm = np.zeros((2 * LANES, 2 * LANES), np.float32)
    perm[2 * np.arange(LANES), np.arange(LANES)] = 1.0
    perm[2 * np.arange(LANES) + 1, LANES + np.arange(LANES)] = 1.0
    perm = jnp.asarray(perm, BF16)
    grid_spec = pltpu.PrefetchScalarGridSpec(
        num_scalar_prefetch=3,
        grid=(n_items,),
        in_specs=[
            pl.BlockSpec((TM_EXP, D_MODEL), by_tile),
            pl.BlockSpec((1, D_MODEL, 2 * D_EXPERT), by_exp),
            pl.BlockSpec((1, 1, D_EXPERT), by_exp),
            pl.BlockSpec((1, 1, D_EXPERT), by_exp),
            pl.BlockSpec((1, D_EXPERT, D_MODEL), by_exp),
            pl.BlockSpec((1, 1, D_MODEL), by_exp),
            pl.BlockSpec(perm.shape, lambda w, *_: (0, 0)),
        ],
        out_specs=pl.BlockSpec((TM_EXP, D_MODEL), by_tile),
        scratch_shapes=[pltpu.VMEM((D_MODEL, D_EXPERT), BF16), pltpu.VMEM((D_MODEL, D_EXPERT), BF16),
                        pltpu.VMEM((D_EXPERT, D_MODEL), BF16)],
    )
    return pl.pallas_call(
        _expert_kernel,
        grid_spec=grid_spec,
        out_shape=jax.ShapeDtypeStruct((R, D_MODEL), F32),
        compiler_params=pltpu.CompilerParams(
            dimension_semantics=("arbitrary",), vmem_limit_bytes=VMEM_LIMIT),
        name="experts",
    )(*meta, xs, wgu, bg, bl, wd, bd, perm)


def _combine_kernel(c8_ref, lb_ref, gb_ref, h2_ref, gate_ref, pos_ref, lnf_ref, y_hbm, out_ref, stage, sem):
    i = pl.program_id(0)
    n = pl.num_programs(0)
    slot = lax.rem(i, 2)
    nxt = 1 - slot

    def copy(s, l, g, size):
        return pltpu.make_async_copy(y_hbm.at[pl.ds(g, size), :], stage.at[s, pl.ds(l, size), :], sem.at[s])

    @pl.when(i == 0)
    def _():
        stage[...] = jnp.zeros(stage.shape, F32)
        _for_each_chunk(c8_ref, lb_ref, gb_ref, 0, lambda l, g, size: copy(0, l, g, size).start())

    @pl.when(i + 1 < n)
    def _():
        _for_each_chunk(c8_ref, lb_ref, gb_ref, i + 1, lambda l, g, size: copy(nxt, l, g, size).start())

    _for_each_chunk(c8_ref, lb_ref, gb_ref, i, lambda l, g, size: copy(slot, l, g, size).wait())

    cols = lax.broadcasted_iota(jnp.int32, (TM_POST, STAGE_ROWS), 1)
    pos = pos_ref[...]
    gates = gate_ref[...]
    w = jnp.zeros((TM_POST, STAGE_ROWS), F32)
    for k in range(TOP_K):
        w = jnp.where(cols == pos[:, k:k + 1], gates[:, k:k + 1], w)
    moe = _dot(w.astype(BF16), stage[slot].astype(BF16))
    out_ref[...] = _rms(h2_ref[...] + moe) * lnf_ref[...]


def _combine(run_meta, h2, gates_t, pos_t, lnf, y):
    T = h2.shape[0]
    grid_spec = pltpu.PrefetchScalarGridSpec(
        num_scalar_prefetch=3,
        grid=(T // TM_POST,),
        in_specs=[
            pl.BlockSpec((TM_POST, D_MODEL), lambda i, *_: (i, 0)),
            pl.BlockSpec((TM_POST, TOP_K), lambda i, *_: (i, 0)),
            pl.BlockSpec((TM_POST, TOP_K), lambda i, *_: (i, 0)),
            pl.BlockSpec(lnf.shape, lambda i, *_: (0, 0)),
            pl.BlockSpec(memory_space=pl.ANY),
        ],
        out_specs=pl.BlockSpec((TM_POST, D_MODEL), lambda i, *_: (i, 0)),
        scratch_shapes=[pltpu.VMEM((2, STAGE_ROWS, D_MODEL), F32), pltpu.SemaphoreType.DMA((2,))],
    )
    return pl.pallas_call(
        _combine_kernel,
        grid_spec=grid_spec,
        out_shape=jax.ShapeDtypeStruct((T, D_MODEL), F32),
        compiler_params=pltpu.CompilerParams(
            dimension_semantics=("arbitrary",), vmem_limit_bytes=VMEM_LIMIT),
        name="combine",
    )(*run_meta, h2, gates_t, pos_t, lnf, y)


def _rope_tables(S):
    half = MLA_ROPE // 2
    inv = ROPE_THETA ** (-jnp.arange(half, dtype=F32) / half)
    ang = jnp.arange(S, dtype=F32)[:, None] * inv[None, :]
    cos, sin = jnp.cos(ang), jnp.sin(ang)
    z = lambda n: jnp.zeros((S, n), F32)
    pad = HEAD_LANES - MLA_NOPE - MLA_ROPE
    c_tab = jnp.concatenate([jnp.ones((S, MLA_NOPE), F32), cos, cos, z(pad)], axis=1)
    s_tab = jnp.concatenate([z(MLA_NOPE), sin, sin, z(pad)], axis=1)
    return c_tab, s_tab


def _pad_cols(w, left, width=HEAD_LANES):
    return jnp.pad(w, ((0, 0), (left, width - left - w.shape[1])))


def _pack_mixer_weights(w_in, w_q_up, w_kv_up):
    half = MLA_ROPE // 2
    cuts = np.cumsum([MLA_Q_RANK, MLA_KV_RANK, MLA_ROPE, 512, 512, 512])
    w_ql, w_kvl, w_kpe, w_fq, w_fk, w_fv, w_fl = jnp.split(w_in, cuts, axis=1)
    kpe_swap = jnp.concatenate([-w_kpe[:, half:], w_kpe[:, :half]], axis=1)
    win = jnp.concatenate([
        w_ql, w_kvl, _pad_cols(w_kpe, MLA_NOPE), _pad_cols(kpe_swap, MLA_NOPE),
        w_fq, w_fk, w_fv, _pad_cols(w_fl, 0)], axis=1).astype(BF16)

    dq = MLA_NOPE + MLA_ROPE
    wq = w_q_up.reshape(MLA_Q_RANK, MLA_HEADS, dq)
    zq = lambda n: jnp.zeros((MLA_Q_RANK, MLA_HEADS, n), F32)
    wqm = jnp.concatenate([wq, zq(HEAD_LANES - dq)], axis=2)
    wqs = jnp.concatenate([zq(MLA_NOPE), -wq[:, :, MLA_NOPE + half:], wq[:, :, MLA_NOPE:MLA_NOPE + half],
                           zq(HEAD_LANES - dq)], axis=2)
    wkv = w_kv_up.reshape(MLA_KV_RANK, MLA_HEADS, MLA_NOPE + MLA_V)
    zk = lambda n: jnp.zeros((MLA_KV_RANK, MLA_HEADS, n), F32)
    wkk = jnp.concatenate([wkv[:, :, :MLA_NOPE], zk(HEAD_LANES - MLA_NOPE)], axis=2)
    v_even = jnp.concatenate([wkv[:, :, MLA_NOPE:], zk(HEAD_LANES - MLA_V)], axis=2)
    v_odd = jnp.concatenate([zk(HEAD_LANES - MLA_V), wkv[:, :, MLA_NOPE:]], axis=2)
    odd = (jnp.arange(MLA_HEADS) % 2 == 1)[None, :, None]
    wkvv = jnp.where(odd, v_odd, v_even)
    flat = lambda w: w.reshape(w.shape[0], MLA_HEADS * HEAD_LANES).astype(BF16)
    return win, flat(wqm), flat(wqs), flat(wkk), flat(wkvv)


def _fox_placement():
    eq = np.zeros((LANES, FOX_HEADS * HEAD_LANES), np.float32)
    ek = np.zeros((LANES, FOX_HEADS * HEAD_LANES), np.float32)
    oq = np.zeros((1, FOX_HEADS * HEAD_LANES), np.float32)
    ok = np.zeros((1, FOX_HEADS * HEAD_LANES), np.float32)
    for h in range(FOX_HEADS):
        base = h * HEAD_LANES + (FOX_HEAD_DIM if h % 2 == 0 else 0)
        for part in range(3):
            eq[part * FOX_HEADS + h, base + part] = 1.0
            ok[0, base + part] = 1.0
            oq[0, base + 3 + part] = 1.0
            ek[part * FOX_HEADS + h, base + 3 + part] = -1.0
    return jnp.asarray(eq, BF16), jnp.asarray(ek, BF16), jnp.asarray(oq), jnp.asarray(ok)


def _row_tiles(rows_per_expert, n_rows):
    n_tiles = n_rows // TM_EXP
    tiles_per = (rows_per_expert + TM_EXP - 1) // TM_EXP
    tile_end = jnp.cumsum(tiles_per)
    starts = ((tile_end - tiles_per) * TM_EXP).astype(jnp.int32)
    w = jnp.arange(n_tiles, dtype=jnp.int32)
    experts = jnp.arange(N_EXPERTS, dtype=jnp.int32)
    e = jnp.minimum(jnp.sum(tile_end[None, :] <= w[:, None], axis=1), N_EXPERTS - 1).astype(jnp.int32)
    active = w < tile_end[-1]
    last_e = jnp.max(jnp.where(rows_per_expert > 0, experts, 0))
    e = jnp.where(active, e, last_e)
    prev_e = jnp.concatenate([jnp.full((1,), -1, jnp.int32), e[:-1]])
    newexp = (e != prev_e).astype(jnp.int32)
    return (e, active.astype(jnp.int32), newexp), starts


def kernel(x, mem, ln_mix, w_in, q_norm, w_q_up, kv_norm, w_kv_up, b_forget, g_out_mla, g_out_fox, w_o,
           ln_cross, mem_norm, w_xq, w_mem_kv, w_xo, ln_ffn, w_router, b_router, w_gate_up, b_gate_up,
           w_down, b_down, ln_final):
    B, S, _ = x.shape
    T = B * S
    row = lambda v: v.reshape(1, -1).astype(F32)

    win, wqm, wqs, wkk, wkvv = _pack_mixer_weights(w_in[0], w_q_up[0], w_kv_up[0])
    c_tab, s_tab = _rope_tables(S)
    q_scale = (MLA_NOPE + MLA_ROPE) ** -0.5 * LOG2E
    eq, ek, oq, ok = _fox_placement()
    ltri = jnp.asarray(np.tril(np.ones((TM_PRE, TM_PRE), np.float32)), BF16)
    bf = _pad_cols(row(b_forget[0]), 0)
    q, k, v = _premix(x, row(ln_mix[0]), win, row(q_norm[0]), wqm, wqs, row(kv_norm[0]), wkk, wkvv,
                      c_tab * q_scale, s_tab * q_scale, c_tab, s_tab, bf, ltri, eq, ek, oq, ok)
    o = _attention(q, k, v)

    mkv = _memkv(mem, row(mem_norm[0]), w_mem_kv[0].astype(BF16))
    wr = w_router[0].T
    wr_hi = wr.astype(BF16)
    wr_lo = (wr - wr_hi.astype(F32)).astype(BF16)
    br = jnp.broadcast_to(b_router[0].astype(F32)[:, None], (N_EXPERTS, LANES))
    u = jnp.asarray(np.triu(np.ones((TM_POST, TM_POST), np.float32), 1), BF16)
    h2, hn3, idx, rank, gates, tcnt = _postmix(
        o, x, row(g_out_mla[0]), row(g_out_fox[0]), w_o[0].astype(BF16), row(ln_cross[0]),
        w_xq[0].astype(BF16), mkv, w_xo[0].astype(BF16), row(ln_ffn[0]), wr_hi, wr_lo, br, u)

    counts = tcnt[:, :, 0].astype(jnp.int32)
    c8 = (counts + RUN_ALIGN - 1) // RUN_ALIGN * RUN_ALIGN
    lb = jnp.cumsum(c8, axis=1) - c8
    worst_rows = T * TOP_K + (T // TM_POST) * N_EXPERTS * (RUN_ALIGN - 1) + N_EXPERTS * (TM_EXP - RUN_ALIGN)
    n_rows = -(-worst_rows // TM_EXP) * TM_EXP
    group_rows = jnp.sum(c8, axis=0)
    meta, starts = _row_tiles(group_rows, n_rows)
    gb = starts[None, :] + jnp.cumsum(c8, axis=0) - c8
    run_meta = (c8.reshape(-1), lb.reshape(-1), gb.reshape(-1))
    ends = starts + group_rows
    used_tiles = jnp.sum((group_rows + TM_EXP - 1) // TM_EXP)
    tail = jnp.concatenate([ends, (-ends) % TM_EXP, used_tiles[None]]).astype(jnp.int32)
    lb_tok = jnp.repeat(lb, TM_POST, axis=0)
    pos = rank + jnp.sum(jnp.where(idx[..., None] == jnp.arange(N_EXPERTS, dtype=jnp.int32),
                                   lb_tok[None], 0), axis=-1)

    xs = _scatter(run_meta, tail, hn3.reshape(T, D_MODEL), pos, n_rows)
    bgu = b_gate_up[0].reshape(N_EXPERTS, 1, D_EXPERT, 2).astype(F32)
    y = _experts(meta, xs, w_gate_up[0], bgu[..., 0], bgu[..., 1],
                 w_down[0], b_down[0].reshape(N_EXPERTS, 1, D_MODEL).astype(F32))

    out = _combine(run_meta, h2.reshape(T, D_MODEL), gates.T, pos.T, row(ln_final), y)
    return out.reshape(B, S, D_MODEL)
```

```python
import functools

import jax
import jax.numpy as jnp
import numpy as np
from jax import lax
from jax.experimental import pallas as pl
from jax.experimental.pallas import tpu as pltpu

F32 = jnp.float32
BF16 = jnp.bfloat16

D_MODEL = 1024
MEM_LEN = 256
MLA_HEADS = 8
MLA_NOPE = 64
MLA_ROPE = 32
MLA_V = 64
MLA_Q_RANK = 256
MLA_KV_RANK = 128
FOX_HEADS = 8
FOX_HEAD_DIM = 64
X_HEADS = 4
X_HEAD_DIM = D_MODEL // X_HEADS
N_EXPERTS = 32
TOP_K = 4
D_EXPERT = D_MODEL
SWIGLU_LIMIT = 7.0
SWIGLU_ALPHA = 1.702
ROPE_THETA = 10000.0
EPS = 1e-6

N_HEADS = MLA_HEADS + FOX_HEADS
HEAD_LANES = 128
LANES = 128
NEG_BIG = -1e30
LOG2E = 1.4426950408889634
V_ONES_EVEN = 64
V_ONES_ODD = 0

_C_QLAT = 0
_C_KVLAT = _C_QLAT + MLA_Q_RANK
_C_KPE = _C_KVLAT + MLA_KV_RANK
_C_KPE_SWAP = _C_KPE + LANES
_C_FQ = _C_KPE_SWAP + LANES
_C_FK = _C_FQ + FOX_HEADS * FOX_HEAD_DIM
_C_FV = _C_FK + FOX_HEADS * FOX_HEAD_DIM
_C_FLOGIT = _C_FV + FOX_HEADS * FOX_HEAD_DIM
IN_COLS_PACKED = _C_FLOGIT + LANES

VMEM_LIMIT = 56 * 1024 * 1024

TM_PRE = 512
TQ = 1024
TK = 1024
TD = 512
TM_POST = 512
TM_EXP = 512
RUN_ALIGN = 8
RUN_SIZES = (512, 256, 128, 64, 32, 16, 8)
RUN_SIZES_RARE, RUN_SIZES_COMMON = RUN_SIZES[:3], RUN_SIZES[3:]
STAGE_ROWS = TOP_K * TM_POST + N_EXPERTS * RUN_ALIGN


def _rms(x, eps=EPS):
    return x * lax.rsqrt(jnp.mean(x * x, axis=-1, keepdims=True) + eps)


def _dot(a, b):
    return jnp.dot(a, b, preferred_element_type=F32)


def _dot_nt(a, b):
    return lax.dot_general(a, b, (((1,), (1,)), ((), ())), preferred_element_type=F32)


def _split3(x):
    hi = x.astype(BF16)
    r = x - hi.astype(F32)
    mid = r.astype(BF16)
    lo = (r - mid.astype(F32)).astype(BF16)
    return hi, mid, lo


def _premix_kernel(x_ref, g_ref, win_ref, qn_ref, wqm_ref, wqs_ref, kvn_ref, wkk_ref, wkv_ref,
                   cq_ref, sq_ref, ck_ref, sk_ref, bf_ref, ltri_ref, eq_ref, ek_ref, oq_ref, ok_ref,
                   q_out, k_out, v_out, carry_ref):
    i = pl.program_id(1)

    @pl.when(i == 0)
    def _():
        carry_ref[...] = jnp.zeros_like(carry_ref)

    x = x_ref[0]
    hn = (_rms(x) * g_ref[...]).astype(BF16)
    proj = _dot(hn, win_ref[...])

    qn = (_rms(proj[:, _C_QLAT:_C_QLAT + MLA_Q_RANK]) * qn_ref[...]).astype(BF16)
    qm = _dot(qn, wqm_ref[...])
    qs = _dot(qn, wqs_ref[...])
    cq = cq_ref[...]
    sq = sq_ref[...]
    for h in range(MLA_HEADS):
        sl = slice(h * HEAD_LANES, (h + 1) * HEAD_LANES)
        q_out[0, h] = (qm[:, sl] * cq + qs[:, sl] * sq).astype(BF16)

    kvn = (_rms(proj[:, _C_KVLAT:_C_KVLAT + MLA_KV_RANK]) * kvn_ref[...]).astype(BF16)
    kk = _dot(kvn, wkk_ref[...])
    vv = _dot(kvn, wkv_ref[...])
    kr = (proj[:, _C_KPE:_C_KPE + LANES] * ck_ref[...]
          + proj[:, _C_KPE_SWAP:_C_KPE_SWAP + LANES] * sk_ref[...])
    lane = lax.broadcasted_iota(jnp.int32, (TM_PRE, LANES), 1)
    low = lane < FOX_HEAD_DIM
    v_ones = (jnp.where(lane == V_ONES_EVEN, 1.0, 0.0), jnp.where(lane == V_ONES_ODD, 1.0, 0.0))
    for h in range(MLA_HEADS):
        sl = slice(h * HEAD_LANES, (h + 1) * HEAD_LANES)
        k_out[0, h] = (kk[:, sl] + kr).astype(BF16)
        v_out[0, h] = (vv[:, sl] + v_ones[h % 2]).astype(BF16)

    z = proj[:, _C_FLOGIT:_C_FLOGIT + LANES] + bf_ref[...]
    logf = jnp.minimum(z, 0.0) - jnp.log(1.0 + jnp.exp(-jnp.abs(z)))
    heads_only = lambda part: jnp.where(lane < FOX_HEADS, part.astype(F32), 0.0)

    def pack3(x):
        hi, mid, lo = _split3(x)
        return (heads_only(hi) + pltpu.roll(heads_only(mid), FOX_HEADS, 1)
                + pltpu.roll(heads_only(lo), 2 * FOX_HEADS, 1)).astype(BF16)

    r = _dot(ltri_ref[...], pack3(logf))
    cs = r + pltpu.roll(r, LANES - FOX_HEADS, 1) + pltpu.roll(r, LANES - 2 * FOX_HEADS, 1)
    c = cs + carry_ref[0:1, :]
    carry_ref[...] = jnp.broadcast_to(c[TM_PRE - 1:TM_PRE, :], carry_ref.shape)
    cparts = pack3(c * LOG2E)
    aug_q = _dot(cparts, eq_ref[...]) + oq_ref[...]
    aug_k = _dot(cparts, ek_ref[...]) + ok_ref[...]
    scale = FOX_HEAD_DIM ** -0.5 * LOG2E
    for p in range(FOX_HEADS // 2):
        fq = proj[:, _C_FQ + p * LANES:_C_FQ + (p + 1) * LANES] * scale
        fk = proj[:, _C_FK + p * LANES:_C_FK + (p + 1) * LANES]
        fv = proj[:, _C_FV + p * LANES:_C_FV + (p + 1) * LANES]
        for par in range(2):
            h = 2 * p + par
            keep = low if par == 0 else jnp.logical_not(low)
            sl = slice(h * HEAD_LANES, (h + 1) * HEAD_LANES)
            q_out[0, MLA_HEADS + h] = (jnp.where(keep, fq, 0.0) + aug_q[:, sl]).astype(BF16)
            k_out[0, MLA_HEADS + h] = (jnp.where(keep, fk, 0.0) + aug_k[:, sl]).astype(BF16)
            v_out[0, MLA_HEADS + h] = jnp.where(keep, fv, v_ones[par]).astype(BF16)


def _premix(x, g, win, qn, wqm, wqs, kvn, wkk, wkv, cq, sq, ck, sk, bf, ltri, eq, ek, oq, ok):
    B, S, _ = x.shape
    tm = TM_PRE
    const = lambda shape: pl.BlockSpec(shape, lambda b, i: (0,) * len(shape))
    rows = lambda w: pl.BlockSpec((tm, w), lambda b, i: (i, 0))
    head_out = pl.BlockSpec((1, N_HEADS, tm, HEAD_LANES), lambda b, i: (b, 0, i, 0))
    out_sds = jax.ShapeDtypeStruct((B, N_HEADS, S, HEAD_LANES), BF16)
    return pl.pallas_call(
        _premix_kernel,
        grid=(B, S // tm),
        in_specs=[
            pl.BlockSpec((1, tm, D_MODEL), lambda b, i: (b, i, 0)),
            const(g.shape), const(win.shape), const(qn.shape), const(wqm.shape), const(wqs.shape),
            const(kvn.shape), const(wkk.shape), const(wkv.shape),
            rows(LANES), rows(LANES), rows(LANES), rows(LANES),
            const(bf.shape), const(ltri.shape), const(eq.shape), const(ek.shape),
            const(oq.shape), const(ok.shape),
        ],
        out_specs=[head_out, head_out, head_out],
        out_shape=[out_sds, out_sds, out_sds],
        scratch_shapes=[pltpu.VMEM((8, LANES), F32)],
        compiler_params=pltpu.CompilerParams(
            dimension_semantics=("arbitrary", "arbitrary"), vmem_limit_bytes=VMEM_LIMIT),
        name="premix",
    )(x, g, win, qn, wqm, wqs, kvn, wkk, wkv, cq, sq, ck, sk, bf, ltri, eq, ek, oq, ok)


def _attn_kernel(q_ref, k_ref, v_ref, o_ref, m_sc, acc_sc):
    i = pl.program_id(2)
    m_sc[...] = jnp.full(m_sc.shape, NEG_BIG, F32)
    acc_sc[...] = jnp.zeros(acc_sc.shape, F32)

    def block(hh, r0, nr, key_start, nk, mask):
        rows = slice(r0, r0 + nr)
        k = k_ref[0, hh, pl.ds(key_start, nk), :]
        v = v_ref[0, hh, pl.ds(key_start, nk), :]
        s = _dot_nt(q_ref[0, hh, rows, :], k)
        if mask is not None:
            s = jnp.where(mask, s, NEG_BIG)
        m_old = m_sc[hh, rows, :]
        m_new = jnp.maximum(m_old, jnp.max(s, axis=-1, keepdims=True))
        p = jnp.exp2(s - jnp.concatenate([m_new] * (nk // LANES), axis=1))
        alpha = jnp.exp2(m_old - m_new)
        acc_sc[hh, rows, :] = alpha * acc_sc[hh, rows, :] + _dot(p.astype(BF16), v)
        m_sc[hh, rows, :] = m_new

    def full_tile(j, carry):
        start = pl.multiple_of(j * TK, TK)
        for hh in range(2):
            block(hh, 0, TQ, start, TK, None)
        return carry

    lax.fori_loop(0, i * (TQ // TK), full_tile, 0)

    for kt in range(TQ // TD):
        start = pl.multiple_of(i * TQ + kt * TD, TD)
        nr = TQ - kt * TD
        causal = (lax.broadcasted_iota(jnp.int32, (nr, TD), 1) <= lax.broadcasted_iota(jnp.int32, (nr, TD), 0))
        for hh in range(2):
            block(hh, kt * TD, nr, start, TD, causal)

    lane = lax.broadcasted_iota(jnp.int32, (TQ, HEAD_LANES), 1)
    a0 = acc_sc[0]
    a1 = acc_sc[1]
    out = jnp.where(lane < V_ONES_EVEN, a0 / a0[:, V_ONES_EVEN:V_ONES_EVEN + 1],
                    a1 / a1[:, V_ONES_ODD:V_ONES_ODD + 1])
    o_ref[0] = out.astype(o_ref.dtype)


def _attention(q, k, v):
    B, H, S, _ = q.shape
    return pl.pallas_call(
        _attn_kernel,
        grid=(B, H // 2, S // TQ),
        in_specs=[
            pl.BlockSpec((1, 2, TQ, HEAD_LANES), lambda b, p, i: (b, p, i, 0)),
            pl.BlockSpec((1, 2, S, HEAD_LANES), lambda b, p, i: (b, p, 0, 0)),
            pl.BlockSpec((1, 2, S, HEAD_LANES), lambda b, p, i: (b, p, 0, 0)),
        ],
        out_specs=pl.BlockSpec((1, TQ, HEAD_LANES), lambda b, p, i: (b, i, p)),
        out_shape=jax.ShapeDtypeStruct((B, S, (H // 2) * HEAD_LANES), BF16),
        scratch_shapes=[pltpu.VMEM((2, TQ, LANES), F32), pltpu.VMEM((2, TQ, HEAD_LANES), F32)],
        compiler_params=pltpu.CompilerParams(
            dimension_semantics=("arbitrary", "arbitrary", "arbitrary"), vmem_limit_bytes=VMEM_LIMIT),
        name="attn",
    )(q, k, v)


def _memkv_kernel(mem_ref, g_ref, w_ref, o_ref):
    mn = (_rms(mem_ref[0]) * g_ref[...]).astype(BF16)
    o_ref[0] = _dot(mn, w_ref[...]).astype(o_ref.dtype)


def _memkv(mem, g, w):
    B, M, _ = mem.shape
    return pl.pallas_call(
        _memkv_kernel,
        grid=(B,),
        in_specs=[
            pl.BlockSpec((1, M, D_MODEL), lambda b: (b, 0, 0)),
            pl.BlockSpec(g.shape, lambda b: (0, 0)),
            pl.BlockSpec(w.shape, lambda b: (0, 0)),
        ],
        out_specs=pl.BlockSpec((1, M, 2 * D_MODEL), lambda b: (b, 0, 0)),
        out_shape=jax.ShapeDtypeStruct((B, M, 2 * D_MODEL), BF16),
        compiler_params=pltpu.CompilerParams(
            dimension_semantics=("arbitrary",), vmem_limit_bytes=VMEM_LIMIT),
        name="memkv",
    )(mem, g, w)


def _postmix_kernel(o_ref, x_ref, gmla_ref, gfox_ref, wo_ref, lnx_ref, wxq_ref, mkv_ref, wxo_ref,
                    lnf_ref, wr_hi_ref, wr_lo_ref, br_ref, u_ref,
                    h2_out, hn_out, idx_out, rank_out, gate_out, cnt_out):
    tm = TM_POST
    half = D_MODEL // 2
    o = o_ref[0].astype(F32)
    on = jnp.concatenate([_rms(o[:, :half]) * gmla_ref[...], _rms(o[:, half:]) * gfox_ref[...]], axis=1)
    h1 = x_ref[0] + _dot(on.astype(BF16), wo_ref[...])

    hn2 = (_rms(h1) * lnx_ref[...]).astype(BF16)
    qx = _dot(hn2, wxq_ref[...]).astype(BF16)
    heads = []
    for h in range(X_HEADS):
        sl = slice(h * X_HEAD_DIM, (h + 1) * X_HEAD_DIM)
        kh = mkv_ref[0, :, sl]
        vh = mkv_ref[0, :, D_MODEL + h * X_HEAD_DIM:D_MODEL + (h + 1) * X_HEAD_DIM]
        s = _dot_nt(qx[:, sl], kh) * (X_HEAD_DIM ** -0.5)
        p = jnp.exp(s - jnp.max(s, axis=-1, keepdims=True))
        p = p / jnp.sum(p, axis=-1, keepdims=True)
        heads.append(_dot(p.astype(BF16), vh))
    ox = jnp.concatenate(heads, axis=1).astype(BF16)
    h2 = h1 + _dot(ox, wxo_ref[...])
    h2_out[0] = h2

    hn3 = _rms(h2) * lnf_ref[...]
    hn_out[0] = hn3
    a_hi = hn3.astype(BF16)
    a_lo = (hn3 - a_hi.astype(F32)).astype(BF16)
    logits = (_dot_nt(wr_hi_ref[...], a_hi) + _dot_nt(wr_hi_ref[...], a_lo)
              + _dot_nt(wr_lo_ref[...], a_hi)) + br_ref[:, 0:1]

    eid = lax.broadcasted_iota(jnp.int32, (N_EXPERTS, tm), 0).astype(F32)
    vals = logits
    top_v, top_i, hots = [], [], []
    for _ in range(TOP_K):
        mx = jnp.max(vals, axis=0, keepdims=True)
        sel = jnp.min(jnp.where(vals == mx, eid, float(N_EXPERTS)), axis=0, keepdims=True)
        hot = eid == sel
        vals = jnp.where(hot, -jnp.inf, vals)
        top_v.append(mx)
        top_i.append(sel.astype(jnp.int32))
        hots.append(hot)
    ex = [jnp.exp(v - top_v[0]) for v in top_v]
    den = ex[0] + ex[1] + ex[2] + ex[3]
    gate_out[...] = jnp.concatenate([e / den for e in ex], axis=0)
    idx_out[...] = jnp.concatenate(top_i, axis=0)

    hot_all = jnp.where(hots[0] | hots[1] | hots[2] | hots[3], 1.0, 0.0)
    before = _dot(hot_all.astype(BF16), u_ref[...])
    ranks = [jnp.sum(jnp.where(hot, before, 0.0), axis=0, keepdims=True) for hot in hots]
    rank_out[...] = jnp.concatenate(ranks, axis=0).astype(jnp.int32)
    cnt_out[0] = jnp.broadcast_to(jnp.sum(hot_all, axis=1, keepdims=True), (N_EXPERTS, LANES))


def _postmix(o, x, gmla, gfox, wo, lnx, wxq, mkv, wxo, lnf, wr_hi, wr_lo, br, u):
    B, S, _ = x.shape
    tm = TM_POST
    nt = S // tm
    T = B * S
    const = lambda a: pl.BlockSpec(a.shape, lambda b, i: (0,) * a.ndim)
    tok = pl.BlockSpec((1, tm, D_MODEL), lambda b, i: (b, i, 0))
    route = pl.BlockSpec((TOP_K, tm), lambda b, i: (0, b * nt + i))
    return pl.pallas_call(
        _postmix_kernel,
        grid=(B, nt),
        in_specs=[
            tok, tok, const(gmla), const(gfox), const(wo), const(lnx), const(wxq),
            pl.BlockSpec((1, MEM_LEN, 2 * D_MODEL), lambda b, i: (b, 0, 0)),
            const(wxo), const(lnf), const(wr_hi), const(wr_lo), const(br), const(u),
        ],
        out_specs=[tok, tok, route, route, route,
                   pl.BlockSpec((1, N_EXPERTS, LANES), lambda b, i: (b * nt + i, 0, 0))],
        out_shape=[
            jax.ShapeDtypeStruct((B, S, D_MODEL), F32),
            jax.ShapeDtypeStruct((B, S, D_MODEL), F32),
            jax.ShapeDtypeStruct((TOP_K, T), jnp.int32),
            jax.ShapeDtypeStruct((TOP_K, T), jnp.int32),
            jax.ShapeDtypeStruct((TOP_K, T), F32),
            jax.ShapeDtypeStruct((B * nt, N_EXPERTS, LANES), F32),
        ],
        compiler_params=pltpu.CompilerParams(
            dimension_semantics=("arbitrary", "arbitrary"), vmem_limit_bytes=VMEM_LIMIT),
        name="postmix",
    )(o, x, gmla, gfox, wo, lnx, wxq, mkv, wxo, lnf, wr_hi, wr_lo, br, u)


def _for_each_chunk(c8_ref, lb_ref, gb_ref, tile, fn):
    for e in range(N_EXPERTS):
        c = c8_ref[tile * N_EXPERTS + e]
        l = lb_ref[tile * N_EXPERTS + e]
        g = gb_ref[tile * N_EXPERTS + e]
        def chunks(sizes, c=c, l=l, g=g):
            for size in sizes:
                @pl.when(jnp.bitwise_and(c, size) != 0)
                def _(size=size):
                    done = jnp.bitwise_and(c, -2 * size)
                    fn(pl.multiple_of(l + done, RUN_ALIGN), pl.multiple_of(g + done, RUN_ALIGN), size)

        pl.when(c >= RUN_SIZES_COMMON[0] * 2)(functools.partial(chunks, RUN_SIZES_RARE))
        chunks(RUN_SIZES_COMMON)


def _scatter_kernel(c8_ref, lb_ref, gb_ref, tail_ref, hn_ref, pos_ref, xs_hbm, stage, zeros, sems):
    i = pl.program_id(0)
    last = pl.num_programs(0) - 1
    slot = lax.rem(i, 2)
    rows = lax.broadcasted_iota(jnp.int32, (STAGE_ROWS, TM_POST), 0)
    pm = jnp.zeros((STAGE_ROWS, TM_POST), F32)
    for k in range(TOP_K):
        pm = jnp.where(rows == pos_ref[k:k + 1, :], 1.0, pm)
    stage[slot] = _dot(pm.astype(BF16), hn_ref[...].astype(BF16))

    def copy(s, l, g, size):
        return pltpu.make_async_copy(stage.at[s, pl.ds(l, size), :], xs_hbm.at[pl.ds(g, size), :], sems.at[s])

    _for_each_chunk(c8_ref, lb_ref, gb_ref, i, lambda l, g, size: copy(slot, l, g, size).start())

    @pl.when(i > 0)
    def _():
        _for_each_chunk(c8_ref, lb_ref, gb_ref, i - 1, lambda l, g, size: copy(1 - slot, l, g, size).wait())

    @pl.when(i == last)
    def _():
        _for_each_chunk(c8_ref, lb_ref, gb_ref, i, lambda l, g, size: copy(slot, l, g, size).wait())
        sem = sems.at[0]
        zeros[...] = jnp.zeros(zeros.shape, F32)

        def spare_tile(t, carry):
            cp = pltpu.make_async_copy(zeros, xs_hbm.at[pl.ds(pl.multiple_of(t * TM_EXP, TM_EXP), TM_EXP), :], sem)
            cp.start()
            cp.wait()
            return carry

        lax.fori_loop(tail_ref[2 * N_EXPERTS], xs_hbm.shape[0] // TM_EXP, spare_tile, 0)

        def group_tails(do):
            for e in range(N_EXPERTS):
                end = tail_ref[e]
                pad = tail_ref[N_EXPERTS + e]
                for size in RUN_SIZES[1:]:
                    @pl.when(jnp.bitwise_and(pad, size) != 0)
                    def _(size=size, end=end, pad=pad):
                        done = jnp.bitwise_and(pad, -2 * size)
                        do(pltpu.make_async_copy(
                            zeros.at[pl.ds(0, size), :],
                            xs_hbm.at[pl.ds(pl.multiple_of(end + done, RUN_ALIGN), size), :], sem))

        group_tails(lambda cp: cp.start())
        group_tails(lambda cp: cp.wait())


def _scatter(run_meta, tail, hn, pos, n_rows):
    T = hn.shape[0]
    grid_spec = pltpu.PrefetchScalarGridSpec(
        num_scalar_prefetch=4,
        grid=(T // TM_POST,),
        in_specs=[
            pl.BlockSpec((TM_POST, D_MODEL), lambda i, *_: (i, 0)),
            pl.BlockSpec((TOP_K, TM_POST), lambda i, *_: (0, i)),
        ],
        out_specs=pl.BlockSpec(memory_space=pl.ANY),
        scratch_shapes=[pltpu.VMEM((2, STAGE_ROWS, D_MODEL), F32), pltpu.VMEM((TM_EXP, D_MODEL), F32),
                        pltpu.SemaphoreType.DMA((2,))],
    )
    return pl.pallas_call(
        _scatter_kernel,
        grid_spec=grid_spec,
        out_shape=jax.ShapeDtypeStruct((n_rows, D_MODEL), F32),
        compiler_params=pltpu.CompilerParams(
            dimension_semantics=("arbitrary",), vmem_limit_bytes=VMEM_LIMIT),
        name="scatter",
    )(*run_meta, tail, hn, pos)


def _expert_kernel(exp_ref, active_ref, newexp_ref, nextexp_ref, wslot_ref,
                   xs_ref, bg_ref, bl_ref, bd_ref, perm_ref, wgu_hbm, wd_hbm, y_ref,
                   wgu_buf, wd_buf, wg_s, wl_s, wd_s, wsem):
    w = pl.program_id(0)

    def fetch(e, s):
        return (pltpu.make_async_copy(wgu_hbm.at[e], wgu_buf.at[s], wsem.at[0, s]),
                pltpu.make_async_copy(wd_hbm.at[e], wd_buf.at[s], wsem.at[1, s]))

    @pl.when(newexp_ref[w] == 1)
    def _():
        s = wslot_ref[w]

        @pl.when(w == 0)
        def _():
            for cp in fetch(exp_ref[w], s):
                cp.start()

        for cp in fetch(exp_ref[w], s):
            cp.wait()

        @pl.when(nextexp_ref[w] >= 0)
        def _():
            for cp in fetch(nextexp_ref[w], 1 - s):
                cp.start()

        perm = perm_ref[...]
        for grp in range(2 * D_EXPERT // (2 * LANES)):
            blk = wgu_buf[s, :, grp * 2 * LANES:(grp + 1) * 2 * LANES].astype(BF16)
            res = _dot(blk, perm)
            wg_s[:, grp * LANES:(grp + 1) * LANES] = res[:, :LANES].astype(BF16)
            wl_s[:, grp * LANES:(grp + 1) * LANES] = res[:, LANES:].astype(BF16)
        wd_s[...] = wd_buf[s].astype(BF16)

    @pl.when(active_ref[w] == 0)
    def _():
        y_ref[...] = jnp.zeros(y_ref.shape, F32)

    @pl.when(active_ref[w] == 1)
    def _():
        x = xs_ref[...].astype(BF16)
        g = _dot(x, wg_s[...]) + bg_ref[0]
        l = _dot(x, wl_s[...]) + bl_ref[0]
        glu = jnp.minimum(g, SWIGLU_LIMIT)
        lin = jnp.clip(l, -SWIGLU_LIMIT, SWIGLU_LIMIT)
        act = glu * (1.0 / (1.0 + jnp.exp(-SWIGLU_ALPHA * glu))) * (lin + 1.0)
        y_ref[...] = _dot(act.astype(BF16), wd_s[...]) + bd_ref[0]


def _experts(meta, xs, wgu, bg, bl, wd, bd):
    R = xs.shape[0]
    n_items = R // TM_EXP
    by_tile = lambda w, exp, *_: (w, 0)
    by_exp = lambda w, exp, *_: (exp[w], 0, 0)
    perm = np.zeros((2 * LANES, 2 * LANES), np.float32)
    perm[2 * np.arange(LANES), np.arange(LANES)] = 1.0
    perm[2 * np.arange(LANES) + 1, LANES + np.arange(LANES)] = 1.0
    perm = jnp.asarray(perm, BF16)
    grid_spec = pltpu.PrefetchScalarGridSpec(
        num_scalar_prefetch=5,
        grid=(n_items,),
        in_specs=[
            pl.BlockSpec((TM_EXP, D_MODEL), by_tile),
            pl.BlockSpec((1, 1, D_EXPERT), by_exp),
            pl.BlockSpec((1, 1, D_EXPERT), by_exp),
            pl.BlockSpec((1, 1, D_MODEL), by_exp),
            pl.BlockSpec(perm.shape, lambda w, *_: (0, 0)),
            pl.BlockSpec(memory_space=pl.ANY),
            pl.BlockSpec(memory_space=pl.ANY),
        ],
        out_specs=pl.BlockSpec((TM_EXP, D_MODEL), by_tile),
        scratch_shapes=[pltpu.VMEM((2, D_MODEL, 2 * D_EXPERT), F32), pltpu.VMEM((2, D_EXPERT, D_MODEL), F32),
                        pltpu.VMEM((D_MODEL, D_EXPERT), BF16), pltpu.VMEM((D_MODEL, D_EXPERT), BF16),
                        pltpu.VMEM((D_EXPERT, D_MODEL), BF16), pltpu.SemaphoreType.DMA((2, 2))],
    )
    return pl.pallas_call(
        _expert_kernel,
        grid_spec=grid_spec,
        out_shape=jax.ShapeDtypeStruct((R, D_MODEL), F32),
        compiler_params=pltpu.CompilerParams(
            dimension_semantics=("arbitrary",), vmem_limit_bytes=VMEM_LIMIT),
        name="experts",
    )(*meta, xs, bg, bl, bd, perm, wgu, wd)


def _combine_kernel(c8_ref, lb_ref, gb_ref, h2_ref, gate_ref, pos_ref, lnf_ref, y_hbm, out_ref, stage, sem):
    i = pl.program_id(0)
    n = pl.num_programs(0)
    slot = lax.rem(i, 2)
    nxt = 1 - slot

    def copy(s, l, g, size):
        return pltpu.make_async_copy(y_hbm.at[pl.ds(g, size), :], stage.at[s, pl.ds(l, size), :], sem.at[s])

    @pl.when(i == 0)
    def _():
        stage[...] = jnp.zeros(stage.shape, F32)
        _for_each_chunk(c8_ref, lb_ref, gb_ref, 0, lambda l, g, size: copy(0, l, g, size).start())

    @pl.when(i + 1 < n)
    def _():
        _for_each_chunk(c8_ref, lb_ref, gb_ref, i + 1, lambda l, g, size: copy(nxt, l, g, size).start())

    _for_each_chunk(c8_ref, lb_ref, gb_ref, i, lambda l, g, size: copy(slot, l, g, size).wait())

    cols = lax.broadcasted_iota(jnp.int32, (TM_POST, STAGE_ROWS), 1)
    pos = pos_ref[...]
    gates = gate_ref[...]
    w = jnp.zeros((TM_POST, STAGE_ROWS), F32)
    for k in range(TOP_K):
        w = jnp.where(cols == pos[:, k:k + 1], gates[:, k:k + 1], w)
    moe = _dot(w.astype(BF16), stage[slot].astype(BF16))
    out_ref[...] = _rms(h2_ref[...] + moe) * lnf_ref[...]


def _combine(run_meta, h2, gates_t, pos_t, lnf, y):
    T = h2.shape[0]
    grid_spec = pltpu.PrefetchScalarGridSpec(
        num_scalar_prefetch=3,
        grid=(T // TM_POST,),
        in_specs=[
            pl.BlockSpec((TM_POST, D_MODEL), lambda i, *_: (i, 0)),
            pl.BlockSpec((TM_POST, TOP_K), lambda i, *_: (i, 0)),
            pl.BlockSpec((TM_POST, TOP_K), lambda i, *_: (i, 0)),
            pl.BlockSpec(lnf.shape, lambda i, *_: (0, 0)),
            pl.BlockSpec(memory_space=pl.ANY),
        ],
        out_specs=pl.BlockSpec((TM_POST, D_MODEL), lambda i, *_: (i, 0)),
        scratch_shapes=[pltpu.VMEM((2, STAGE_ROWS, D_MODEL), F32), pltpu.SemaphoreType.DMA((2,))],
    )
    return pl.pallas_call(
        _combine_kernel,
        grid_spec=grid_spec,
        out_shape=jax.ShapeDtypeStruct((T, D_MODEL), F32),
        compiler_params=pltpu.CompilerParams(
            dimension_semantics=("arbitrary",), vmem_limit_bytes=VMEM_LIMIT),
        name="combine",
    )(*run_meta, h2, gates_t, pos_t, lnf, y)


def _rope_tables(S):
    half = MLA_ROPE // 2
    inv = ROPE_THETA ** (-jnp.arange(half, dtype=F32) / half)
    ang = jnp.arange(S, dtype=F32)[:, None] * inv[None, :]
    cos, sin = jnp.cos(ang), jnp.sin(ang)
    z = lambda n: jnp.zeros((S, n), F32)
    pad = HEAD_LANES - MLA_NOPE - MLA_ROPE
    c_tab = jnp.concatenate([jnp.ones((S, MLA_NOPE), F32), cos, cos, z(pad)], axis=1)
    s_tab = jnp.concatenate([z(MLA_NOPE), sin, sin, z(pad)], axis=1)
    return c_tab, s_tab


def _pad_cols(w, left, width=HEAD_LANES):
    return jnp.pad(w, ((0, 0), (left, width - left - w.shape[1])))


def _pack_mixer_weights(w_in, w_q_up, w_kv_up):
    half = MLA_ROPE // 2
    cuts = np.cumsum([MLA_Q_RANK, MLA_KV_RANK, MLA_ROPE, 512, 512, 512])
    w_ql, w_kvl, w_kpe, w_fq, w_fk, w_fv, w_fl = jnp.split(w_in, cuts, axis=1)
    kpe_swap = jnp.concatenate([-w_kpe[:, half:], w_kpe[:, :half]], axis=1)
    win = jnp.concatenate([
        w_ql, w_kvl, _pad_cols(w_kpe, MLA_NOPE), _pad_cols(kpe_swap, MLA_NOPE),
        w_fq, w_fk, w_fv, _pad_cols(w_fl, 0)], axis=1).astype(BF16)

    dq = MLA_NOPE + MLA_ROPE
    wq = w_q_up.reshape(MLA_Q_RANK, MLA_HEADS, dq)
    zq = lambda n: jnp.zeros((MLA_Q_RANK, MLA_HEADS, n), F32)
    wqm = jnp.concatenate([wq, zq(HEAD_LANES - dq)], axis=2)
    wqs = jnp.concatenate([zq(MLA_NOPE), -wq[:, :, MLA_NOPE + half:], wq[:, :, MLA_NOPE:MLA_NOPE + half],
                           zq(HEAD_LANES - dq)], axis=2)
    wkv = w_kv_up.reshape(MLA_KV_RANK, MLA_HEADS, MLA_NOPE + MLA_V)
    zk = lambda n: jnp.zeros((MLA_KV_RANK, MLA_HEADS, n), F32)
    wkk = jnp.concatenate([wkv[:, :, :MLA_NOPE], zk(HEAD_LANES - MLA_NOPE)], axis=2)
    v_even = jnp.concatenate([wkv[:, :, MLA_NOPE:], zk(HEAD_LANES - MLA_V)], axis=2)
    v_odd = jnp.concatenate([zk(HEAD_LANES - MLA_V), wkv[:, :, MLA_NOPE:]], axis=2)
    odd = (jnp.arange(MLA_HEADS) % 2 == 1)[None, :, None]
    wkvv = jnp.where(odd, v_odd, v_even)
    flat = lambda w: w.reshape(w.shape[0], MLA_HEADS * HEAD_LANES).astype(BF16)
    return win, flat(wqm), flat(wqs), flat(wkk), flat(wkvv)


def _fox_placement():
    eq = np.zeros((LANES, FOX_HEADS * HEAD_LANES), np.float32)
    ek = np.zeros((LANES, FOX_HEADS * HEAD_LANES), np.float32)
    oq = np.zeros((1, FOX_HEADS * HEAD_LANES), np.float32)
    ok = np.zeros((1, FOX_HEADS * HEAD_LANES), np.float32)
    for h in range(FOX_HEADS):
        base = h * HEAD_LANES + (FOX_HEAD_DIM if h % 2 == 0 else 0)
        for part in range(3):
            eq[part * FOX_HEADS + h, base + part] = 1.0
            ok[0, base + part] = 1.0
            oq[0, base + 3 + part] = 1.0
            ek[part * FOX_HEADS + h, base + 3 + part] = -1.0
    return jnp.asarray(eq, BF16), jnp.asarray(ek, BF16), jnp.asarray(oq), jnp.asarray(ok)


def _row_tiles(rows_per_expert, n_rows):
    n_tiles = n_rows // TM_EXP
    tiles_per = (rows_per_expert + TM_EXP - 1) // TM_EXP
    tile_end = jnp.cumsum(tiles_per)
    starts = ((tile_end - tiles_per) * TM_EXP).astype(jnp.int32)
    w = jnp.arange(n_tiles, dtype=jnp.int32)
    experts = jnp.arange(N_EXPERTS, dtype=jnp.int32)
    e = jnp.minimum(jnp.sum(tile_end[None, :] <= w[:, None], axis=1), N_EXPERTS - 1).astype(jnp.int32)
    active = w < tile_end[-1]
    last_e = jnp.max(jnp.where(rows_per_expert > 0, experts, 0))
    e = jnp.where(active, e, last_e)
    prev_e = jnp.concatenate([jnp.full((1,), -1, jnp.int32), e[:-1]])
    newexp = (e != prev_e).astype(jnp.int32)
    has = tiles_per > 0
    later = jnp.logical_and(has[None, :], experts[None, :] > experts[:, None])
    next_of = jnp.min(jnp.where(later, experts[None, :], N_EXPERTS), axis=1)
    next_of = jnp.where(next_of < N_EXPERTS, next_of, -1)
    slot_of = (jnp.cumsum(has.astype(jnp.int32)) - 1) % 2
    hot = e[:, None] == experts[None, :]
    pick = lambda table: jnp.sum(jnp.where(hot, table[None, :], 0), axis=1).astype(jnp.int32)
    return (e, active.astype(jnp.int32), newexp, pick(next_of), pick(slot_of)), starts


def kernel(x, mem, ln_mix, w_in, q_norm, w_q_up, kv_norm, w_kv_up, b_forget, g_out_mla, g_out_fox, w_o,
           ln_cross, mem_norm, w_xq, w_mem_kv, w_xo, ln_ffn, w_router, b_router, w_gate_up, b_gate_up,
           w_down, b_down, ln_final):
    B, S, _ = x.shape
    T = B * S
    row = lambda v: v.reshape(1, -1).astype(F32)

    win, wqm, wqs, wkk, wkvv = _pack_mixer_weights(w_in[0], w_q_up[0], w_kv_up[0])
    c_tab, s_tab = _rope_tables(S)
    q_scale = (MLA_NOPE + MLA_ROPE) ** -0.5 * LOG2E
    eq, ek, oq, ok = _fox_placement()
    ltri = jnp.asarray(np.tril(np.ones((TM_PRE, TM_PRE), np.float32)), BF16)
    bf = _pad_cols(row(b_forget[0]), 0)
    q, k, v = _premix(x, row(ln_mix[0]), win, row(q_norm[0]), wqm, wqs, row(kv_norm[0]), wkk, wkvv,
                      c_tab * q_scale, s_tab * q_scale, c_tab, s_tab, bf, ltri, eq, ek, oq, ok)
    o = _attention(q, k, v)

    mkv = _memkv(mem, row(mem_norm[0]), w_mem_kv[0].astype(BF16))
    wr = w_router[0].T
    wr_hi = wr.astype(BF16)
    wr_lo = (wr - wr_hi.astype(F32)).astype(BF16)
    br = jnp.broadcast_to(b_router[0].astype(F32)[:, None], (N_EXPERTS, LANES))
    u = jnp.asarray(np.triu(np.ones((TM_POST, TM_POST), np.float32), 1), BF16)
    h2, hn3, idx, rank, gates, tcnt = _postmix(
        o, x, row(g_out_mla[0]), row(g_out_fox[0]), w_o[0].astype(BF16), row(ln_cross[0]),
        w_xq[0].astype(BF16), mkv, w_xo[0].astype(BF16), row(ln_ffn[0]), wr_hi, wr_lo, br, u)

    counts = tcnt[:, :, 0].astype(jnp.int32)
    c8 = (counts + RUN_ALIGN - 1) // RUN_ALIGN * RUN_ALIGN
    lb = jnp.cumsum(c8, axis=1) - c8
    worst_rows = T * TOP_K + (T // TM_POST) * N_EXPERTS * (RUN_ALIGN - 1) + N_EXPERTS * (TM_EXP - RUN_ALIGN)
    n_rows = -(-worst_rows // TM_EXP) * TM_EXP
    group_rows = jnp.sum(c8, axis=0)
    meta, starts = _row_tiles(group_rows, n_rows)
    gb = starts[None, :] + jnp.cumsum(c8, axis=0) - c8
    run_meta = (c8.reshape(-1), lb.reshape(-1), gb.reshape(-1))
    ends = starts + group_rows
    used_tiles = jnp.sum((group_rows + TM_EXP - 1) // TM_EXP)
    tail = jnp.concatenate([ends, (-ends) % TM_EXP, used_tiles[None]]).astype(jnp.int32)
    lb_tok = jnp.repeat(lb, TM_POST, axis=0)
    pos = rank + jnp.sum(jnp.where(idx[..., None] == jnp.arange(N_EXPERTS, dtype=jnp.int32),
                                   lb_tok[None], 0), axis=-1)

    xs = _scatter(run_meta, tail, hn3.reshape(T, D_MODEL), pos, n_rows)
    bgu = b_gate_up[0].reshape(N_EXPERTS, 1, D_EXPERT, 2).astype(F32)
    y = _experts(meta, xs, w_gate_up[0], bgu[..., 0], bgu[..., 1],
                 w_down[0], b_down[0].reshape(N_EXPERTS, 1, D_MODEL).astype(F32))

    out = _combine(run_meta, h2.reshape(T, D_MODEL), gates.T, pos.T, row(ln_final), y)
    return out.reshape(B, S, D_MODEL)
```

```python
import functools

import jax
import jax.numpy as jnp
import numpy as np
from jax import lax
from jax.experimental import pallas as pl
from jax.experimental.pallas import tpu as pltpu

F32 = jnp.float32
BF16 = jnp.bfloat16

D_MODEL = 1024
MEM_LEN = 256
MLA_HEADS = 8
MLA_NOPE = 64
MLA_ROPE = 32
MLA_V = 64
MLA_Q_RANK = 256
MLA_KV_RANK = 128
FOX_HEADS = 8
FOX_HEAD_DIM = 64
X_HEADS = 4
X_HEAD_DIM = D_MODEL // X_HEADS
N_EXPERTS = 32
TOP_K = 4
D_EXPERT = D_MODEL
SWIGLU_LIMIT = 7.0
SWIGLU_ALPHA = 1.702
ROPE_THETA = 10000.0
EPS = 1e-6

N_HEADS = MLA_HEADS + FOX_HEADS
HEAD_LANES = 128
LANES = 128
NEG_BIG = -1e30
LOG2E = 1.4426950408889634
V_ONES_EVEN = 64
V_ONES_ODD = 0

_C_QLAT = 0
_C_KVLAT = _C_QLAT + MLA_Q_RANK
_C_KPE = _C_KVLAT + MLA_KV_RANK
_C_KPE_SWAP = _C_KPE + LANES
_C_FQ = _C_KPE_SWAP + LANES
_C_FK = _C_FQ + FOX_HEADS * FOX_HEAD_DIM
_C_FV = _C_FK + FOX_HEADS * FOX_HEAD_DIM
_C_FLOGIT = _C_FV + FOX_HEADS * FOX_HEAD_DIM
IN_COLS_PACKED = _C_FLOGIT + LANES

VMEM_LIMIT = 56 * 1024 * 1024

TM_PRE = 512
TK = 512
TM_POST = 512
TM_EXP = 512
RUN_ALIGN = 8
RUN_SIZES = (512, 256, 128, 64, 32, 16, 8)
RUN_SIZES_RARE, RUN_SIZES_COMMON = RUN_SIZES[:3], RUN_SIZES[3:]
STAGE_ROWS = TOP_K * TM_POST + N_EXPERTS * RUN_ALIGN


def _rms(x, eps=EPS):
    return x * lax.rsqrt(jnp.mean(x * x, axis=-1, keepdims=True) + eps)


def _dot(a, b):
    return jnp.dot(a, b, preferred_element_type=F32)


def _dot_nt(a, b):
    return lax.dot_general(a, b, (((1,), (1,)), ((), ())), preferred_element_type=F32)


def _split3(x):
    hi = x.astype(BF16)
    r = x - hi.astype(F32)
    mid = r.astype(BF16)
    lo = (r - mid.astype(F32)).astype(BF16)
    return hi, mid, lo


def _premix_kernel(x_ref, g_ref, win_ref, qn_ref, wqm_ref, wqs_ref, kvn_ref, wkk_ref, wkv_ref,
                   cq_ref, sq_ref, ck_ref, sk_ref, bf_ref, ltri_ref, eq_ref, ek_ref, oq_ref, ok_ref,
                   q_out, k_out, v_out, carry_ref):
    i = pl.program_id(1)

    @pl.when(i == 0)
    def _():
        carry_ref[...] = jnp.zeros_like(carry_ref)

    x = x_ref[0]
    hn = (_rms(x) * g_ref[...]).astype(BF16)
    proj = _dot(hn, win_ref[...])

    qn = (_rms(proj[:, _C_QLAT:_C_QLAT + MLA_Q_RANK]) * qn_ref[...]).astype(BF16)
    qm = _dot(qn, wqm_ref[...])
    qs = _dot(qn, wqs_ref[...])
    cq = cq_ref[...]
    sq = sq_ref[...]
    for h in range(MLA_HEADS):
        sl = slice(h * HEAD_LANES, (h + 1) * HEAD_LANES)
        q_out[0, h] = (qm[:, sl] * cq + qs[:, sl] * sq).astype(BF16)

    kvn = (_rms(proj[:, _C_KVLAT:_C_KVLAT + MLA_KV_RANK]) * kvn_ref[...]).astype(BF16)
    kk = _dot(kvn, wkk_ref[...])
    vv = _dot(kvn, wkv_ref[...])
    kr = (proj[:, _C_KPE:_C_KPE + LANES] * ck_ref[...]
          + proj[:, _C_KPE_SWAP:_C_KPE_SWAP + LANES] * sk_ref[...])
    lane = lax.broadcasted_iota(jnp.int32, (TM_PRE, LANES), 1)
    low = lane < FOX_HEAD_DIM
    v_ones = (jnp.where(lane == V_ONES_EVEN, 1.0, 0.0), jnp.where(lane == V_ONES_ODD, 1.0, 0.0))
    for h in range(MLA_HEADS):
        sl = slice(h * HEAD_LANES, (h + 1) * HEAD_LANES)
        k_out[0, h] = (kk[:, sl] + kr).astype(BF16)
        v_out[0, h] = (vv[:, sl] + v_ones[h % 2]).astype(BF16)

    z = proj[:, _C_FLOGIT:_C_FLOGIT + LANES] + bf_ref[...]
    logf = jnp.minimum(z, 0.0) - jnp.log(1.0 + jnp.exp(-jnp.abs(z)))
    heads_only = lambda part: jnp.where(lane < FOX_HEADS, part.astype(F32), 0.0)

    def pack3(x):
        hi, mid, lo = _split3(x)
        return (heads_only(hi) + pltpu.roll(heads_only(mid), FOX_HEADS, 1)
                + pltpu.roll(heads_only(lo), 2 * FOX_HEADS, 1)).astype(BF16)

    r = _dot(ltri_ref[...], pack3(logf))
    cs = r + pltpu.roll(r, LANES - FOX_HEADS, 1) + pltpu.roll(r, LANES - 2 * FOX_HEADS, 1)
    c = cs + carry_ref[0:1, :]
    carry_ref[...] = jnp.broadcast_to(c[TM_PRE - 1:TM_PRE, :], carry_ref.shape)
    cparts = pack3(c * LOG2E)
    aug_q = _dot(cparts, eq_ref[...]) + oq_ref[...]
    aug_k = _dot(cparts, ek_ref[...]) + ok_ref[...]
    scale = FOX_HEAD_DIM ** -0.5 * LOG2E
    for p in range(FOX_HEADS // 2):
        fq = proj[:, _C_FQ + p * LANES:_C_FQ + (p + 1) * LANES] * scale
        fk = proj[:, _C_FK + p * LANES:_C_FK + (p + 1) * LANES]
        fv = proj[:, _C_FV + p * LANES:_C_FV + (p + 1) * LANES]
        for par in range(2):
            h = 2 * p + par
            keep = low if par == 0 else jnp.logical_not(low)
            sl = slice(h * HEAD_LANES, (h + 1) * HEAD_LANES)
            q_out[0, MLA_HEADS + h] = (jnp.where(keep, fq, 0.0) + aug_q[:, sl]).astype(BF16)
            k_out[0, MLA_HEADS + h] = (jnp.where(keep, fk, 0.0) + aug_k[:, sl]).astype(BF16)
            v_out[0, MLA_HEADS + h] = jnp.where(keep, fv, v_ones[par]).astype(BF16)


def _premix(x, g, win, qn, wqm, wqs, kvn, wkk, wkv, cq, sq, ck, sk, bf, ltri, eq, ek, oq, ok):
    B, S, _ = x.shape
    tm = TM_PRE
    const = lambda shape: pl.BlockSpec(shape, lambda b, i: (0,) * len(shape))
    rows = lambda w: pl.BlockSpec((tm, w), lambda b, i: (i, 0))
    head_out = pl.BlockSpec((1, N_HEADS, tm, HEAD_LANES), lambda b, i: (b, 0, i, 0))
    out_sds = jax.ShapeDtypeStruct((B, N_HEADS, S, HEAD_LANES), BF16)
    return pl.pallas_call(
        _premix_kernel,
        grid=(B, S // tm),
        in_specs=[
            pl.BlockSpec((1, tm, D_MODEL), lambda b, i: (b, i, 0)),
            const(g.shape), const(win.shape), const(qn.shape), const(wqm.shape), const(wqs.shape),
            const(kvn.shape), const(wkk.shape), const(wkv.shape),
            rows(LANES), rows(LANES), rows(LANES), rows(LANES),
            const(bf.shape), const(ltri.shape), const(eq.shape), const(ek.shape),
            const(oq.shape), const(ok.shape),
        ],
        out_specs=[head_out, head_out, head_out],
        out_shape=[out_sds, out_sds, out_sds],
        scratch_shapes=[pltpu.VMEM((8, LANES), F32)],
        compiler_params=pltpu.CompilerParams(
            dimension_semantics=("arbitrary", "arbitrary"), vmem_limit_bytes=VMEM_LIMIT),
        name="premix",
    )(x, g, win, qn, wqm, wqs, kvn, wkk, wkv, cq, sq, ck, sk, bf, ltri, eq, ek, oq, ok)


def _attn_kernel(q_ref, k_ref, v_ref, o_ref, m_sc, acc_sc):
    S = q_ref.shape[2]
    m_sc[...] = jnp.full(m_sc.shape, NEG_BIG, F32)
    acc_sc[...] = jnp.zeros(acc_sc.shape, F32)

    causal = (lax.broadcasted_iota(jnp.int32, (TK, TK), 1) <= lax.broadcasted_iota(jnp.int32, (TK, TK), 0))

    def block(hh, key_start):
        rows = slice(key_start, S)
        k = k_ref[0, hh, key_start:key_start + TK, :]
        v = v_ref[0, hh, key_start:key_start + TK, :]
        s = _dot_nt(q_ref[0, hh, rows, :], k)
        top = jnp.where(causal, s[:TK], NEG_BIG)
        s = jnp.concatenate([top, s[TK:]], axis=0) if S - key_start > TK else top
        m_old = m_sc[hh, rows, :]
        m_new = jnp.maximum(m_old, jnp.max(s, axis=-1, keepdims=True))
        p = jnp.exp2(s - jnp.concatenate([m_new] * (TK // LANES), axis=1))
        alpha = jnp.exp2(m_old - m_new)
        acc_sc[hh, rows, :] = alpha * acc_sc[hh, rows, :] + _dot(p.astype(BF16), v)
        m_sc[hh, rows, :] = m_new

    for key_start in range(0, S, TK):
        for hh in range(2):
            block(hh, key_start)

    lane = lax.broadcasted_iota(jnp.int32, (S, HEAD_LANES), 1)
    a0 = acc_sc[0]
    a1 = acc_sc[1]
    out = jnp.where(lane < V_ONES_EVEN, a0 / a0[:, V_ONES_EVEN:V_ONES_EVEN + 1],
                    a1 / a1[:, V_ONES_ODD:V_ONES_ODD + 1])
    o_ref[0] = out.astype(o_ref.dtype)


def _attention(q, k, v):
    B, H, S, _ = q.shape
    return pl.pallas_call(
        _attn_kernel,
        grid=(B, H // 2),
        in_specs=[pl.BlockSpec((1, 2, S, HEAD_LANES), lambda b, p: (b, p, 0, 0))] * 3,
        out_specs=pl.BlockSpec((1, S, HEAD_LANES), lambda b, p: (b, 0, p)),
        out_shape=jax.ShapeDtypeStruct((B, S, (H // 2) * HEAD_LANES), BF16),
        scratch_shapes=[pltpu.VMEM((2, S, LANES), F32), pltpu.VMEM((2, S, HEAD_LANES), F32)],
        compiler_params=pltpu.CompilerParams(
            dimension_semantics=("arbitrary", "arbitrary"), vmem_limit_bytes=VMEM_LIMIT),
        name="attn",
    )(q, k, v)


def _memkv_kernel(mem_ref, g_ref, w_ref, o_ref):
    mn = (_rms(mem_ref[0]) * g_ref[...]).astype(BF16)
    o_ref[0] = _dot(mn, w_ref[...]).astype(o_ref.dtype)


def _memkv(mem, g, w):
    B, M, _ = mem.shape
    return pl.pallas_call(
        _memkv_kernel,
        grid=(B,),
        in_specs=[
            pl.BlockSpec((1, M, D_MODEL), lambda b: (b, 0, 0)),
            pl.BlockSpec(g.shape, lambda b: (0, 0)),
            pl.BlockSpec(w.shape, lambda b: (0, 0)),
        ],
        out_specs=pl.BlockSpec((1, M, 2 * D_MODEL), lambda b: (b, 0, 0)),
        out_shape=jax.ShapeDtypeStruct((B, M, 2 * D_MODEL), BF16),
        compiler_params=pltpu.CompilerParams(
            dimension_semantics=("arbitrary",), vmem_limit_bytes=VMEM_LIMIT),
        name="memkv",
    )(mem, g, w)


def _postmix_kernel(o_ref, x_ref, gmla_ref, gfox_ref, wo_ref, lnx_ref, wxq_ref, mkv_ref, wxo_ref,
                    lnf_ref, wr_hi_ref, wr_lo_ref, br_ref, u_ref,
                    h2_out, hn_out, idx_out, rank_out, gate_out, cnt_out):
    tm = TM_POST
    half = D_MODEL // 2
    o = o_ref[0].astype(F32)
    on = jnp.concatenate([_rms(o[:, :half]) * gmla_ref[...], _rms(o[:, half:]) * gfox_ref[...]], axis=1)
    h1 = x_ref[0] + _dot(on.astype(BF16), wo_ref[...])

    hn2 = (_rms(h1) * lnx_ref[...]).astype(BF16)
    qx = _dot(hn2, wxq_ref[...]).astype(BF16)
    heads = []
    for h in range(X_HEADS):
        sl = slice(h * X_HEAD_DIM, (h + 1) * X_HEAD_DIM)
        kh = mkv_ref[0, :, sl]
        vh = mkv_ref[0, :, D_MODEL + h * X_HEAD_DIM:D_MODEL + (h + 1) * X_HEAD_DIM]
        s = _dot_nt(qx[:, sl], kh) * (X_HEAD_DIM ** -0.5)
        p = jnp.exp(s - jnp.max(s, axis=-1, keepdims=True))
        p = p / jnp.sum(p, axis=-1, keepdims=True)
        heads.append(_dot(p.astype(BF16), vh))
    ox = jnp.concatenate(heads, axis=1).astype(BF16)
    h2 = h1 + _dot(ox, wxo_ref[...])
    h2_out[0] = h2

    hn3 = _rms(h2) * lnf_ref[...]
    hn_out[0] = hn3
    a_hi = hn3.astype(BF16)
    a_lo = (hn3 - a_hi.astype(F32)).astype(BF16)
    logits = (_dot_nt(wr_hi_ref[...], a_hi) + _dot_nt(wr_hi_ref[...], a_lo)
              + _dot_nt(wr_lo_ref[...], a_hi)) + br_ref[:, 0:1]

    eid = lax.broadcasted_iota(jnp.int32, (N_EXPERTS, tm), 0).astype(F32)
    vals = logits
    top_v, top_i, hots = [], [], []
    for _ in range(TOP_K):
        mx = jnp.max(vals, axis=0, keepdims=True)
        sel = jnp.min(jnp.where(vals == mx, eid, float(N_EXPERTS)), axis=0, keepdims=True)
        hot = eid == sel
        vals = jnp.where(hot, -jnp.inf, vals)
        top_v.append(mx)
        top_i.append(sel.astype(jnp.int32))
        hots.append(hot)
    ex = [jnp.exp(v - top_v[0]) for v in top_v]
    den = ex[0] + ex[1] + ex[2] + ex[3]
    gate_out[...] = jnp.concatenate([e / den for e in ex], axis=0)
    idx_out[...] = jnp.concatenate(top_i, axis=0)

    hot_all = jnp.where(hots[0] | hots[1] | hots[2] | hots[3], 1.0, 0.0)
    before = _dot(hot_all.astype(BF16), u_ref[...])
    ranks = [jnp.sum(jnp.where(hot, before, 0.0), axis=0, keepdims=True) for hot in hots]
    rank_out[...] = jnp.concatenate(ranks, axis=0).astype(jnp.int32)
    cnt_out[0] = jnp.broadcast_to(jnp.sum(hot_all, axis=1, keepdims=True), (N_EXPERTS, LANES))


def _postmix(o, x, gmla, gfox, wo, lnx, wxq, mkv, wxo, lnf, wr_hi, wr_lo, br, u):
    B, S, _ = x.shape
    tm = TM_POST
    nt = S // tm
    T = B * S
    const = lambda a: pl.BlockSpec(a.shape, lambda b, i: (0,) * a.ndim)
    tok = pl.BlockSpec((1, tm, D_MODEL), lambda b, i: (b, i, 0))
    route = pl.BlockSpec((TOP_K, tm), lambda b, i: (0, b * nt + i))
    return pl.pallas_call(
        _postmix_kernel,
        grid=(B, nt),
        in_specs=[
            tok, tok, const(gmla), const(gfox), const(wo), const(lnx), const(wxq),
            pl.BlockSpec((1, MEM_LEN, 2 * D_MODEL), lambda b, i: (b, 0, 0)),
            const(wxo), const(lnf), const(wr_hi), const(wr_lo), const(br), const(u),
        ],
        out_specs=[tok, tok, route, route, route,
                   pl.BlockSpec((1, N_EXPERTS, LANES), lambda b, i: (b * nt + i, 0, 0))],
        out_shape=[
            jax.ShapeDtypeStruct((B, S, D_MODEL), F32),
            jax.ShapeDtypeStruct((B, S, D_MODEL), F32),
            jax.ShapeDtypeStruct((TOP_K, T), jnp.int32),
            jax.ShapeDtypeStruct((TOP_K, T), jnp.int32),
            jax.ShapeDtypeStruct((TOP_K, T), F32),
            jax.ShapeDtypeStruct((B * nt, N_EXPERTS, LANES), F32),
        ],
        compiler_params=pltpu.CompilerParams(
            dimension_semantics=("arbitrary", "arbitrary"), vmem_limit_bytes=VMEM_LIMIT),
        name="postmix",
    )(o, x, gmla, gfox, wo, lnx, wxq, mkv, wxo, lnf, wr_hi, wr_lo, br, u)


def _for_each_chunk(c8_ref, lb_ref, gb_ref, tile, fn):
    for e in range(N_EXPERTS):
        c = c8_ref[tile * N_EXPERTS + e]
        l = lb_ref[tile * N_EXPERTS + e]
        g = gb_ref[tile * N_EXPERTS + e]
        def chunks(sizes, c=c, l=l, g=g):
            for size in sizes:
                @pl.when(jnp.bitwise_and(c, size) != 0)
                def _(size=size):
                    done = jnp.bitwise_and(c, -2 * size)
                    fn(pl.multiple_of(l + done, RUN_ALIGN), pl.multiple_of(g + done, RUN_ALIGN), size)

        pl.when(c >= RUN_SIZES_COMMON[0] * 2)(functools.partial(chunks, RUN_SIZES_RARE))
        chunks(RUN_SIZES_COMMON)


def _scatter_kernel(c8_ref, lb_ref, gb_ref, tail_ref, hn_ref, pos_ref, xs_hbm, stage, zeros, sems):
    i = pl.program_id(0)
    last = pl.num_programs(0) - 1
    slot = lax.rem(i, 2)
    rows = lax.broadcasted_iota(jnp.int32, (STAGE_ROWS, TM_POST), 0)
    pm = jnp.zeros((STAGE_ROWS, TM_POST), F32)
    for k in range(TOP_K):
        pm = jnp.where(rows == pos_ref[k:k + 1, :], 1.0, pm)
    stage[slot] = _dot(pm.astype(BF16), hn_ref[...].astype(BF16))

    def copy(s, l, g, size):
        return pltpu.make_async_copy(stage.at[s, pl.ds(l, size), :], xs_hbm.at[pl.ds(g, size), :], sems.at[s])

    _for_each_chunk(c8_ref, lb_ref, gb_ref, i, lambda l, g, size: copy(slot, l, g, size).start())

    @pl.when(i > 0)
    def _():
        _for_each_chunk(c8_ref, lb_ref, gb_ref, i - 1, lambda l, g, size: copy(1 - slot, l, g, size).wait())

    @pl.when(i == last)
    def _():
        _for_each_chunk(c8_ref, lb_ref, gb_ref, i, lambda l, g, size: copy(slot, l, g, size).wait())
        sem = sems.at[0]
        zeros[...] = jnp.zeros(zeros.shape, F32)

        def spare_tile(t, carry):
            cp = pltpu.make_async_copy(zeros, xs_hbm.at[pl.ds(pl.multiple_of(t * TM_EXP, TM_EXP), TM_EXP), :], sem)
            cp.start()
            cp.wait()
            return carry

        lax.fori_loop(tail_ref[2 * N_EXPERTS], xs_hbm.shape[0] // TM_EXP, spare_tile, 0)

        def group_tails(do):
            for e in range(N_EXPERTS):
                end = tail_ref[e]
                pad = tail_ref[N_EXPERTS + e]
                for size in RUN_SIZES[1:]:
                    @pl.when(jnp.bitwise_and(pad, size) != 0)
                    def _(size=size, end=end, pad=pad):
                        done = jnp.bitwise_and(pad, -2 * size)
                        do(pltpu.make_async_copy(
                            zeros.at[pl.ds(0, size), :],
                            xs_hbm.at[pl.ds(pl.multiple_of(end + done, RUN_ALIGN), size), :], sem))

        group_tails(lambda cp: cp.start())
        group_tails(lambda cp: cp.wait())


def _scatter(run_meta, tail, hn, pos, n_rows):
    T = hn.shape[0]
    grid_spec = pltpu.PrefetchScalarGridSpec(
        num_scalar_prefetch=4,
        grid=(T // TM_POST,),
        in_specs=[
            pl.BlockSpec((TM_POST, D_MODEL), lambda i, *_: (i, 0)),
            pl.BlockSpec((TOP_K, TM_POST), lambda i, *_: (0, i)),
        ],
        out_specs=pl.BlockSpec(memory_space=pl.ANY),
        scratch_shapes=[pltpu.VMEM((2, STAGE_ROWS, D_MODEL), F32), pltpu.VMEM((TM_EXP, D_MODEL), F32),
                        pltpu.SemaphoreType.DMA((2,))],
    )
    return pl.pallas_call(
        _scatter_kernel,
        grid_spec=grid_spec,
        out_shape=jax.ShapeDtypeStruct((n_rows, D_MODEL), F32),
        compiler_params=pltpu.CompilerParams(
            dimension_semantics=("arbitrary",), vmem_limit_bytes=VMEM_LIMIT),
        name="scatter",
    )(*run_meta, tail, hn, pos)


def _expert_kernel(exp_ref, active_ref, newexp_ref, nextexp_ref, wslot_ref,
                   xs_ref, bg_ref, bl_ref, bd_ref, perm_ref, wgu_hbm, wd_hbm, y_ref,
                   wgu_buf, wd_buf, wg_s, wl_s, wd_s, wsem):
    w = pl.program_id(0)

    def fetch(e, s):
        return (pltpu.make_async_copy(wgu_hbm.at[e], wgu_buf.at[s], wsem.at[0, s]),
                pltpu.make_async_copy(wd_hbm.at[e], wd_buf.at[s], wsem.at[1, s]))

    @pl.when(newexp_ref[w] == 1)
    def _():
        s = wslot_ref[w]

        @pl.when(w == 0)
        def _():
            for cp in fetch(exp_ref[w], s):
                cp.start()

        for cp in fetch(exp_ref[w], s):
            cp.wait()

        @pl.when(nextexp_ref[w] >= 0)
        def _():
            for cp in fetch(nextexp_ref[w], 1 - s):
                cp.start()

        perm = perm_ref[...]
        for grp in range(2 * D_EXPERT // (2 * LANES)):
            blk = wgu_buf[s, :, grp * 2 * LANES:(grp + 1) * 2 * LANES].astype(BF16)
            res = _dot(blk, perm)
            wg_s[:, grp * LANES:(grp + 1) * LANES] = res[:, :LANES].astype(BF16)
            wl_s[:, grp * LANES:(grp + 1) * LANES] = res[:, LANES:].astype(BF16)
        wd_s[...] = wd_buf[s].astype(BF16)

    @pl.when(active_ref[w] == 0)
    def _():
        y_ref[...] = jnp.zeros(y_ref.shape, F32)

    @pl.when(active_ref[w] == 1)
    def _():
        x = xs_ref[...].astype(BF16)
        g = _dot(x, wg_s[...]) + bg_ref[0]
        l = _dot(x, wl_s[...]) + bl_ref[0]
        glu = jnp.minimum(g, SWIGLU_LIMIT)
        lin = jnp.clip(l, -SWIGLU_LIMIT, SWIGLU_LIMIT)
        act = glu * (1.0 / (1.0 + jnp.exp(-SWIGLU_ALPHA * glu))) * (lin + 1.0)
        y_ref[...] = _dot(act.astype(BF16), wd_s[...]) + bd_ref[0]


def _experts(meta, xs, wgu, bg, bl, wd, bd):
    R = xs.shape[0]
    n_items = R // TM_EXP
    by_tile = lambda w, exp, *_: (w, 0)
    by_exp = lambda w, exp, *_: (exp[w], 0, 0)
    perm = np.zeros((2 * LANES, 2 * LANES), np.float32)
    perm[2 * np.arange(LANES), np.arange(LANES)] = 1.0
    perm[2 * np.arange(LANES) + 1, LANES + np.arange(LANES)] = 1.0
    perm = jnp.asarray(perm, BF16)
    grid_spec = pltpu.PrefetchScalarGridSpec(
        num_scalar_prefetch=5,
        grid=(n_items,),
        in_specs=[
            pl.BlockSpec((TM_EXP, D_MODEL), by_tile),
            pl.BlockSpec((1, 1, D_EXPERT), by_exp),
            pl.BlockSpec((1, 1, D_EXPERT), by_exp),
            pl.BlockSpec((1, 1, D_MODEL), by_exp),
            pl.BlockSpec(perm.shape, lambda w, *_: (0, 0)),
            pl.BlockSpec(memory_space=pl.ANY),
            pl.BlockSpec(memory_space=pl.ANY),
        ],
        out_specs=pl.BlockSpec((TM_EXP, D_MODEL), by_tile),
        scratch_shapes=[pltpu.VMEM((2, D_MODEL, 2 * D_EXPERT), F32), pltpu.VMEM((2, D_EXPERT, D_MODEL), F32),
                        pltpu.VMEM((D_MODEL, D_EXPERT), BF16), pltpu.VMEM((D_MODEL, D_EXPERT), BF16),
                        pltpu.VMEM((D_EXPERT, D_MODEL), BF16), pltpu.SemaphoreType.DMA((2, 2))],
    )
    return pl.pallas_call(
        _expert_kernel,
        grid_spec=grid_spec,
        out_shape=jax.ShapeDtypeStruct((R, D_MODEL), F32),
        compiler_params=pltpu.CompilerParams(
            dimension_semantics=("arbitrary",), vmem_limit_bytes=VMEM_LIMIT),
        name="experts",
    )(*meta, xs, bg, bl, bd, perm, wgu, wd)


def _combine_kernel(c8_ref, lb_ref, gb_ref, h2_ref, gate_ref, pos_ref, lnf_ref, y_hbm, out_ref, stage, sem):
    i = pl.program_id(0)
    n = pl.num_programs(0)
    slot = lax.rem(i, 2)
    nxt = 1 - slot

    def copy(s, l, g, size):
        return pltpu.make_async_copy(y_hbm.at[pl.ds(g, size), :], stage.at[s, pl.ds(l, size), :], sem.at[s])

    @pl.when(i == 0)
    def _():
        stage[...] = jnp.zeros(stage.shape, F32)
        _for_each_chunk(c8_ref, lb_ref, gb_ref, 0, lambda l, g, size: copy(0, l, g, size).start())

    @pl.when(i + 1 < n)
    def _():
        _for_each_chunk(c8_ref, lb_ref, gb_ref, i + 1, lambda l, g, size: copy(nxt, l, g, size).start())

    _for_each_chunk(c8_ref, lb_ref, gb_ref, i, lambda l, g, size: copy(slot, l, g, size).wait())

    cols = lax.broadcasted_iota(jnp.int32, (TM_POST, STAGE_ROWS), 1)
    pos = pos_ref[...]
    gates = gate_ref[...]
    w = jnp.zeros((TM_POST, STAGE_ROWS), F32)
    for k in range(TOP_K):
        w = jnp.where(cols == pos[:, k:k + 1], gates[:, k:k + 1], w)
    moe = _dot(w.astype(BF16), stage[slot].astype(BF16))
    out_ref[...] = _rms(h2_ref[...] + moe) * lnf_ref[...]


def _combine(run_meta, h2, gates_t, pos_t, lnf, y):
    T = h2.shape[0]
    grid_spec = pltpu.PrefetchScalarGridSpec(
        num_scalar_prefetch=3,
        grid=(T // TM_POST,),
        in_specs=[
            pl.BlockSpec((TM_POST, D_MODEL), lambda i, *_: (i, 0)),
            pl.BlockSpec((TM_POST, TOP_K), lambda i, *_: (i, 0)),
            pl.BlockSpec((TM_POST, TOP_K), lambda i, *_: (i, 0)),
            pl.BlockSpec(lnf.shape, lambda i, *_: (0, 0)),
            pl.BlockSpec(memory_space=pl.ANY),
        ],
        out_specs=pl.BlockSpec((TM_POST, D_MODEL), lambda i, *_: (i, 0)),
        scratch_shapes=[pltpu.VMEM((2, STAGE_ROWS, D_MODEL), F32), pltpu.SemaphoreType.DMA((2,))],
    )
    return pl.pallas_call(
        _combine_kernel,
        grid_spec=grid_spec,
        out_shape=jax.ShapeDtypeStruct((T, D_MODEL), F32),
        compiler_params=pltpu.CompilerParams(
            dimension_semantics=("arbitrary",), vmem_limit_bytes=VMEM_LIMIT),
        name="combine",
    )(*run_meta, h2, gates_t, pos_t, lnf, y)


def _rope_tables(S):
    half = MLA_ROPE // 2
    inv = ROPE_THETA ** (-jnp.arange(half, dtype=F32) / half)
    ang = jnp.arange(S, dtype=F32)[:, None] * inv[None, :]
    cos, sin = jnp.cos(ang), jnp.sin(ang)
    z = lambda n: jnp.zeros((S, n), F32)
    pad = HEAD_LANES - MLA_NOPE - MLA_ROPE
    c_tab = jnp.concatenate([jnp.ones((S, MLA_NOPE), F32), cos, cos, z(pad)], axis=1)
    s_tab = jnp.concatenate([z(MLA_NOPE), sin, sin, z(pad)], axis=1)
    return c_tab, s_tab


def _pad_cols(w, left, width=HEAD_LANES):
    return jnp.pad(w, ((0, 0), (left, width - left - w.shape[1])))


def _pack_mixer_weights(w_in, w_q_up, w_kv_up):
    half = MLA_ROPE // 2
    cuts = np.cumsum([MLA_Q_RANK, MLA_KV_RANK, MLA_ROPE, 512, 512, 512])
    w_ql, w_kvl, w_kpe, w_fq, w_fk, w_fv, w_fl = jnp.split(w_in, cuts, axis=1)
    kpe_swap = jnp.concatenate([-w_kpe[:, half:], w_kpe[:, :half]], axis=1)
    win = jnp.concatenate([
        w_ql, w_kvl, _pad_cols(w_kpe, MLA_NOPE), _pad_cols(kpe_swap, MLA_NOPE),
        w_fq, w_fk, w_fv, _pad_cols(w_fl, 0)], axis=1).astype(BF16)

    dq = MLA_NOPE + MLA_ROPE
    wq = w_q_up.reshape(MLA_Q_RANK, MLA_HEADS, dq)
    zq = lambda n: jnp.zeros((MLA_Q_RANK, MLA_HEADS, n), F32)
    wqm = jnp.concatenate([wq, zq(HEAD_LANES - dq)], axis=2)
    wqs = jnp.concatenate([zq(MLA_NOPE), -wq[:, :, MLA_NOPE + half:], wq[:, :, MLA_NOPE:MLA_NOPE + half],
                           zq(HEAD_LANES - dq)], axis=2)
    wkv = w_kv_up.reshape(MLA_KV_RANK, MLA_HEADS, MLA_NOPE + MLA_V)
    zk = lambda n: jnp.zeros((MLA_KV_RANK, MLA_HEADS, n), F32)
    wkk = jnp.concatenate([wkv[:, :, :MLA_NOPE], zk(HEAD_LANES - MLA_NOPE)], axis=2)
    v_even = jnp.concatenate([wkv[:, :, MLA_NOPE:], zk(HEAD_LANES - MLA_V)], axis=2)
    v_odd = jnp.concatenate([zk(HEAD_LANES - MLA_V), wkv[:, :, MLA_NOPE:]], axis=2)
    odd = (jnp.arange(MLA_HEADS) % 2 == 1)[None, :, None]
    wkvv = jnp.where(odd, v_odd, v_even)
    flat = lambda w: w.reshape(w.shape[0], MLA_HEADS * HEAD_LANES).astype(BF16)
    return win, flat(wqm), flat(wqs), flat(wkk), flat(wkvv)


def _fox_placement():
    eq = np.zeros((LANES, FOX_HEADS * HEAD_LANES), np.float32)
    ek = np.zeros((LANES, FOX_HEADS * HEAD_LANES), np.float32)
    oq = np.zeros((1, FOX_HEADS * HEAD_LANES), np.float32)
    ok = np.zeros((1, FOX_HEADS * HEAD_LANES), np.float32)
    for h in range(FOX_HEADS):
        base = h * HEAD_LANES + (FOX_HEAD_DIM if h % 2 == 0 else 0)
        for part in range(3):
            eq[part * FOX_HEADS + h, base + part] = 1.0
            ok[0, base + part] = 1.0
            oq[0, base + 3 + part] = 1.0
            ek[part * FOX_HEADS + h, base + 3 + part] = -1.0
    return jnp.asarray(eq, BF16), jnp.asarray(ek, BF16), jnp.asarray(oq), jnp.asarray(ok)


def _row_tiles(rows_per_expert, n_rows):
    n_tiles = n_rows // TM_EXP
    tiles_per = (rows_per_expert + TM_EXP - 1) // TM_EXP
    tile_end = jnp.cumsum(tiles_per)
    starts = ((tile_end - tiles_per) * TM_EXP).astype(jnp.int32)
    w = jnp.arange(n_tiles, dtype=jnp.int32)
    experts = jnp.arange(N_EXPERTS, dtype=jnp.int32)
    e = jnp.minimum(jnp.sum(tile_end[None, :] <= w[:, None], axis=1), N_EXPERTS - 1).astype(jnp.int32)
    active = w < tile_end[-1]
    last_e = jnp.max(jnp.where(rows_per_expert > 0, experts, 0))
    e = jnp.where(active, e, last_e)
    prev_e = jnp.concatenate([jnp.full((1,), -1, jnp.int32), e[:-1]])
    newexp = (e != prev_e).astype(jnp.int32)
    has = tiles_per > 0
    later = jnp.logical_and(has[None, :], experts[None, :] > experts[:, None])
    next_of = jnp.min(jnp.where(later, experts[None, :], N_EXPERTS), axis=1)
    next_of = jnp.where(next_of < N_EXPERTS, next_of, -1)
    slot_of = (jnp.cumsum(has.astype(jnp.int32)) - 1) % 2
    hot = e[:, None] == experts[None, :]
    pick = lambda table: jnp.sum(jnp.where(hot, table[None, :], 0), axis=1).astype(jnp.int32)
    return (e, active.astype(jnp.int32), newexp, pick(next_of), pick(slot_of)), starts


def kernel(x, mem, ln_mix, w_in, q_norm, w_q_up, kv_norm, w_kv_up, b_forget, g_out_mla, g_out_fox, w_o,
           ln_cross, mem_norm, w_xq, w_mem_kv, w_xo, ln_ffn, w_router, b_router, w_gate_up, b_gate_up,
           w_down, b_down, ln_final):
    B, S, _ = x.shape
    T = B * S
    row = lambda v: v.reshape(1, -1).astype(F32)

    win, wqm, wqs, wkk, wkvv = _pack_mixer_weights(w_in[0], w_q_up[0], w_kv_up[0])
    c_tab, s_tab = _rope_tables(S)
    q_scale = (MLA_NOPE + MLA_ROPE) ** -0.5 * LOG2E
    eq, ek, oq, ok = _fox_placement()
    ltri = jnp.asarray(np.tril(np.ones((TM_PRE, TM_PRE), np.float32)), BF16)
    bf = _pad_cols(row(b_forget[0]), 0)
    q, k, v = _premix(x, row(ln_mix[0]), win, row(q_norm[0]), wqm, wqs, row(kv_norm[0]), wkk, wkvv,
                      c_tab * q_scale, s_tab * q_scale, c_tab, s_tab, bf, ltri, eq, ek, oq, ok)
    o = _attention(q, k, v)

    mkv = _memkv(mem, row(mem_norm[0]), w_mem_kv[0].astype(BF16))
    wr = w_router[0].T
    wr_hi = wr.astype(BF16)
    wr_lo = (wr - wr_hi.astype(F32)).astype(BF16)
    br = jnp.broadcast_to(b_router[0].astype(F32)[:, None], (N_EXPERTS, LANES))
    u = jnp.asarray(np.triu(np.ones((TM_POST, TM_POST), np.float32), 1), BF16)
    h2, hn3, idx, rank, gates, tcnt = _postmix(
        o, x, row(g_out_mla[0]), row(g_out_fox[0]), w_o[0].astype(BF16), row(ln_cross[0]),
        w_xq[0].astype(BF16), mkv, w_xo[0].astype(BF16), row(ln_ffn[0]), wr_hi, wr_lo, br, u)

    counts = tcnt[:, :, 0].astype(jnp.int32)
    c8 = (counts + RUN_ALIGN - 1) // RUN_ALIGN * RUN_ALIGN
    lb = jnp.cumsum(c8, axis=1) - c8
    worst_rows = T * TOP_K + (T // TM_POST) * N_EXPERTS * (RUN_ALIGN - 1) + N_EXPERTS * (TM_EXP - RUN_ALIGN)
    n_rows = -(-worst_rows // TM_EXP) * TM_EXP
    group_rows = jnp.sum(c8, axis=0)
    meta, starts = _row_tiles(group_rows, n_rows)
    gb = starts[None, :] + jnp.cumsum(c8, axis=0) - c8
    run_meta = (c8.reshape(-1), lb.reshape(-1), gb.reshape(-1))
    ends = starts + group_rows
    used_tiles = jnp.sum((group_rows + TM_EXP - 1) // TM_EXP)
    tail = jnp.concatenate([ends, (-ends) % TM_EXP, used_tiles[None]]).astype(jnp.int32)
    lb_tok = jnp.repeat(lb, TM_POST, axis=0)
    pos = rank + jnp.sum(jnp.where(idx[..., None] == jnp.arange(N_EXPERTS, dtype=jnp.int32),
                                   lb_tok[None], 0), axis=-1)

    xs = _scatter(run_meta, tail, hn3.reshape(T, D_MODEL), pos, n_rows)
    bgu = b_gate_up[0].reshape(N_EXPERTS, 1, D_EXPERT, 2).astype(F32)
    y = _experts(meta, xs, w_gate_up[0], bgu[..., 0], bgu[..., 1],
                 w_down[0], b_down[0].reshape(N_EXPERTS, 1, D_MODEL).astype(F32))

    out = _combine(run_meta, h2.reshape(T, D_MODEL), gates.T, pos.T, row(ln_final), y)
    return out.reshape(B, S, D_MODEL)
```

```python
import functools

import jax
import jax.numpy as jnp
import numpy as np
from jax import lax
from jax.experimental import pallas as pl
from jax.experimental.pallas import tpu as pltpu

F32 = jnp.float32
BF16 = jnp.bfloat16

D_MODEL = 1024
MEM_LEN = 256
MLA_HEADS = 8
MLA_NOPE = 64
MLA_ROPE = 32
MLA_V = 64
MLA_Q_RANK = 256
MLA_KV_RANK = 128
FOX_HEADS = 8
FOX_HEAD_DIM = 64
X_HEADS = 4
X_HEAD_DIM = D_MODEL // X_HEADS
N_EXPERTS = 32
TOP_K = 4
D_EXPERT = D_MODEL
SWIGLU_LIMIT = 7.0
SWIGLU_ALPHA = 1.702
ROPE_THETA = 10000.0
EPS = 1e-6

N_HEADS = MLA_HEADS + FOX_HEADS
HEAD_LANES = 128
LANES = 128
NEG_BIG = -1e30
LOG2E = 1.4426950408889634
V_ONES_EVEN = 64
V_ONES_ODD = 0

_C_QLAT = 0
_C_KVLAT = _C_QLAT + MLA_Q_RANK
_C_KPE = _C_KVLAT + MLA_KV_RANK
_C_KPE_SWAP = _C_KPE + LANES
_C_FQ = _C_KPE_SWAP + LANES
_C_FK = _C_FQ + FOX_HEADS * FOX_HEAD_DIM
_C_FV = _C_FK + FOX_HEADS * FOX_HEAD_DIM
_C_FLOGIT = _C_FV + FOX_HEADS * FOX_HEAD_DIM
IN_COLS_PACKED = _C_FLOGIT + LANES

VMEM_LIMIT = 56 * 1024 * 1024

TM_PRE = 512
TK = 512
TM_POST = 512
TM_EXP = 512
RUN_ALIGN = 8
RUN_SIZES = tuple(TM_POST >> i for i in range((TM_POST // RUN_ALIGN).bit_length()))
RUN_SIZES_RARE, RUN_SIZES_COMMON = RUN_SIZES[:3], RUN_SIZES[3:]
TAIL_SIZES = tuple(TM_EXP >> i for i in range(1, (TM_EXP // RUN_ALIGN).bit_length()))
STAGE_ROWS = TOP_K * TM_POST + N_EXPERTS * RUN_ALIGN


def _rms(x, eps=EPS):
    return x * lax.rsqrt(jnp.mean(x * x, axis=-1, keepdims=True) + eps)


def _dot(a, b):
    return jnp.dot(a, b, preferred_element_type=F32)


def _dot_nt(a, b):
    return lax.dot_general(a, b, (((1,), (1,)), ((), ())), preferred_element_type=F32)


def _split3(x):
    hi = x.astype(BF16)
    r = x - hi.astype(F32)
    mid = r.astype(BF16)
    lo = (r - mid.astype(F32)).astype(BF16)
    return hi, mid, lo


def _premix_kernel(x_ref, g_ref, win_ref, qn_ref, wqm_ref, wqs_ref, kvn_ref, wkk_ref, wkv_ref,
                   cq_ref, sq_ref, ck_ref, sk_ref, bf_ref, ltri_ref, eq_ref, ek_ref, oq_ref, ok_ref,
                   q_out, k_out, v_out, carry_ref):
    i = pl.program_id(1)

    @pl.when(i == 0)
    def _():
        carry_ref[...] = jnp.zeros_like(carry_ref)

    x = x_ref[0]
    hn = (_rms(x) * g_ref[...]).astype(BF16)
    proj = _dot(hn, win_ref[...])

    qn = (_rms(proj[:, _C_QLAT:_C_QLAT + MLA_Q_RANK]) * qn_ref[...]).astype(BF16)
    qm = _dot(qn, wqm_ref[...])
    qs = _dot(qn, wqs_ref[...])
    cq = cq_ref[...]
    sq = sq_ref[...]
    for h in range(MLA_HEADS):
        sl = slice(h * HEAD_LANES, (h + 1) * HEAD_LANES)
        q_out[0, h] = (qm[:, sl] * cq + qs[:, sl] * sq).astype(BF16)

    kvn = (_rms(proj[:, _C_KVLAT:_C_KVLAT + MLA_KV_RANK]) * kvn_ref[...]).astype(BF16)
    kk = _dot(kvn, wkk_ref[...])
    vv = _dot(kvn, wkv_ref[...])
    kr = (proj[:, _C_KPE:_C_KPE + LANES] * ck_ref[...]
          + proj[:, _C_KPE_SWAP:_C_KPE_SWAP + LANES] * sk_ref[...])
    lane = lax.broadcasted_iota(jnp.int32, (TM_PRE, LANES), 1)
    low = lane < FOX_HEAD_DIM
    v_ones = (jnp.where(lane == V_ONES_EVEN, 1.0, 0.0), jnp.where(lane == V_ONES_ODD, 1.0, 0.0))
    for h in range(MLA_HEADS):
        sl = slice(h * HEAD_LANES, (h + 1) * HEAD_LANES)
        k_out[0, h] = (kk[:, sl] + kr).astype(BF16)
        v_out[0, h] = (vv[:, sl] + v_ones[h % 2]).astype(BF16)

    z = proj[:, _C_FLOGIT:_C_FLOGIT + LANES] + bf_ref[...]
    logf = jnp.minimum(z, 0.0) - jnp.log(1.0 + jnp.exp(-jnp.abs(z)))
    heads_only = lambda part: jnp.where(lane < FOX_HEADS, part.astype(F32), 0.0)

    def pack3(x):
        hi, mid, lo = _split3(x)
        return (heads_only(hi) + pltpu.roll(heads_only(mid), FOX_HEADS, 1)
                + pltpu.roll(heads_only(lo), 2 * FOX_HEADS, 1)).astype(BF16)

    r = _dot(ltri_ref[...], pack3(logf))
    cs = r + pltpu.roll(r, LANES - FOX_HEADS, 1) + pltpu.roll(r, LANES - 2 * FOX_HEADS, 1)
    c = cs + carry_ref[0:1, :]
    carry_ref[...] = jnp.broadcast_to(c[TM_PRE - 1:TM_PRE, :], carry_ref.shape)
    cparts = pack3(c * LOG2E)
    aug_q = _dot(cparts, eq_ref[...]) + oq_ref[...]
    aug_k = _dot(cparts, ek_ref[...]) + ok_ref[...]
    scale = FOX_HEAD_DIM ** -0.5 * LOG2E
    for p in range(FOX_HEADS // 2):
        fq = proj[:, _C_FQ + p * LANES:_C_FQ + (p + 1) * LANES] * scale
        fk = proj[:, _C_FK + p * LANES:_C_FK + (p + 1) * LANES]
        fv = proj[:, _C_FV + p * LANES:_C_FV + (p + 1) * LANES]
        for par in range(2):
            h = 2 * p + par
            keep = low if par == 0 else jnp.logical_not(low)
            sl = slice(h * HEAD_LANES, (h + 1) * HEAD_LANES)
            q_out[0, MLA_HEADS + h] = (jnp.where(keep, fq, 0.0) + aug_q[:, sl]).astype(BF16)
            k_out[0, MLA_HEADS + h] = (jnp.where(keep, fk, 0.0) + aug_k[:, sl]).astype(BF16)
            v_out[0, MLA_HEADS + h] = jnp.where(keep, fv, v_ones[par]).astype(BF16)


def _premix(x, g, win, qn, wqm, wqs, kvn, wkk, wkv, cq, sq, ck, sk, bf, ltri, eq, ek, oq, ok):
    B, S, _ = x.shape
    tm = TM_PRE
    const = lambda shape: pl.BlockSpec(shape, lambda b, i: (0,) * len(shape))
    rows = lambda w: pl.BlockSpec((tm, w), lambda b, i: (i, 0))
    head_out = pl.BlockSpec((1, N_HEADS, tm, HEAD_LANES), lambda b, i: (b, 0, i, 0))
    out_sds = jax.ShapeDtypeStruct((B, N_HEADS, S, HEAD_LANES), BF16)
    return pl.pallas_call(
        _premix_kernel,
        grid=(B, S // tm),
        in_specs=[
            pl.BlockSpec((1, tm, D_MODEL), lambda b, i: (b, i, 0)),
            const(g.shape), const(win.shape), const(qn.shape), const(wqm.shape), const(wqs.shape),
            const(kvn.shape), const(wkk.shape), const(wkv.shape),
            rows(LANES), rows(LANES), rows(LANES), rows(LANES),
            const(bf.shape), const(ltri.shape), const(eq.shape), const(ek.shape),
            const(oq.shape), const(ok.shape),
        ],
        out_specs=[head_out, head_out, head_out],
        out_shape=[out_sds, out_sds, out_sds],
        scratch_shapes=[pltpu.VMEM((8, LANES), F32)],
        compiler_params=pltpu.CompilerParams(
            dimension_semantics=("arbitrary", "arbitrary"), vmem_limit_bytes=VMEM_LIMIT),
        name="premix",
    )(x, g, win, qn, wqm, wqs, kvn, wkk, wkv, cq, sq, ck, sk, bf, ltri, eq, ek, oq, ok)


def _attn_kernel(q_ref, k_ref, v_ref, o_ref, m_sc, acc_sc):
    S = q_ref.shape[2]
    m_sc[...] = jnp.full(m_sc.shape, NEG_BIG, F32)
    acc_sc[...] = jnp.zeros(acc_sc.shape, F32)

    causal = (lax.broadcasted_iota(jnp.int32, (TK, TK), 1) <= lax.broadcasted_iota(jnp.int32, (TK, TK), 0))

    def block(hh, key_start):
        rows = slice(key_start, S)
        k = k_ref[0, hh, key_start:key_start + TK, :]
        v = v_ref[0, hh, key_start:key_start + TK, :]
        s = _dot_nt(q_ref[0, hh, rows, :], k)
        top = jnp.where(causal, s[:TK], NEG_BIG)
        s = jnp.concatenate([top, s[TK:]], axis=0) if S - key_start > TK else top
        m_old = m_sc[hh, rows, :]
        m_new = jnp.maximum(m_old, jnp.max(s, axis=-1, keepdims=True))
        p = jnp.exp2(s - jnp.concatenate([m_new] * (TK // LANES), axis=1))
        alpha = jnp.exp2(m_old - m_new)
        acc_sc[hh, rows, :] = alpha * acc_sc[hh, rows, :] + _dot(p.astype(BF16), v)
        m_sc[hh, rows, :] = m_new

    for key_start in range(0, S, TK):
        for hh in range(2):
            block(hh, key_start)

    lane = lax.broadcasted_iota(jnp.int32, (S, HEAD_LANES), 1)
    a0 = acc_sc[0]
    a1 = acc_sc[1]
    out = jnp.where(lane < V_ONES_EVEN, a0 / a0[:, V_ONES_EVEN:V_ONES_EVEN + 1],
                    a1 / a1[:, V_ONES_ODD:V_ONES_ODD + 1])
    o_ref[0] = out.astype(o_ref.dtype)


def _attention(q, k, v):
    B, H, S, _ = q.shape
    return pl.pallas_call(
        _attn_kernel,
        grid=(B, H // 2),
        in_specs=[pl.BlockSpec((1, 2, S, HEAD_LANES), lambda b, p: (b, p, 0, 0))] * 3,
        out_specs=pl.BlockSpec((1, S, HEAD_LANES), lambda b, p: (b, 0, p)),
        out_shape=jax.ShapeDtypeStruct((B, S, (H // 2) * HEAD_LANES), BF16),
        scratch_shapes=[pltpu.VMEM((2, S, LANES), F32), pltpu.VMEM((2, S, HEAD_LANES), F32)],
        compiler_params=pltpu.CompilerParams(
            dimension_semantics=("arbitrary", "arbitrary"), vmem_limit_bytes=VMEM_LIMIT),
        name="attn",
    )(q, k, v)


def _memkv_kernel(mem_ref, g_ref, w_ref, o_ref):
    mn = (_rms(mem_ref[0]) * g_ref[...]).astype(BF16)
    o_ref[0] = _dot(mn, w_ref[...]).astype(o_ref.dtype)


def _memkv(mem, g, w):
    B, M, _ = mem.shape
    return pl.pallas_call(
        _memkv_kernel,
        grid=(B,),
        in_specs=[
            pl.BlockSpec((1, M, D_MODEL), lambda b: (b, 0, 0)),
            pl.BlockSpec(g.shape, lambda b: (0, 0)),
            pl.BlockSpec(w.shape, lambda b: (0, 0)),
        ],
        out_specs=pl.BlockSpec((1, M, 2 * D_MODEL), lambda b: (b, 0, 0)),
        out_shape=jax.ShapeDtypeStruct((B, M, 2 * D_MODEL), BF16),
        compiler_params=pltpu.CompilerParams(
            dimension_semantics=("arbitrary",), vmem_limit_bytes=VMEM_LIMIT),
        name="memkv",
    )(mem, g, w)


def _postmix_kernel(o_ref, x_ref, gmla_ref, gfox_ref, wo_ref, lnx_ref, wxq_ref, mkv_ref, wxo_ref,
                    lnf_ref, wr_hi_ref, wr_lo_ref, br_ref, u_ref,
                    h2_out, hn_out, idx_out, rank_out, gate_out, cnt_out):
    tm = TM_POST
    half = D_MODEL // 2
    o = o_ref[0].astype(F32)
    on = jnp.concatenate([_rms(o[:, :half]) * gmla_ref[...], _rms(o[:, half:]) * gfox_ref[...]], axis=1)
    h1 = x_ref[0] + _dot(on.astype(BF16), wo_ref[...])

    hn2 = (_rms(h1) * lnx_ref[...]).astype(BF16)
    qx = _dot(hn2, wxq_ref[...]).astype(BF16)
    heads = []
    for h in range(X_HEADS):
        sl = slice(h * X_HEAD_DIM, (h + 1) * X_HEAD_DIM)
        kh = mkv_ref[0, :, sl]
        vh = mkv_ref[0, :, D_MODEL + h * X_HEAD_DIM:D_MODEL + (h + 1) * X_HEAD_DIM]
        s = _dot_nt(qx[:, sl], kh) * (X_HEAD_DIM ** -0.5)
        p = jnp.exp(s - jnp.max(s, axis=-1, keepdims=True))
        p = p / jnp.sum(p, axis=-1, keepdims=True)
        heads.append(_dot(p.astype(BF16), vh))
    ox = jnp.concatenate(heads, axis=1).astype(BF16)
    h2 = h1 + _dot(ox, wxo_ref[...])
    h2_out[0] = h2

    hn3 = _rms(h2) * lnf_ref[...]
    hn_out[0] = hn3
    a_hi = hn3.astype(BF16)
    a_lo = (hn3 - a_hi.astype(F32)).astype(BF16)
    logits = (_dot_nt(wr_hi_ref[...], a_hi) + _dot_nt(wr_hi_ref[...], a_lo)
              + _dot_nt(wr_lo_ref[...], a_hi)) + br_ref[:, 0:1]

    eid = lax.broadcasted_iota(jnp.int32, (N_EXPERTS, tm), 0).astype(F32)
    vals = logits
    top_v, top_i, hots = [], [], []
    for _ in range(TOP_K):
        mx = jnp.max(vals, axis=0, keepdims=True)
        sel = jnp.min(jnp.where(vals == mx, eid, float(N_EXPERTS)), axis=0, keepdims=True)
        hot = eid == sel
        vals = jnp.where(hot, -jnp.inf, vals)
        top_v.append(mx)
        top_i.append(sel.astype(jnp.int32))
        hots.append(hot)
    ex = [jnp.exp(v - top_v[0]) for v in top_v]
    den = ex[0] + ex[1] + ex[2] + ex[3]
    gate_out[...] = jnp.concatenate([e / den for e in ex], axis=0)
    idx_out[...] = jnp.concatenate(top_i, axis=0)

    hot_all = jnp.where(hots[0] | hots[1] | hots[2] | hots[3], 1.0, 0.0)
    before = _dot(hot_all.astype(BF16), u_ref[...])
    ranks = [jnp.sum(jnp.where(hot, before, 0.0), axis=0, keepdims=True) for hot in hots]
    rank_out[...] = jnp.concatenate(ranks, axis=0).astype(jnp.int32)
    cnt_out[0] = jnp.broadcast_to(jnp.sum(hot_all, axis=1, keepdims=True), (N_EXPERTS, LANES))


def _postmix(o, x, gmla, gfox, wo, lnx, wxq, mkv, wxo, lnf, wr_hi, wr_lo, br, u):
    B, S, _ = x.shape
    tm = TM_POST
    nt = S // tm
    T = B * S
    const = lambda a: pl.BlockSpec(a.shape, lambda b, i: (0,) * a.ndim)
    tok = pl.BlockSpec((1, tm, D_MODEL), lambda b, i: (b, i, 0))
    route = pl.BlockSpec((TOP_K, tm), lambda b, i: (0, b * nt + i))
    return pl.pallas_call(
        _postmix_kernel,
        grid=(B, nt),
        in_specs=[
            tok, tok, const(gmla), const(gfox), const(wo), const(lnx), const(wxq),
            pl.BlockSpec((1, MEM_LEN, 2 * D_MODEL), lambda b, i: (b, 0, 0)),
            const(wxo), const(lnf), const(wr_hi), const(wr_lo), const(br), const(u),
        ],
        out_specs=[tok, tok, route, route, route,
                   pl.BlockSpec((1, N_EXPERTS, LANES), lambda b, i: (b * nt + i, 0, 0))],
        out_shape=[
            jax.ShapeDtypeStruct((B, S, D_MODEL), F32),
            jax.ShapeDtypeStruct((B, S, D_MODEL), F32),
            jax.ShapeDtypeStruct((TOP_K, T), jnp.int32),
            jax.ShapeDtypeStruct((TOP_K, T), jnp.int32),
            jax.ShapeDtypeStruct((TOP_K, T), F32),
            jax.ShapeDtypeStruct((B * nt, N_EXPERTS, LANES), F32),
        ],
        compiler_params=pltpu.CompilerParams(
            dimension_semantics=("arbitrary", "arbitrary"), vmem_limit_bytes=VMEM_LIMIT),
        name="postmix",
    )(o, x, gmla, gfox, wo, lnx, wxq, mkv, wxo, lnf, wr_hi, wr_lo, br, u)


def _for_each_chunk(c8_ref, lb_ref, gb_ref, tile, fn):
    for e in range(N_EXPERTS):
        c = c8_ref[tile * N_EXPERTS + e]
        l = lb_ref[tile * N_EXPERTS + e]
        g = gb_ref[tile * N_EXPERTS + e]
        def chunks(sizes, c=c, l=l, g=g):
            for size in sizes:
                @pl.when(jnp.bitwise_and(c, size) != 0)
                def _(size=size):
                    done = jnp.bitwise_and(c, -2 * size)
                    fn(pl.multiple_of(l + done, RUN_ALIGN), pl.multiple_of(g + done, RUN_ALIGN), size)

        pl.when(c >= RUN_SIZES_COMMON[0] * 2)(functools.partial(chunks, RUN_SIZES_RARE))
        chunks(RUN_SIZES_COMMON)


def _scatter_kernel(c8_ref, lb_ref, gb_ref, tail_ref, hn_ref, pos_ref, xs_hbm, stage, zeros, sems):
    i = pl.program_id(0)
    last = pl.num_programs(0) - 1
    slot = lax.rem(i, 2)
    rows = lax.broadcasted_iota(jnp.int32, (STAGE_ROWS, TM_POST), 0)
    pm = jnp.zeros((STAGE_ROWS, TM_POST), F32)
    for k in range(TOP_K):
        pm = jnp.where(rows == pos_ref[k:k + 1, :], 1.0, pm)
    stage[slot] = _dot(pm.astype(BF16), hn_ref[...].astype(BF16))

    def copy(s, l, g, size):
        return pltpu.make_async_copy(stage.at[s, pl.ds(l, size), :], xs_hbm.at[pl.ds(g, size), :], sems.at[s])

    _for_each_chunk(c8_ref, lb_ref, gb_ref, i, lambda l, g, size: copy(slot, l, g, size).start())

    @pl.when(i > 0)
    def _():
        _for_each_chunk(c8_ref, lb_ref, gb_ref, i - 1, lambda l, g, size: copy(1 - slot, l, g, size).wait())

    @pl.when(i == last)
    def _():
        _for_each_chunk(c8_ref, lb_ref, gb_ref, i, lambda l, g, size: copy(slot, l, g, size).wait())
        sem = sems.at[0]
        zeros[...] = jnp.zeros(zeros.shape, F32)

        def spare_tile(t, carry):
            cp = pltpu.make_async_copy(zeros, xs_hbm.at[pl.ds(pl.multiple_of(t * TM_EXP, TM_EXP), TM_EXP), :], sem)
            cp.start()
            cp.wait()
            return carry

        lax.fori_loop(tail_ref[2 * N_EXPERTS], xs_hbm.shape[0] // TM_EXP, spare_tile, 0)

        def group_tails(do):
            for e in range(N_EXPERTS):
                end = tail_ref[e]
                pad = tail_ref[N_EXPERTS + e]
                for size in TAIL_SIZES:
                    @pl.when(jnp.bitwise_and(pad, size) != 0)
                    def _(size=size, end=end, pad=pad):
                        done = jnp.bitwise_and(pad, -2 * size)
                        do(pltpu.make_async_copy(
                            zeros.at[pl.ds(0, size), :],
                            xs_hbm.at[pl.ds(pl.multiple_of(end + done, RUN_ALIGN), size), :], sem))

        group_tails(lambda cp: cp.start())
        group_tails(lambda cp: cp.wait())


def _scatter(run_meta, tail, hn, pos, n_rows):
    T = hn.shape[0]
    grid_spec = pltpu.PrefetchScalarGridSpec(
        num_scalar_prefetch=4,
        grid=(T // TM_POST,),
        in_specs=[
            pl.BlockSpec((TM_POST, D_MODEL), lambda i, *_: (i, 0)),
            pl.BlockSpec((TOP_K, TM_POST), lambda i, *_: (0, i)),
        ],
        out_specs=pl.BlockSpec(memory_space=pl.ANY),
        scratch_shapes=[pltpu.VMEM((2, STAGE_ROWS, D_MODEL), F32), pltpu.VMEM((TM_EXP, D_MODEL), F32),
                        pltpu.SemaphoreType.DMA((2,))],
    )
    return pl.pallas_call(
        _scatter_kernel,
        grid_spec=grid_spec,
        out_shape=jax.ShapeDtypeStruct((n_rows, D_MODEL), F32),
        compiler_params=pltpu.CompilerParams(
            dimension_semantics=("arbitrary",), vmem_limit_bytes=VMEM_LIMIT),
        name="scatter",
    )(*run_meta, tail, hn, pos)


def _expert_kernel(exp_ref, active_ref, newexp_ref, nextexp_ref, wslot_ref,
                   xs_ref, bg_ref, bl_ref, bd_ref, perm_ref, wgu_hbm, wd_hbm, y_ref,
                   wgu_buf, wd_buf, wg_s, wl_s, wd_s, wsem):
    w = pl.program_id(0)

    def fetch(e, s):
        return (pltpu.make_async_copy(wgu_hbm.at[e], wgu_buf.at[s], wsem.at[0, s]),
                pltpu.make_async_copy(wd_hbm.at[e], wd_buf.at[s], wsem.at[1, s]))

    @pl.when(newexp_ref[w] == 1)
    def _():
        s = wslot_ref[w]

        @pl.when(w == 0)
        def _():
            for cp in fetch(exp_ref[w], s):
                cp.start()

        for cp in fetch(exp_ref[w], s):
            cp.wait()

        @pl.when(nextexp_ref[w] >= 0)
        def _():
            for cp in fetch(nextexp_ref[w], 1 - s):
                cp.start()

        perm = perm_ref[...]
        for grp in range(2 * D_EXPERT // (2 * LANES)):
            blk = wgu_buf[s, :, grp * 2 * LANES:(grp + 1) * 2 * LANES].astype(BF16)
            res = _dot(blk, perm)
            wg_s[:, grp * LANES:(grp + 1) * LANES] = res[:, :LANES].astype(BF16)
            wl_s[:, grp * LANES:(grp + 1) * LANES] = res[:, LANES:].astype(BF16)
        wd_s[...] = wd_buf[s].astype(BF16)

    @pl.when(active_ref[w] == 0)
    def _():
        y_ref[...] = jnp.zeros(y_ref.shape, F32)

    @pl.when(active_ref[w] == 1)
    def _():
        x = xs_ref[...].astype(BF16)
        g = _dot(x, wg_s[...]) + bg_ref[0]
        l = _dot(x, wl_s[...]) + bl_ref[0]
        glu = jnp.minimum(g, SWIGLU_LIMIT)
        lin = jnp.clip(l, -SWIGLU_LIMIT, SWIGLU_LIMIT)
        act = glu * (1.0 / (1.0 + jnp.exp(-SWIGLU_ALPHA * glu))) * (lin + 1.0)
        y_ref[...] = _dot(act.astype(BF16), wd_s[...]) + bd_ref[0]


def _experts(meta, xs, wgu, bg, bl, wd, bd):
    R = xs.shape[0]
    n_items = R // TM_EXP
    by_tile = lambda w, exp, *_: (w, 0)
    by_exp = lambda w, exp, *_: (exp[w], 0, 0)
    perm = np.zeros((2 * LANES, 2 * LANES), np.float32)
    perm[2 * np.arange(LANES), np.arange(LANES)] = 1.0
    perm[2 * np.arange(LANES) + 1, LANES + np.arange(LANES)] = 1.0
    perm = jnp.asarray(perm, BF16)
    grid_spec = pltpu.PrefetchScalarGridSpec(
        num_scalar_prefetch=5,
        grid=(n_items,),
        in_specs=[
            pl.BlockSpec((TM_EXP, D_MODEL), by_tile),
            pl.BlockSpec((1, 1, D_EXPERT), by_exp),
            pl.BlockSpec((1, 1, D_EXPERT), by_exp),
            pl.BlockSpec((1, 1, D_MODEL), by_exp),
            pl.BlockSpec(perm.shape, lambda w, *_: (0, 0)),
            pl.BlockSpec(memory_space=pl.ANY),
            pl.BlockSpec(memory_space=pl.ANY),
        ],
        out_specs=pl.BlockSpec((TM_EXP, D_MODEL), by_tile),
        scratch_shapes=[pltpu.VMEM((2, D_MODEL, 2 * D_EXPERT), F32), pltpu.VMEM((2, D_EXPERT, D_MODEL), F32),
                        pltpu.VMEM((D_MODEL, D_EXPERT), BF16), pltpu.VMEM((D_MODEL, D_EXPERT), BF16),
                        pltpu.VMEM((D_EXPERT, D_MODEL), BF16), pltpu.SemaphoreType.DMA((2, 2))],
    )
    return pl.pallas_call(
        _expert_kernel,
        grid_spec=grid_spec,
        out_shape=jax.ShapeDtypeStruct((R, D_MODEL), F32),
        compiler_params=pltpu.CompilerParams(
            dimension_semantics=("arbitrary",), vmem_limit_bytes=VMEM_LIMIT),
        name="experts",
    )(*meta, xs, bg, bl, bd, perm, wgu, wd)


def _combine_kernel(c8_ref, lb_ref, gb_ref, h2_ref, gate_ref, pos_ref, lnf_ref, y_hbm, out_ref, stage, sem):
    i = pl.program_id(0)
    n = pl.num_programs(0)
    slot = lax.rem(i, 2)
    nxt = 1 - slot

    def copy(s, l, g, size):
        return pltpu.make_async_copy(y_hbm.at[pl.ds(g, size), :], stage.at[s, pl.ds(l, size), :], sem.at[s])

    @pl.when(i == 0)
    def _():
        stage[...] = jnp.zeros(stage.shape, F32)
        _for_each_chunk(c8_ref, lb_ref, gb_ref, 0, lambda l, g, size: copy(0, l, g, size).start())

    @pl.when(i + 1 < n)
    def _():
        _for_each_chunk(c8_ref, lb_ref, gb_ref, i + 1, lambda l, g, size: copy(nxt, l, g, size).start())

    _for_each_chunk(c8_ref, lb_ref, gb_ref, i, lambda l, g, size: copy(slot, l, g, size).wait())

    cols = lax.broadcasted_iota(jnp.int32, (TM_POST, STAGE_ROWS), 1)
    pos = pos_ref[...]
    gates = gate_ref[...]
    w = jnp.zeros((TM_POST, STAGE_ROWS), F32)
    for k in range(TOP_K):
        w = jnp.where(cols == pos[:, k:k + 1], gates[:, k:k + 1], w)
    moe = _dot(w.astype(BF16), stage[slot].astype(BF16))
    out_ref[...] = _rms(h2_ref[...] + moe) * lnf_ref[...]


def _combine(run_meta, h2, gates_t, pos_t, lnf, y):
    T = h2.shape[0]
    grid_spec = pltpu.PrefetchScalarGridSpec(
        num_scalar_prefetch=3,
        grid=(T // TM_POST,),
        in_specs=[
            pl.BlockSpec((TM_POST, D_MODEL), lambda i, *_: (i, 0)),
            pl.BlockSpec((TM_POST, TOP_K), lambda i, *_: (i, 0)),
            pl.BlockSpec((TM_POST, TOP_K), lambda i, *_: (i, 0)),
            pl.BlockSpec(lnf.shape, lambda i, *_: (0, 0)),
            pl.BlockSpec(memory_space=pl.ANY),
        ],
        out_specs=pl.BlockSpec((TM_POST, D_MODEL), lambda i, *_: (i, 0)),
        scratch_shapes=[pltpu.VMEM((2, STAGE_ROWS, D_MODEL), F32), pltpu.SemaphoreType.DMA((2,))],
    )
    return pl.pallas_call(
        _combine_kernel,
        grid_spec=grid_spec,
        out_shape=jax.ShapeDtypeStruct((T, D_MODEL), F32),
        compiler_params=pltpu.CompilerParams(
            dimension_semantics=("arbitrary",), vmem_limit_bytes=VMEM_LIMIT),
        name="combine",
    )(*run_meta, h2, gates_t, pos_t, lnf, y)


def _rope_tables(S):
    half = MLA_ROPE // 2
    inv = ROPE_THETA ** (-jnp.arange(half, dtype=F32) / half)
    ang = jnp.arange(S, dtype=F32)[:, None] * inv[None, :]
    cos, sin = jnp.cos(ang), jnp.sin(ang)
    z = lambda n: jnp.zeros((S, n), F32)
    pad = HEAD_LANES - MLA_NOPE - MLA_ROPE
    c_tab = jnp.concatenate([jnp.ones((S, MLA_NOPE), F32), cos, cos, z(pad)], axis=1)
    s_tab = jnp.concatenate([z(MLA_NOPE), sin, sin, z(pad)], axis=1)
    return c_tab, s_tab


def _pad_cols(w, left, width=HEAD_LANES):
    return jnp.pad(w, ((0, 0), (left, width - left - w.shape[1])))


def _pack_mixer_weights(w_in, w_q_up, w_kv_up):
    half = MLA_ROPE // 2
    cuts = np.cumsum([MLA_Q_RANK, MLA_KV_RANK, MLA_ROPE, 512, 512, 512])
    w_ql, w_kvl, w_kpe, w_fq, w_fk, w_fv, w_fl = jnp.split(w_in, cuts, axis=1)
    kpe_swap = jnp.concatenate([-w_kpe[:, half:], w_kpe[:, :half]], axis=1)
    win = jnp.concatenate([
        w_ql, w_kvl, _pad_cols(w_kpe, MLA_NOPE), _pad_cols(kpe_swap, MLA_NOPE),
        w_fq, w_fk, w_fv, _pad_cols(w_fl, 0)], axis=1).astype(BF16)

    dq = MLA_NOPE + MLA_ROPE
    wq = w_q_up.reshape(MLA_Q_RANK, MLA_HEADS, dq)
    zq = lambda n: jnp.zeros((MLA_Q_RANK, MLA_HEADS, n), F32)
    wqm = jnp.concatenate([wq, zq(HEAD_LANES - dq)], axis=2)
    wqs = jnp.concatenate([zq(MLA_NOPE), -wq[:, :, MLA_NOPE + half:], wq[:, :, MLA_NOPE:MLA_NOPE + half],
                           zq(HEAD_LANES - dq)], axis=2)
    wkv = w_kv_up.reshape(MLA_KV_RANK, MLA_HEADS, MLA_NOPE + MLA_V)
    zk = lambda n: jnp.zeros((MLA_KV_RANK, MLA_HEADS, n), F32)
    wkk = jnp.concatenate([wkv[:, :, :MLA_NOPE], zk(HEAD_LANES - MLA_NOPE)], axis=2)
    v_even = jnp.concatenate([wkv[:, :, MLA_NOPE:], zk(HEAD_LANES - MLA_V)], axis=2)
    v_odd = jnp.concatenate([zk(HEAD_LANES - MLA_V), wkv[:, :, MLA_NOPE:]], axis=2)
    odd = (jnp.arange(MLA_HEADS) % 2 == 1)[None, :, None]
    wkvv = jnp.where(odd, v_odd, v_even)
    flat = lambda w: w.reshape(w.shape[0], MLA_HEADS * HEAD_LANES).astype(BF16)
    return win, flat(wqm), flat(wqs), flat(wkk), flat(wkvv)


def _fox_placement():
    eq = np.zeros((LANES, FOX_HEADS * HEAD_LANES), np.float32)
    ek = np.zeros((LANES, FOX_HEADS * HEAD_LANES), np.float32)
    oq = np.zeros((1, FOX_HEADS * HEAD_LANES), np.float32)
    ok = np.zeros((1, FOX_HEADS * HEAD_LANES), np.float32)
    for h in range(FOX_HEADS):
        base = h * HEAD_LANES + (FOX_HEAD_DIM if h % 2 == 0 else 0)
        for part in range(3):
            eq[part * FOX_HEADS + h, base + part] = 1.0
            ok[0, base + part] = 1.0
            oq[0, base + 3 + part] = 1.0
            ek[part * FOX_HEADS + h, base + 3 + part] = -1.0
    return jnp.asarray(eq, BF16), jnp.asarray(ek, BF16), jnp.asarray(oq), jnp.asarray(ok)


def _row_tiles(rows_per_expert, n_rows):
    n_tiles = n_rows // TM_EXP
    tiles_per = (rows_per_expert + TM_EXP - 1) // TM_EXP
    tile_end = jnp.cumsum(tiles_per)
    starts = ((tile_end - tiles_per) * TM_EXP).astype(jnp.int32)
    w = jnp.arange(n_tiles, dtype=jnp.int32)
    experts = jnp.arange(N_EXPERTS, dtype=jnp.int32)
    e = jnp.minimum(jnp.sum(tile_end[None, :] <= w[:, None], axis=1), N_EXPERTS - 1).astype(jnp.int32)
    active = w < tile_end[-1]
    last_e = jnp.max(jnp.where(rows_per_expert > 0, experts, 0))
    e = jnp.where(active, e, last_e)
    prev_e = jnp.concatenate([jnp.full((1,), -1, jnp.int32), e[:-1]])
    newexp = (e != prev_e).astype(jnp.int32)
    has = tiles_per > 0
    later = jnp.logical_and(has[None, :], experts[None, :] > experts[:, None])
    next_of = jnp.min(jnp.where(later, experts[None, :], N_EXPERTS), axis=1)
    next_of = jnp.where(next_of < N_EXPERTS, next_of, -1)
    slot_of = (jnp.cumsum(has.astype(jnp.int32)) - 1) % 2
    hot = e[:, None] == experts[None, :]
    pick = lambda table: jnp.sum(jnp.where(hot, table[None, :], 0), axis=1).astype(jnp.int32)
    return (e, active.astype(jnp.int32), newexp, pick(next_of), pick(slot_of)), starts


def kernel(x, mem, ln_mix, w_in, q_norm, w_q_up, kv_norm, w_kv_up, b_forget, g_out_mla, g_out_fox, w_o,
           ln_cross, mem_norm, w_xq, w_mem_kv, w_xo, ln_ffn, w_router, b_router, w_gate_up, b_gate_up,
           w_down, b_down, ln_final):
    B, S, _ = x.shape
    T = B * S
    row = lambda v: v.reshape(1, -1).astype(F32)

    win, wqm, wqs, wkk, wkvv = _pack_mixer_weights(w_in[0], w_q_up[0], w_kv_up[0])
    c_tab, s_tab = _rope_tables(S)
    q_scale = (MLA_NOPE + MLA_ROPE) ** -0.5 * LOG2E
    eq, ek, oq, ok = _fox_placement()
    ltri = jnp.asarray(np.tril(np.ones((TM_PRE, TM_PRE), np.float32)), BF16)
    bf = _pad_cols(row(b_forget[0]), 0)
    q, k, v = _premix(x, row(ln_mix[0]), win, row(q_norm[0]), wqm, wqs, row(kv_norm[0]), wkk, wkvv,
                      c_tab * q_scale, s_tab * q_scale, c_tab, s_tab, bf, ltri, eq, ek, oq, ok)
    o = _attention(q, k, v)

    mkv = _memkv(mem, row(mem_norm[0]), w_mem_kv[0].astype(BF16))
    wr = w_router[0].T
    wr_hi = wr.astype(BF16)
    wr_lo = (wr - wr_hi.astype(F32)).astype(BF16)
    br = jnp.broadcast_to(b_router[0].astype(F32)[:, None], (N_EXPERTS, LANES))
    u = jnp.asarray(np.triu(np.ones((TM_POST, TM_POST), np.float32), 1), BF16)
    h2, hn3, idx, rank, gates, tcnt = _postmix(
        o, x, row(g_out_mla[0]), row(g_out_fox[0]), w_o[0].astype(BF16), row(ln_cross[0]),
        w_xq[0].astype(BF16), mkv, w_xo[0].astype(BF16), row(ln_ffn[0]), wr_hi, wr_lo, br, u)

    counts = tcnt[:, :, 0].astype(jnp.int32)
    c8 = (counts + RUN_ALIGN - 1) // RUN_ALIGN * RUN_ALIGN
    lb = jnp.cumsum(c8, axis=1) - c8
    worst_rows = T * TOP_K + (T // TM_POST) * N_EXPERTS * (RUN_ALIGN - 1) + N_EXPERTS * (TM_EXP - RUN_ALIGN)
    n_rows = -(-worst_rows // TM_EXP) * TM_EXP
    group_rows = jnp.sum(c8, axis=0)
    meta, starts = _row_tiles(group_rows, n_rows)
    gb = starts[None, :] + jnp.cumsum(c8, axis=0) - c8
    run_meta = (c8.reshape(-1), lb.reshape(-1), gb.reshape(-1))
    ends = starts + group_rows
    used_tiles = jnp.sum((group_rows + TM_EXP - 1) // TM_EXP)
    tail = jnp.concatenate([ends, (-ends) % TM_EXP, used_tiles[None]]).astype(jnp.int32)
    lb_tok = jnp.repeat(lb, TM_POST, axis=0)
    pos = rank + jnp.sum(jnp.where(idx[..., None] == jnp.arange(N_EXPERTS, dtype=jnp.int32),
                                   lb_tok[None], 0), axis=-1)

    xs = _scatter(run_meta, tail, hn3.reshape(T, D_MODEL), pos, n_rows)
    bgu = b_gate_up[0].reshape(N_EXPERTS, 1, D_EXPERT, 2).astype(F32)
    y = _experts(meta, xs, w_gate_up[0], bgu[..., 0], bgu[..., 1],
                 w_down[0], b_down[0].reshape(N_EXPERTS, 1, D_MODEL).astype(F32))

    out = _combine(run_meta, h2.reshape(T, D_MODEL), gates.T, pos.T, row(ln_final), y)
    return out.reshape(B, S, D_MODEL)
```

```python
import functools

import jax
import jax.numpy as jnp
import numpy as np
from jax import lax
from jax.experimental import pallas as pl
from jax.experimental.pallas import tpu as pltpu

F32 = jnp.float32
BF16 = jnp.bfloat16

D_MODEL = 1024
MEM_LEN = 256
MLA_HEADS = 8
MLA_NOPE = 64
MLA_ROPE = 32
MLA_V = 64
MLA_Q_RANK = 256
MLA_KV_RANK = 128
FOX_HEADS = 8
FOX_HEAD_DIM = 64
X_HEADS = 4
X_HEAD_DIM = D_MODEL // X_HEADS
N_EXPERTS = 32
TOP_K = 4
D_EXPERT = D_MODEL
SWIGLU_LIMIT = 7.0
SWIGLU_ALPHA = 1.702
ROPE_THETA = 10000.0
EPS = 1e-6

N_HEADS = MLA_HEADS + FOX_HEADS
HEAD_LANES = 128
LANES = 128
NEG_BIG = -1e30
LOG2E = 1.4426950408889634
V_ONES_EVEN = 64
V_ONES_ODD = 0

_C_QLAT = 0
_C_KVLAT = _C_QLAT + MLA_Q_RANK
_C_KPE = _C_KVLAT + MLA_KV_RANK
_C_FQ = _C_KPE + LANES
_C_FK = _C_FQ + FOX_HEADS * FOX_HEAD_DIM
_C_FV = _C_FK + FOX_HEADS * FOX_HEAD_DIM
IN_COLS_PACKED = _C_FV + FOX_HEADS * FOX_HEAD_DIM

VMEM_LIMIT = 56 * 1024 * 1024

TM_PRE = 512
TK = 512
TM_POST = 512
TM_EXP = 512
RUN_ALIGN = 8
RUN_SIZES = tuple(TM_POST >> i for i in range((TM_POST // RUN_ALIGN).bit_length()))
RUN_SIZES_RARE, RUN_SIZES_COMMON = RUN_SIZES[:3], RUN_SIZES[3:]
TAIL_SIZES = tuple(TM_EXP >> i for i in range(1, (TM_EXP // RUN_ALIGN).bit_length()))
STAGE_ROWS = TOP_K * TM_POST + N_EXPERTS * RUN_ALIGN


def _rms(x, eps=EPS):
    return x * lax.rsqrt(jnp.mean(x * x, axis=-1, keepdims=True) + eps)


def _dot(a, b):
    return jnp.dot(a, b, preferred_element_type=F32)


def _dot_nt(a, b):
    return lax.dot_general(a, b, (((1,), (1,)), ((), ())), preferred_element_type=F32)


def _split3(x):
    hi = x.astype(BF16)
    r = x - hi.astype(F32)
    mid = r.astype(BF16)
    lo = (r - mid.astype(F32)).astype(BF16)
    return hi, mid, lo


def _premix_kernel(x_ref, g_ref, win_ref, qn_ref, wqm_ref, wqs_ref, kvn_ref, wkk_ref, wkv_ref,
                   cq_ref, sq_ref, ck_ref, skn_ref, skp_ref, bf_ref, ltri_ref, eq_ref, ek_ref, oq_ref, ok_ref,
                   q_out, k_out, v_out, carry_ref):
    i = pl.program_id(1)

    @pl.when(i == 0)
    def _():
        carry_ref[...] = jnp.zeros_like(carry_ref)

    x = x_ref[0]
    hn = (_rms(x) * g_ref[...]).astype(BF16)
    proj = _dot(hn, win_ref[...])

    qn = (_rms(proj[:, _C_QLAT:_C_QLAT + MLA_Q_RANK]) * qn_ref[...]).astype(BF16)
    qm = _dot(qn, wqm_ref[...])
    qs = _dot(qn, wqs_ref[...])
    cq = cq_ref[...]
    sq = sq_ref[...]
    for h in range(MLA_HEADS):
        sl = slice(h * HEAD_LANES, (h + 1) * HEAD_LANES)
        q_out[0, h] = (qm[:, sl] * cq + qs[:, sl] * sq).astype(BF16)

    kvn = (_rms(proj[:, _C_KVLAT:_C_KVLAT + MLA_KV_RANK]) * kvn_ref[...]).astype(BF16)
    kk = _dot(kvn, wkk_ref[...])
    vv = _dot(kvn, wkv_ref[...])
    kf = proj[:, _C_KPE:_C_KPE + LANES]
    half = MLA_ROPE // 2
    kr = (kf * ck_ref[...] + pltpu.roll(kf, LANES - half, 1) * skn_ref[...]
          + pltpu.roll(kf, half, 1) * skp_ref[...])
    lane = lax.broadcasted_iota(jnp.int32, (TM_PRE, LANES), 1)
    low = lane < FOX_HEAD_DIM
    v_ones = (jnp.where(lane == V_ONES_EVEN, 1.0, 0.0), jnp.where(lane == V_ONES_ODD, 1.0, 0.0))
    for h in range(MLA_HEADS):
        sl = slice(h * HEAD_LANES, (h + 1) * HEAD_LANES)
        k_out[0, h] = (kk[:, sl] + kr).astype(BF16)
        v_out[0, h] = (vv[:, sl] + v_ones[h % 2]).astype(BF16)

    z = kf + bf_ref[...]
    logf = jnp.minimum(z, 0.0) - jnp.log(1.0 + jnp.exp(-jnp.abs(z)))
    heads_only = lambda part: jnp.where(lane < FOX_HEADS, part.astype(F32), 0.0)

    def pack3(x):
        hi, mid, lo = _split3(x)
        return (heads_only(hi) + pltpu.roll(heads_only(mid), FOX_HEADS, 1)
                + pltpu.roll(heads_only(lo), 2 * FOX_HEADS, 1)).astype(BF16)

    r = _dot(ltri_ref[...], pack3(logf))
    cs = r + pltpu.roll(r, LANES - FOX_HEADS, 1) + pltpu.roll(r, LANES - 2 * FOX_HEADS, 1)
    c = cs + carry_ref[0:1, :]
    carry_ref[...] = jnp.broadcast_to(c[TM_PRE - 1:TM_PRE, :], carry_ref.shape)
    cparts = pack3(c * LOG2E)
    aug_q = _dot(cparts, eq_ref[...]) + oq_ref[...]
    aug_k = _dot(cparts, ek_ref[...]) + ok_ref[...]
    scale = FOX_HEAD_DIM ** -0.5 * LOG2E
    for p in range(FOX_HEADS // 2):
        fq = proj[:, _C_FQ + p * LANES:_C_FQ + (p + 1) * LANES] * scale
        fk = proj[:, _C_FK + p * LANES:_C_FK + (p + 1) * LANES]
        fv = proj[:, _C_FV + p * LANES:_C_FV + (p + 1) * LANES]
        for par in range(2):
            h = 2 * p + par
            keep = low if par == 0 else jnp.logical_not(low)
            sl = slice(h * HEAD_LANES, (h + 1) * HEAD_LANES)
            q_out[0, MLA_HEADS + h] = (jnp.where(keep, fq, 0.0) + aug_q[:, sl]).astype(BF16)
            k_out[0, MLA_HEADS + h] = (jnp.where(keep, fk, 0.0) + aug_k[:, sl]).astype(BF16)
            v_out[0, MLA_HEADS + h] = jnp.where(keep, fv, v_ones[par]).astype(BF16)


def _premix(x, g, win, qn, wqm, wqs, kvn, wkk, wkv, cq, sq, ck, skn, skp, bf, ltri, eq, ek, oq, ok):
    B, S, _ = x.shape
    tm = TM_PRE
    const = lambda shape: pl.BlockSpec(shape, lambda b, i: (0,) * len(shape))
    rows = lambda w: pl.BlockSpec((tm, w), lambda b, i: (i, 0))
    head_out = pl.BlockSpec((1, N_HEADS, tm, HEAD_LANES), lambda b, i: (b, 0, i, 0))
    out_sds = jax.ShapeDtypeStruct((B, N_HEADS, S, HEAD_LANES), BF16)
    return pl.pallas_call(
        _premix_kernel,
        grid=(B, S // tm),
        in_specs=[
            pl.BlockSpec((1, tm, D_MODEL), lambda b, i: (b, i, 0)),
            const(g.shape), const(win.shape), const(qn.shape), const(wqm.shape), const(wqs.shape),
            const(kvn.shape), const(wkk.shape), const(wkv.shape),
            rows(LANES), rows(LANES), rows(LANES), rows(LANES), rows(LANES),
            const(bf.shape), const(ltri.shape), const(eq.shape), const(ek.shape),
            const(oq.shape), const(ok.shape),
        ],
        out_specs=[head_out, head_out, head_out],
        out_shape=[out_sds, out_sds, out_sds],
        scratch_shapes=[pltpu.VMEM((8, LANES), F32)],
        compiler_params=pltpu.CompilerParams(
            dimension_semantics=("arbitrary", "arbitrary"), vmem_limit_bytes=VMEM_LIMIT),
        name="premix",
    )(x, g, win, qn, wqm, wqs, kvn, wkk, wkv, cq, sq, ck, skn, skp, bf, ltri, eq, ek, oq, ok)


def _attn_kernel(q_ref, k_ref, v_ref, o_ref, m_sc, acc_sc):
    S = q_ref.shape[2]
    m_sc[...] = jnp.full(m_sc.shape, NEG_BIG, F32)
    acc_sc[...] = jnp.zeros(acc_sc.shape, F32)

    causal = (lax.broadcasted_iota(jnp.int32, (TK, TK), 1) <= lax.broadcasted_iota(jnp.int32, (TK, TK), 0))

    def block(hh, key_start):
        rows = slice(key_start, S)
        k = k_ref[0, hh, key_start:key_start + TK, :]
        v = v_ref[0, hh, key_start:key_start + TK, :]
        s = _dot_nt(q_ref[0, hh, rows, :], k)
        top = jnp.where(causal, s[:TK], NEG_BIG)
        s = jnp.concatenate([top, s[TK:]], axis=0) if S - key_start > TK else top
        m_old = m_sc[hh, rows, :]
        m_new = jnp.maximum(m_old, jnp.max(s, axis=-1, keepdims=True))
        p = jnp.exp2(s - jnp.concatenate([m_new] * (TK // LANES), axis=1))
        alpha = jnp.exp2(m_old - m_new)
        acc_sc[hh, rows, :] = alpha * acc_sc[hh, rows, :] + _dot(p.astype(BF16), v)
        m_sc[hh, rows, :] = m_new

    for key_start in range(0, S, TK):
        for hh in range(2):
            block(hh, key_start)

    lane = lax.broadcasted_iota(jnp.int32, (S, HEAD_LANES), 1)
    a0 = acc_sc[0]
    a1 = acc_sc[1]
    out = jnp.where(lane < V_ONES_EVEN, a0 / a0[:, V_ONES_EVEN:V_ONES_EVEN + 1],
                    a1 / a1[:, V_ONES_ODD:V_ONES_ODD + 1])
    o_ref[0] = out.astype(o_ref.dtype)


def _attention(q, k, v):
    B, H, S, _ = q.shape
    return pl.pallas_call(
        _attn_kernel,
        grid=(B, H // 2),
        in_specs=[pl.BlockSpec((1, 2, S, HEAD_LANES), lambda b, p: (b, p, 0, 0))] * 3,
        out_specs=pl.BlockSpec((1, S, HEAD_LANES), lambda b, p: (b, 0, p)),
        out_shape=jax.ShapeDtypeStruct((B, S, (H // 2) * HEAD_LANES), BF16),
        scratch_shapes=[pltpu.VMEM((2, S, LANES), F32), pltpu.VMEM((2, S, HEAD_LANES), F32)],
        compiler_params=pltpu.CompilerParams(
            dimension_semantics=("arbitrary", "arbitrary"), vmem_limit_bytes=VMEM_LIMIT),
        name="attn",
    )(q, k, v)


def _memkv_kernel(mem_ref, g_ref, w_ref, o_ref):
    mn = (_rms(mem_ref[0]) * g_ref[...]).astype(BF16)
    o_ref[0] = _dot(mn, w_ref[...]).astype(o_ref.dtype)


def _memkv(mem, g, w):
    B, M, _ = mem.shape
    return pl.pallas_call(
        _memkv_kernel,
        grid=(B,),
        in_specs=[
            pl.BlockSpec((1, M, D_MODEL), lambda b: (b, 0, 0)),
            pl.BlockSpec(g.shape, lambda b: (0, 0)),
            pl.BlockSpec(w.shape, lambda b: (0, 0)),
        ],
        out_specs=pl.BlockSpec((1, M, 2 * D_MODEL), lambda b: (b, 0, 0)),
        out_shape=jax.ShapeDtypeStruct((B, M, 2 * D_MODEL), BF16),
        compiler_params=pltpu.CompilerParams(
            dimension_semantics=("arbitrary",), vmem_limit_bytes=VMEM_LIMIT),
        name="memkv",
    )(mem, g, w)


def _postmix_kernel(o_ref, x_ref, gmla_ref, gfox_ref, wo_ref, lnx_ref, wxq_ref, mkv_ref, wxo_ref,
                    lnf_ref, wr_hi_ref, wr_lo_ref, br_ref, u_ref,
                    h2_out, hn_out, idx_out, rank_out, gate_out, cnt_out):
    tm = TM_POST
    half = D_MODEL // 2
    o = o_ref[0].astype(F32)
    on = jnp.concatenate([_rms(o[:, :half]) * gmla_ref[...], _rms(o[:, half:]) * gfox_ref[...]], axis=1)
    h1 = x_ref[0] + _dot(on.astype(BF16), wo_ref[...])

    hn2 = (_rms(h1) * lnx_ref[...]).astype(BF16)
    qx = _dot(hn2, wxq_ref[...]).astype(BF16)
    heads = []
    for h in range(X_HEADS):
        sl = slice(h * X_HEAD_DIM, (h + 1) * X_HEAD_DIM)
        kh = mkv_ref[0, :, sl]
        vh = mkv_ref[0, :, D_MODEL + h * X_HEAD_DIM:D_MODEL + (h + 1) * X_HEAD_DIM]
        s = _dot_nt(qx[:, sl], kh) * (X_HEAD_DIM ** -0.5)
        p = jnp.exp(s - jnp.max(s, axis=-1, keepdims=True))
        p = p / jnp.sum(p, axis=-1, keepdims=True)
        heads.append(_dot(p.astype(BF16), vh))
    ox = jnp.concatenate(heads, axis=1).astype(BF16)
    h2 = h1 + _dot(ox, wxo_ref[...])
    h2_out[0] = h2

    hn3 = _rms(h2) * lnf_ref[...]
    hn_out[0] = hn3
    a_hi = hn3.astype(BF16)
    a_lo = (hn3 - a_hi.astype(F32)).astype(BF16)
    logits = (_dot_nt(wr_hi_ref[...], a_hi) + _dot_nt(wr_hi_ref[...], a_lo)
              + _dot_nt(wr_lo_ref[...], a_hi)) + br_ref[:, 0:1]

    eid = lax.broadcasted_iota(jnp.int32, (N_EXPERTS, tm), 0).astype(F32)
    vals = logits
    top_v, top_i, hots = [], [], []
    for _ in range(TOP_K):
        mx = jnp.max(vals, axis=0, keepdims=True)
        sel = jnp.min(jnp.where(vals == mx, eid, float(N_EXPERTS)), axis=0, keepdims=True)
        hot = eid == sel
        vals = jnp.where(hot, -jnp.inf, vals)
        top_v.append(mx)
        top_i.append(sel.astype(jnp.int32))
        hots.append(hot)
    ex = [jnp.exp(v - top_v[0]) for v in top_v]
    den = ex[0] + ex[1] + ex[2] + ex[3]
    gate_out[...] = jnp.concatenate([e / den for e in ex], axis=0)
    idx_out[...] = jnp.concatenate(top_i, axis=0)

    hot_all = jnp.where(hots[0] | hots[1] | hots[2] | hots[3], 1.0, 0.0)
    before = _dot(hot_all.astype(BF16), u_ref[...])
    ranks = [jnp.sum(jnp.where(hot, before, 0.0), axis=0, keepdims=True) for hot in hots]
    rank_out[...] = jnp.concatenate(ranks, axis=0).astype(jnp.int32)
    cnt_out[0] = jnp.broadcast_to(jnp.sum(hot_all, axis=1, keepdims=True), (N_EXPERTS, LANES))


def _postmix(o, x, gmla, gfox, wo, lnx, wxq, mkv, wxo, lnf, wr_hi, wr_lo, br, u):
    B, S, _ = x.shape
    tm = TM_POST
    nt = S // tm
    T = B * S
    const = lambda a: pl.BlockSpec(a.shape, lambda b, i: (0,) * a.ndim)
    tok = pl.BlockSpec((1, tm, D_MODEL), lambda b, i: (b, i, 0))
    route = pl.BlockSpec((TOP_K, tm), lambda b, i: (0, b * nt + i))
    return pl.pallas_call(
        _postmix_kernel,
        grid=(B, nt),
        in_specs=[
            tok, tok, const(gmla), const(gfox), const(wo), const(lnx), const(wxq),
            pl.BlockSpec((1, MEM_LEN, 2 * D_MODEL), lambda b, i: (b, 0, 0)),
            const(wxo), const(lnf), const(wr_hi), const(wr_lo), const(br), const(u),
        ],
        out_specs=[tok, tok, route, route, route,
                   pl.BlockSpec((1, N_EXPERTS, LANES), lambda b, i: (b * nt + i, 0, 0))],
        out_shape=[
            jax.ShapeDtypeStruct((B, S, D_MODEL), F32),
            jax.ShapeDtypeStruct((B, S, D_MODEL), F32),
            jax.ShapeDtypeStruct((TOP_K, T), jnp.int32),
            jax.ShapeDtypeStruct((TOP_K, T), jnp.int32),
            jax.ShapeDtypeStruct((TOP_K, T), F32),
            jax.ShapeDtypeStruct((B * nt, N_EXPERTS, LANES), F32),
        ],
        compiler_params=pltpu.CompilerParams(
            dimension_semantics=("arbitrary", "arbitrary"), vmem_limit_bytes=VMEM_LIMIT),
        name="postmix",
    )(o, x, gmla, gfox, wo, lnx, wxq, mkv, wxo, lnf, wr_hi, wr_lo, br, u)


def _for_each_chunk(c8_ref, lb_ref, gb_ref, tile, fn):
    for e in range(N_EXPERTS):
        c = c8_ref[tile * N_EXPERTS + e]
        l = lb_ref[tile * N_EXPERTS + e]
        g = gb_ref[tile * N_EXPERTS + e]
        def chunks(sizes, c=c, l=l, g=g):
            for size in sizes:
                @pl.when(jnp.bitwise_and(c, size) != 0)
                def _(size=size):
                    done = jnp.bitwise_and(c, -2 * size)
                    fn(pl.multiple_of(l + done, RUN_ALIGN), pl.multiple_of(g + done, RUN_ALIGN), size)

        pl.when(c >= RUN_SIZES_COMMON[0] * 2)(functools.partial(chunks, RUN_SIZES_RARE))
        chunks(RUN_SIZES_COMMON)


def _scatter_kernel(c8_ref, lb_ref, gb_ref, tail_ref, hn_ref, pos_ref, xs_hbm, stage, zeros, sems):
    i = pl.program_id(0)
    last = pl.num_programs(0) - 1
    slot = lax.rem(i, 2)
    rows = lax.broadcasted_iota(jnp.int32, (STAGE_ROWS, TM_POST), 0)
    pm = jnp.zeros((STAGE_ROWS, TM_POST), F32)
    for k in range(TOP_K):
        pm = jnp.where(rows == pos_ref[k:k + 1, :], 1.0, pm)
    stage[slot] = _dot(pm.astype(BF16), hn_ref[...].astype(BF16))

    def copy(s, l, g, size):
        return pltpu.make_async_copy(stage.at[s, pl.ds(l, size), :], xs_hbm.at[pl.ds(g, size), :], sems.at[s])

    _for_each_chunk(c8_ref, lb_ref, gb_ref, i, lambda l, g, size: copy(slot, l, g, size).start())

    @pl.when(i > 0)
    def _():
        _for_each_chunk(c8_ref, lb_ref, gb_ref, i - 1, lambda l, g, size: copy(1 - slot, l, g, size).wait())

    @pl.when(i == last)
    def _():
        _for_each_chunk(c8_ref, lb_ref, gb_ref, i, lambda l, g, size: copy(slot, l, g, size).wait())
        sem = sems.at[0]
        zeros[...] = jnp.zeros(zeros.shape, F32)

        def spare_tile(t, carry):
            cp = pltpu.make_async_copy(zeros, xs_hbm.at[pl.ds(pl.multiple_of(t * TM_EXP, TM_EXP), TM_EXP), :], sem)
            cp.start()
            cp.wait()
            return carry

        lax.fori_loop(tail_ref[2 * N_EXPERTS], xs_hbm.shape[0] // TM_EXP, spare_tile, 0)

        def group_tails(do):
            for e in range(N_EXPERTS):
                end = tail_ref[e]
                pad = tail_ref[N_EXPERTS + e]
                for size in TAIL_SIZES:
                    @pl.when(jnp.bitwise_and(pad, size) != 0)
                    def _(size=size, end=end, pad=pad):
                        done = jnp.bitwise_and(pad, -2 * size)
                        do(pltpu.make_async_copy(
                            zeros.at[pl.ds(0, size), :],
                            xs_hbm.at[pl.ds(pl.multiple_of(end + done, RUN_ALIGN), size), :], sem))

        group_tails(lambda cp: cp.start())
        group_tails(lambda cp: cp.wait())


def _scatter(run_meta, tail, hn, pos, n_rows):
    T = hn.shape[0]
    grid_spec = pltpu.PrefetchScalarGridSpec(
        num_scalar_prefetch=4,
        grid=(T // TM_POST,),
        in_specs=[
            pl.BlockSpec((TM_POST, D_MODEL), lambda i, *_: (i, 0)),
            pl.BlockSpec((TOP_K, TM_POST), lambda i, *_: (0, i)),
        ],
        out_specs=pl.BlockSpec(memory_space=pl.ANY),
        scratch_shapes=[pltpu.VMEM((2, STAGE_ROWS, D_MODEL), F32), pltpu.VMEM((TM_EXP, D_MODEL), F32),
                        pltpu.SemaphoreType.DMA((2,))],
    )
    return pl.pallas_call(
        _scatter_kernel,
        grid_spec=grid_spec,
        out_shape=jax.ShapeDtypeStruct((n_rows, D_MODEL), F32),
        compiler_params=pltpu.CompilerParams(
            dimension_semantics=("arbitrary",), vmem_limit_bytes=VMEM_LIMIT),
        name="scatter",
    )(*run_meta, tail, hn, pos)


def _expert_kernel(exp_ref, active_ref, newexp_ref, nextexp_ref, wslot_ref,
                   xs_ref, bg_ref, bl_ref, bd_ref, perm_ref, wgu_hbm, wd_hbm, y_ref,
                   wgu_buf, wd_buf, wg_s, wl_s, wd_s, wsem):
    w = pl.program_id(0)

    def fetch(e, s):
        return (pltpu.make_async_copy(wgu_hbm.at[e], wgu_buf.at[s], wsem.at[0, s]),
                pltpu.make_async_copy(wd_hbm.at[e], wd_buf.at[s], wsem.at[1, s]))

    @pl.when(newexp_ref[w] == 1)
    def _():
        s = wslot_ref[w]

        @pl.when(w == 0)
        def _():
            for cp in fetch(exp_ref[w], s):
                cp.start()

        for cp in fetch(exp_ref[w], s):
            cp.wait()

        @pl.when(nextexp_ref[w] >= 0)
        def _():
            for cp in fetch(nextexp_ref[w], 1 - s):
                cp.start()

        perm = perm_ref[...]
        for grp in range(2 * D_EXPERT // (2 * LANES)):
            blk = wgu_buf[s, :, grp * 2 * LANES:(grp + 1) * 2 * LANES].astype(BF16)
            res = _dot(blk, perm)
            wg_s[:, grp * LANES:(grp + 1) * LANES] = res[:, :LANES].astype(BF16)
            wl_s[:, grp * LANES:(grp + 1) * LANES] = res[:, LANES:].astype(BF16)
        wd_s[...] = wd_buf[s].astype(BF16)

    @pl.when(active_ref[w] == 0)
    def _():
        y_ref[...] = jnp.zeros(y_ref.shape, F32)

    @pl.when(active_ref[w] == 1)
    def _():
        x = xs_ref[...].astype(BF16)
        g = _dot(x, wg_s[...]) + bg_ref[0]
        l = _dot(x, wl_s[...]) + bl_ref[0]
        glu = jnp.minimum(g, SWIGLU_LIMIT)
        lin = jnp.clip(l, -SWIGLU_LIMIT, SWIGLU_LIMIT)
        act = glu * (1.0 / (1.0 + jnp.exp(-SWIGLU_ALPHA * glu))) * (lin + 1.0)
        y_ref[...] = _dot(act.astype(BF16), wd_s[...]) + bd_ref[0]


def _experts(meta, xs, wgu, bg, bl, wd, bd):
    R = xs.shape[0]
    n_items = R // TM_EXP
    by_tile = lambda w, exp, *_: (w, 0)
    by_exp = lambda w, exp, *_: (exp[w], 0, 0)
    perm = np.zeros((2 * LANES, 2 * LANES), np.float32)
    perm[2 * np.arange(LANES), np.arange(LANES)] = 1.0
    perm[2 * np.arange(LANES) + 1, LANES + np.arange(LANES)] = 1.0
    perm = jnp.asarray(perm, BF16)
    grid_spec = pltpu.PrefetchScalarGridSpec(
        num_scalar_prefetch=5,
        grid=(n_items,),
        in_specs=[
            pl.BlockSpec((TM_EXP, D_MODEL), by_tile),
            pl.BlockSpec((1, 1, D_EXPERT), by_exp),
            pl.BlockSpec((1, 1, D_EXPERT), by_exp),
            pl.BlockSpec((1, 1, D_MODEL), by_exp),
            pl.BlockSpec(perm.shape, lambda w, *_: (0, 0)),
            pl.BlockSpec(memory_space=pl.ANY),
            pl.BlockSpec(memory_space=pl.ANY),
        ],
        out_specs=pl.BlockSpec((TM_EXP, D_MODEL), by_tile),
        scratch_shapes=[pltpu.VMEM((2, D_MODEL, 2 * D_EXPERT), F32), pltpu.VMEM((2, D_EXPERT, D_MODEL), F32),
                        pltpu.VMEM((D_MODEL, D_EXPERT), BF16), pltpu.VMEM((D_MODEL, D_EXPERT), BF16),
                        pltpu.VMEM((D_EXPERT, D_MODEL), BF16), pltpu.SemaphoreType.DMA((2, 2))],
    )
    return pl.pallas_call(
        _expert_kernel,
        grid_spec=grid_spec,
        out_shape=jax.ShapeDtypeStruct((R, D_MODEL), F32),
        compiler_params=pltpu.CompilerParams(
            dimension_semantics=("arbitrary",), vmem_limit_bytes=VMEM_LIMIT),
        name="experts",
    )(*meta, xs, bg, bl, bd, perm, wgu, wd)


def _combine_kernel(c8_ref, lb_ref, gb_ref, h2_ref, gate_ref, pos_ref, lnf_ref, y_hbm, out_ref, stage, sem):
    i = pl.program_id(0)
    n = pl.num_programs(0)
    slot = lax.rem(i, 2)
    nxt = 1 - slot

    def copy(s, l, g, size):
        return pltpu.make_async_copy(y_hbm.at[pl.ds(g, size), :], stage.at[s, pl.ds(l, size), :], sem.at[s])

    @pl.when(i == 0)
    def _():
        stage[...] = jnp.zeros(stage.shape, F32)
        _for_each_chunk(c8_ref, lb_ref, gb_ref, 0, lambda l, g, size: copy(0, l, g, size).start())

    @pl.when(i + 1 < n)
    def _():
        _for_each_chunk(c8_ref, lb_ref, gb_ref, i + 1, lambda l, g, size: copy(nxt, l, g, size).start())

    _for_each_chunk(c8_ref, lb_ref, gb_ref, i, lambda l, g, size: copy(slot, l, g, size).wait())

    cols = lax.broadcasted_iota(jnp.int32, (TM_POST, STAGE_ROWS), 1)
    pos = pos_ref[...]
    gates = gate_ref[...]
    w = jnp.zeros((TM_POST, STAGE_ROWS), F32)
    for k in range(TOP_K):
        w = jnp.where(cols == pos[:, k:k + 1], gates[:, k:k + 1], w)
    moe = _dot(w.astype(BF16), stage[slot].astype(BF16))
    out_ref[...] = _rms(h2_ref[...] + moe) * lnf_ref[...]


def _combine(run_meta, h2, gates_t, pos_t, lnf, y):
    T = h2.shape[0]
    grid_spec = pltpu.PrefetchScalarGridSpec(
        num_scalar_prefetch=3,
        grid=(T // TM_POST,),
        in_specs=[
            pl.BlockSpec((TM_POST, D_MODEL), lambda i, *_: (i, 0)),
            pl.BlockSpec((TM_POST, TOP_K), lambda i, *_: (i, 0)),
            pl.BlockSpec((TM_POST, TOP_K), lambda i, *_: (i, 0)),
            pl.BlockSpec(lnf.shape, lambda i, *_: (0, 0)),
            pl.BlockSpec(memory_space=pl.ANY),
        ],
        out_specs=pl.BlockSpec((TM_POST, D_MODEL), lambda i, *_: (i, 0)),
        scratch_shapes=[pltpu.VMEM((2, STAGE_ROWS, D_MODEL), F32), pltpu.SemaphoreType.DMA((2,))],
    )
    return pl.pallas_call(
        _combine_kernel,
        grid_spec=grid_spec,
        out_shape=jax.ShapeDtypeStruct((T, D_MODEL), F32),
        compiler_params=pltpu.CompilerParams(
            dimension_semantics=("arbitrary",), vmem_limit_bytes=VMEM_LIMIT),
        name="combine",
    )(*run_meta, h2, gates_t, pos_t, lnf, y)


def _rope_tables(S):
    half = MLA_ROPE // 2
    inv = ROPE_THETA ** (-jnp.arange(half, dtype=F32) / half)
    ang = jnp.arange(S, dtype=F32)[:, None] * inv[None, :]
    cos, sin = jnp.cos(ang), jnp.sin(ang)
    z = lambda n: jnp.zeros((S, n), F32)
    pad = HEAD_LANES - MLA_NOPE - MLA_ROPE
    c_tab = jnp.concatenate([jnp.ones((S, MLA_NOPE), F32), cos, cos, z(pad)], axis=1)
    s_tab = jnp.concatenate([z(MLA_NOPE), sin, sin, z(pad)], axis=1)
    ck = jnp.concatenate([z(MLA_NOPE), cos, cos, z(pad)], axis=1)
    skn = jnp.concatenate([z(MLA_NOPE), -sin, z(half), z(pad)], axis=1)
    skp = jnp.concatenate([z(MLA_NOPE), z(half), sin, z(pad)], axis=1)
    return c_tab, s_tab, (ck, skn, skp)


def _pad_cols(w, left, width=HEAD_LANES):
    return jnp.pad(w, ((0, 0), (left, width - left - w.shape[1])))


def _pack_mixer_weights(w_in, w_q_up, w_kv_up):
    half = MLA_ROPE // 2
    cuts = np.cumsum([MLA_Q_RANK, MLA_KV_RANK, MLA_ROPE, 512, 512, 512])
    w_ql, w_kvl, w_kpe, w_fq, w_fk, w_fv, w_fl = jnp.split(w_in, cuts, axis=1)
    win = jnp.concatenate([
        w_ql, w_kvl, _pad_cols(w_kpe, MLA_NOPE) + _pad_cols(w_fl, 0),
        w_fq, w_fk, w_fv], axis=1).astype(BF16)

    dq = MLA_NOPE + MLA_ROPE
    wq = w_q_up.reshape(MLA_Q_RANK, MLA_HEADS, dq)
    zq = lambda n: jnp.zeros((MLA_Q_RANK, MLA_HEADS, n), F32)
    wqm = jnp.concatenate([wq, zq(HEAD_LANES - dq)], axis=2)
    wqs = jnp.concatenate([zq(MLA_NOPE), -wq[:, :, MLA_NOPE + half:], wq[:, :, MLA_NOPE:MLA_NOPE + half],
                           zq(HEAD_LANES - dq)], axis=2)
    wkv = w_kv_up.reshape(MLA_KV_RANK, MLA_HEADS, MLA_NOPE + MLA_V)
    zk = lambda n: jnp.zeros((MLA_KV_RANK, MLA_HEADS, n), F32)
    wkk = jnp.concatenate([wkv[:, :, :MLA_NOPE], zk(HEAD_LANES - MLA_NOPE)], axis=2)
    v_even = jnp.concatenate([wkv[:, :, MLA_NOPE:], zk(HEAD_LANES - MLA_V)], axis=2)
    v_odd = jnp.concatenate([zk(HEAD_LANES - MLA_V), wkv[:, :, MLA_NOPE:]], axis=2)
    odd = (jnp.arange(MLA_HEADS) % 2 == 1)[None, :, None]
    wkvv = jnp.where(odd, v_odd, v_even)
    flat = lambda w: w.reshape(w.shape[0], MLA_HEADS * HEAD_LANES).astype(BF16)
    return win, flat(wqm), flat(wqs), flat(wkk), flat(wkvv)


def _fox_placement():
    eq = np.zeros((LANES, FOX_HEADS * HEAD_LANES), np.float32)
    ek = np.zeros((LANES, FOX_HEADS * HEAD_LANES), np.float32)
    oq = np.zeros((1, FOX_HEADS * HEAD_LANES), np.float32)
    ok = np.zeros((1, FOX_HEADS * HEAD_LANES), np.float32)
    for h in range(FOX_HEADS):
        base = h * HEAD_LANES + (FOX_HEAD_DIM if h % 2 == 0 else 0)
        for part in range(3):
            eq[part * FOX_HEADS + h, base + part] = 1.0
            ok[0, base + part] = 1.0
            oq[0, base + 3 + part] = 1.0
            ek[part * FOX_HEADS + h, base + 3 + part] = -1.0
    return jnp.asarray(eq, BF16), jnp.asarray(ek, BF16), jnp.asarray(oq), jnp.asarray(ok)


def _row_tiles(rows_per_expert, n_rows):
    n_tiles = n_rows // TM_EXP
    tiles_per = (rows_per_expert + TM_EXP - 1) // TM_EXP
    tile_end = jnp.cumsum(tiles_per)
    starts = ((tile_end - tiles_per) * TM_EXP).astype(jnp.int32)
    w = jnp.arange(n_tiles, dtype=jnp.int32)
    experts = jnp.arange(N_EXPERTS, dtype=jnp.int32)
    e = jnp.minimum(jnp.sum(tile_end[None, :] <= w[:, None], axis=1), N_EXPERTS - 1).astype(jnp.int32)
    active = w < tile_end[-1]
    last_e = jnp.max(jnp.where(rows_per_expert > 0, experts, 0))
    e = jnp.where(active, e, last_e)
    prev_e = jnp.concatenate([jnp.full((1,), -1, jnp.int32), e[:-1]])
    newexp = (e != prev_e).astype(jnp.int32)
    has = tiles_per > 0
    later = jnp.logical_and(has[None, :], experts[None, :] > experts[:, None])
    next_of = jnp.min(jnp.where(later, experts[None, :], N_EXPERTS), axis=1)
    next_of = jnp.where(next_of < N_EXPERTS, next_of, -1)
    slot_of = (jnp.cumsum(has.astype(jnp.int32)) - 1) % 2
    hot = e[:, None] == experts[None, :]
    pick = lambda table: jnp.sum(jnp.where(hot, table[None, :], 0), axis=1).astype(jnp.int32)
    return (e, active.astype(jnp.int32), newexp, pick(next_of), pick(slot_of)), starts


def kernel(x, mem, ln_mix, w_in, q_norm, w_q_up, kv_norm, w_kv_up, b_forget, g_out_mla, g_out_fox, w_o,
           ln_cross, mem_norm, w_xq, w_mem_kv, w_xo, ln_ffn, w_router, b_router, w_gate_up, b_gate_up,
           w_down, b_down, ln_final):
    B, S, _ = x.shape
    T = B * S
    row = lambda v: v.reshape(1, -1).astype(F32)

    win, wqm, wqs, wkk, wkvv = _pack_mixer_weights(w_in[0], w_q_up[0], w_kv_up[0])
    c_tab, s_tab, key_tabs = _rope_tables(S)
    q_scale = (MLA_NOPE + MLA_ROPE) ** -0.5 * LOG2E
    eq, ek, oq, ok = _fox_placement()
    ltri = jnp.asarray(np.tril(np.ones((TM_PRE, TM_PRE), np.float32)), BF16)
    bf = _pad_cols(row(b_forget[0]), 0)
    q, k, v = _premix(x, row(ln_mix[0]), win, row(q_norm[0]), wqm, wqs, row(kv_norm[0]), wkk, wkvv,
                      c_tab * q_scale, s_tab * q_scale, *key_tabs, bf, ltri, eq, ek, oq, ok)
    o = _attention(q, k, v)

    mkv = _memkv(mem, row(mem_norm[0]), w_mem_kv[0].astype(BF16))
    wr = w_router[0].T
    wr_hi = wr.astype(BF16)
    wr_lo = (wr - wr_hi.astype(F32)).astype(BF16)
    br = jnp.broadcast_to(b_router[0].astype(F32)[:, None], (N_EXPERTS, LANES))
    u = jnp.asarray(np.triu(np.ones((TM_POST, TM_POST), np.float32), 1), BF16)
    h2, hn3, idx, rank, gates, tcnt = _postmix(
        o, x, row(g_out_mla[0]), row(g_out_fox[0]), w_o[0].astype(BF16), row(ln_cross[0]),
        w_xq[0].astype(BF16), mkv, w_xo[0].astype(BF16), row(ln_ffn[0]), wr_hi, wr_lo, br, u)

    counts = tcnt[:, :, 0].astype(jnp.int32)
    c8 = (counts + RUN_ALIGN - 1) // RUN_ALIGN * RUN_ALIGN
    lb = jnp.cumsum(c8, axis=1) - c8
    worst_rows = T * TOP_K + (T // TM_POST) * N_EXPERTS * (RUN_ALIGN - 1) + N_EXPERTS * (TM_EXP - RUN_ALIGN)
    n_rows = -(-worst_rows // TM_EXP) * TM_EXP
    group_rows = jnp.sum(c8, axis=0)
    meta, starts = _row_tiles(group_rows, n_rows)
    gb = starts[None, :] + jnp.cumsum(c8, axis=0) - c8
    run_meta = (c8.reshape(-1), lb.reshape(-1), gb.reshape(-1))
    ends = starts + group_rows
    used_tiles = jnp.sum((group_rows + TM_EXP - 1) // TM_EXP)
    tail = jnp.concatenate([ends, (-ends) % TM_EXP, used_tiles[None]]).astype(jnp.int32)
    lb_tok = jnp.repeat(lb, TM_POST, axis=0)
    pos = rank + jnp.sum(jnp.where(idx[..., None] == jnp.arange(N_EXPERTS, dtype=jnp.int32),
                                   lb_tok[None], 0), axis=-1)

    xs = _scatter(run_meta, tail, hn3.reshape(T, D_MODEL), pos, n_rows)
    bgu = b_gate_up[0].reshape(N_EXPERTS, 1, D_EXPERT, 2).astype(F32)
    y = _experts(meta, xs, w_gate_up[0], bgu[..., 0], bgu[..., 1],
                 w_down[0], b_down[0].reshape(N_EXPERTS, 1, D_MODEL).astype(F32))

    out = _combine(run_meta, h2.reshape(T, D_MODEL), gates.T, pos.T, row(ln_final), y)
    return out.reshape(B, S, D_MODEL)
```
